```python
import math
import jax, jax.numpy as jnp
from jax import lax
import numpy as np

D_MODEL = 1024
BATCH = 8
SEQ = 4096
DEPTH = 2

MLA_HEADS = 8
MLA_Q_LORA = 384
MLA_KV_LORA = 256
MLA_DN = 64
MLA_DR = 32
MLA_DV = 64
ROPE_THETA = 10000.0
Q_BLOCK = 128
SSM_HEADS = 16
SSM_HEAD_DIM = 64
SSM_D_INNER = SSM_HEADS * SSM_HEAD_DIM
SSM_GROUPS = 4
SSM_STATE = 64
SSM_CONV = 5
SSM_CHUNK = 128
SSM_CONV_DIM = SSM_D_INNER + 2 * SSM_GROUPS * SSM_STATE
CNV_CH = 512
CNV_WIDTH = 31
N_EXPERTS = 32
TOP_K = 4
D_FF = 1024
SWIGLU_LIMIT = 7.0
SWIGLU_ALPHA = 1.702
MOE_BLOCK = 128
DN_ALPHA = (2 * DEPTH) ** 0.25
DN_BETA = (8 * DEPTH) ** -0.25
IN_SIZES = (MLA_Q_LORA, MLA_KV_LORA, MLA_DR,
            SSM_D_INNER, SSM_CONV_DIM, 2 * SSM_HEADS,
            CNV_CH, CNV_CH,
            3 * D_MODEL)
N_IN = sum(IN_SIZES)

kernel_name = "hybrid_mla_ssd_conformer_moe_deepnorm"


def _split(t, sizes):
    offs = np.cumsum(sizes)[:-1].tolist()
    return jnp.split(t, offs, axis=-1)


def _layernorm(x, g, b, eps=1e-5):
    xf = x.astype(jnp.float32)
    mu = jnp.mean(xf, -1, keepdims=True)
    var = jnp.mean(jnp.square(xf - mu), -1, keepdims=True)
    return ((xf - mu) * lax.rsqrt(var + eps) * g + b).astype(x.dtype)


def _rmsnorm(x, g, eps=1e-6):
    xf = x.astype(jnp.float32)
    return (xf * lax.rsqrt(jnp.mean(xf * xf, -1, keepdims=True) + eps) * g).astype(x.dtype)


def _rope_tables(seq):
    pos = jnp.arange(seq, dtype=jnp.float32)
    inv = ROPE_THETA ** (-jnp.arange(0, MLA_DR, 2, dtype=jnp.float32) / MLA_DR)
    ang = pos[:, None] * inv[None, :]
    return jnp.cos(ang), jnp.sin(ang)


def _apply_rope(t, cos, sin):
    t1, t2 = jnp.split(t, 2, axis=-1)
    return jnp.concatenate([t1 * cos - t2 * sin, t1 * sin + t2 * cos], -1).astype(t.dtype)


def _dwconv(x, w, b):
    width = w.shape[0]
    y = lax.conv_general_dilated(x, w[:, None, :].astype(x.dtype), window_strides=(1,),
                                 padding=[((width - 1) // 2, width // 2)],
                                 dimension_numbers=('NWC', 'WIO', 'NWC'),
                                 feature_group_count=x.shape[-1])
    return y + b


def _mla(c_q, c_kv, k_rope, q_norm, kv_norm, w_uq, w_ukv, cos, sin):
    bsz, s, _ = c_q.shape
    q = (_rmsnorm(c_q, q_norm) @ w_uq).reshape(bsz, s, MLA_HEADS, MLA_DN + MLA_DR)
    q_nope = q[..., :MLA_DN]
    q_rope = _apply_rope(q[..., MLA_DN:], cos[:, None, :], sin[:, None, :])
    kv = (_rmsnorm(c_kv, kv_norm) @ w_ukv).reshape(bsz, s, MLA_HEADS, MLA_DN + MLA_DV)
    k_nope, v = kv[..., :MLA_DN], kv[..., MLA_DN:]
    k_rope = _apply_rope(k_rope, cos, sin)
    scale = (MLA_DN + MLA_DR) ** -0.5
    nb = s // Q_BLOCK
    qn = q_nope.reshape(bsz, nb, Q_BLOCK, MLA_HEADS, MLA_DN).transpose(1, 0, 2, 3, 4)
    qr = q_rope.reshape(bsz, nb, Q_BLOCK, MLA_HEADS, MLA_DR).transpose(1, 0, 2, 3, 4)

    def block(args):
        qn_b, qr_b = args
        sc = (jnp.einsum('bqhd,bkhd->bhqk', qn_b, k_nope)
              + jnp.einsum('bqhd,bkd->bhqk', qr_b, k_rope))
        p = jax.nn.softmax(sc.astype(jnp.float32) * scale, axis=-1).astype(v.dtype)
        return jnp.einsum('bhqk,bkhd->bqhd', p, v)

    o = lax.map(block, (qn, qr))
    return o.transpose(1, 0, 2, 3, 4).reshape(bsz, s, MLA_HEADS * MLA_DV)


def _ssd(xh, dta, bm, cm):
    bsz, s, g, r, p = xh.shape
    nc, l = s // SSM_CHUNK, SSM_CHUNK
    x = xh.reshape(bsz, nc, l, g, r, p)
    bc = bm.reshape(bsz, nc, l, g, -1)
    cc = cm.reshape(bsz, nc, l, g, -1)
    a = dta.astype(jnp.float32).reshape(bsz, nc, l, g, r).transpose(0, 3, 4, 1, 2)
    a_cs = jnp.cumsum(a, axis=-1)
    seg = a_cs[..., :, None] - a_cs[..., None, :]
    lower = jnp.tril(jnp.ones((l, l), dtype=bool))
    decay = jnp.exp(jnp.where(lower, seg, -jnp.inf)).astype(x.dtype)
    cb = jnp.einsum('bclgn,bcsgn->bcgls', cc, bc)
    y_diag = jnp.einsum('bcgls,bgrcls,bcsgrp->bclgrp', cb, decay, x)
    decay_states = jnp.exp(a_cs[..., -1:] - a_cs).astype(x.dtype)
    states = jnp.einsum('bclgn,bgrcl,bclgrp->bcgrpn', bc, decay_states, x)
    chunk_decay = jnp.exp(a_cs[..., -1])

    def step(h, inp):
        st, dec = inp
        return h * dec[..., None, None] + st, h

    h0 = jnp.zeros(states.shape[:1] + states.shape[2:], jnp.float32)
    _, h_in = lax.scan(step, h0, (states.transpose(1, 0, 2, 3, 4, 5).astype(jnp.float32),
                                  chunk_decay.transpose(3, 0, 1, 2)))
    h_in = h_in.transpose(1, 0, 2, 3, 4, 5).astype(x.dtype)
    y_off = jnp.einsum('bclgn,bcgrpn,bgrcl->bclgrp', cc, h_in, jnp.exp(a_cs).astype(x.dtype))
    return (y_diag + y_off).reshape(bsz, s, g, r, p)


def _mamba2(z, xbc, dt_raw, conv_w, conv_b, dt_bias, a_log, d_skip, norm_g):
    bsz, s, _ = z.shape
    g, r = SSM_GROUPS, SSM_HEADS // SSM_GROUPS
    xbc = jax.nn.silu(_dwconv(xbc, conv_w, conv_b))
    xs, bm, cm = _split(xbc, (SSM_D_INNER, SSM_GROUPS * SSM_STATE, SSM_GROUPS * SSM_STATE))
    xh = xs.reshape(bsz, s, g, r, SSM_HEAD_DIM)
    bm = bm.reshape(bsz, s, g, SSM_STATE)
    cm = cm.reshape(bsz, s, g, SSM_STATE)
    dt = jax.nn.softplus(dt_raw.reshape(bsz, s, 2, SSM_HEADS) + dt_bias)
    dta = dt * (-jnp.exp(a_log))

    def direction(i, flip):
        f = (lambda t: jnp.flip(t, axis=1)) if flip else (lambda t: t)
        dt_i = dt[:, :, i].reshape(bsz, s, g, r)
        y = _ssd(f(xh * dt_i[..., None]), f(dta[:, :, i].reshape(bsz, s, g, r)), f(bm), f(cm))
        return f(y)

    y = direction(0, False) + direction(1, True) + xh * d_skip.reshape(g, r)[..., None]
    y = y.reshape(bsz, s, SSM_D_INNER) * jax.nn.silu(z)
    y = _rmsnorm(y.reshape(bsz, s, g, -1), norm_g.reshape(g, -1))
    return y.reshape(bsz, s, SSM_D_INNER)


def _conformer_conv(a, gt, dw_w, dw_b, ln_g, ln_b):
    u = a * jax.nn.sigmoid(gt)
    u = _dwconv(u, dw_w, dw_b)
    return jax.nn.silu(_layernorm(u, ln_g, ln_b))


def _moe(t, router_w, router_b, w_gu, b_gu, w_dn, b_dn):
    n_tok, d = t.shape
    logits = (t @ router_w + router_b).astype(jnp.float32)
    top_vals, top_idx = lax.top_k(logits, TOP_K)
    gates = jax.nn.softmax(top_vals, axis=-1).astype(t.dtype)
    tk = n_tok * TOP_K
    n_blocks = -(-(tk + N_EXPERTS * (MOE_BLOCK - 1)) // MOE_BLOCK)
    n_rows = n_blocks * MOE_BLOCK
    flat_e = top_idx.reshape(tk).astype(jnp.int32)
    flat_tok = jnp.repeat(jnp.arange(n_tok, dtype=jnp.int32), TOP_K)
    flat_gate = gates.reshape(tk)
    order = jnp.argsort(flat_e)
    se = flat_e[order]
    counts = jnp.bincount(flat_e, length=N_EXPERTS).astype(jnp.int32)
    padded = (counts + MOE_BLOCK - 1) // MOE_BLOCK * MOE_BLOCK
    pad_end = jnp.cumsum(padded)
    pad_start = pad_end - padded
    start = jnp.cumsum(counts) - counts
    dest = pad_start[se] + jnp.arange(tk, dtype=jnp.int32) - start[se]
    row_tok = jnp.zeros((n_rows,), jnp.int32).at[dest].set(flat_tok[order])
    row_gate = jnp.zeros((n_rows,), t.dtype).at[dest].set(flat_gate[order])
    blk_e = jnp.minimum(jnp.searchsorted(pad_end, jnp.arange(n_blocks, dtype=jnp.int32) * MOE_BLOCK,
                                         side='right'), N_EXPERTS - 1)
    xb = t[row_tok].reshape(n_blocks, MOE_BLOCK, d)

    def expert_block(args):
        xe, e = args
        h = xe @ w_gu[e] + b_gu[e]
        gate, up = h[:, :D_FF], h[:, D_FF:]
        gate = jnp.minimum(gate, SWIGLU_LIMIT)
        up = jnp.clip(up, -SWIGLU_LIMIT, SWIGLU_LIMIT)
        return ((up + 1) * (gate * jax.nn.sigmoid(SWIGLU_ALPHA * gate))) @ w_dn[e] + b_dn[e]

    yb = lax.map(expert_block, (xb, blk_e)).reshape(n_rows, d)
    return jnp.zeros_like(t).at[row_tok].add(yb * row_gate[:, None])


def setup_inputs(seed: int = 0) -> dict:
    key = jax.random.key(seed)
    ks = iter(jax.random.split(key, 48))
    L = DEPTH
    f32 = jnp.float32

    def nrm(shape, fan_in, scale=1.0):
        return jax.random.normal(next(ks), shape, f32) * (scale * fan_in ** -0.5)

    def gain(shape):
        return 1.0 + 0.02 * jax.random.normal(next(ks), shape, f32)

    def bias(shape, scale=0.02):
        return scale * jax.random.normal(next(ks), shape, f32)

    x = jax.random.normal(next(ks), (BATCH, SEQ, D_MODEL), f32)
    dt0 = jnp.exp(jax.random.uniform(next(ks), (L, 2, SSM_HEADS), f32,
                                     math.log(1e-3), math.log(1e-1)))
    dt_bias = dt0 + jnp.log(-jnp.expm1(-dt0))
    a_log = jnp.log(jax.random.uniform(next(ks), (L, 2, SSM_HEADS), f32, 1.0, 16.0))
    return {
        "x": x,
        "w_in": nrm((L, D_MODEL, N_IN), D_MODEL),
        "b_in": bias((L, N_IN)),
        "mla_q_norm": gain((L, MLA_Q_LORA)),
        "mla_kv_norm": gain((L, MLA_KV_LORA)),
        "mla_w_uq": nrm((L, MLA_Q_LORA, MLA_HEADS * (MLA_DN + MLA_DR)), MLA_Q_LORA),
        "mla_w_ukv": nrm((L, MLA_KV_LORA, MLA_HEADS * (MLA_DN + MLA_DV)), MLA_KV_LORA),
        "w_br_attn": nrm((L, MLA_HEADS * MLA_DV, D_MODEL), MLA_HEADS * MLA_DV, DN_BETA),
        "ssm_conv_w": nrm((L, SSM_CONV, SSM_CONV_DIM), SSM_CONV),
        "ssm_conv_b": bias((L, SSM_CONV_DIM)),
        "ssm_dt_bias": dt_bias,
        "ssm_a_log": a_log,
        "ssm_d": 1.0 + 0.1 * jax.random.normal(next(ks), (L, SSM_HEADS), f32),
        "ssm_norm": gain((L, SSM_D_INNER)),
        "w_br_ssm": nrm((L, SSM_D_INNER, D_MODEL), SSM_D_INNER, DN_BETA),
        "cnv_dw_w": nrm((L, CNV_WIDTH, CNV_CH), CNV_WIDTH),
        "cnv_dw_b": bias((L, CNV_CH)),
        "cnv_ln_g": gain((L, CNV_CH)),
        "cnv_ln_b": bias((L, CNV_CH)),
        "w_br_conv": nrm((L, CNV_CH, D_MODEL), CNV_CH, DN_BETA),
        "b_br_conv": bias((L, D_MODEL)),
        "w_out": nrm((L, D_MODEL, D_MODEL), D_MODEL, DN_BETA),
        "ln1_g": gain((L, D_MODEL)),
        "ln1_b": bias((L, D_MODEL)),
        "router_w": nrm((L, D_MODEL, N_EXPERTS), D_MODEL),
        "router_b": bias((L, N_EXPERTS), 0.01),
        "moe_w_gate_up": nrm((L, N_EXPERTS, D_MODEL, 2 * D_FF), D_MODEL),
        "moe_b_gate_up": bias((L, N_EXPERTS, 2 * D_FF)),
        "moe_w_down": nrm((L, N_EXPERTS, D_FF, D_MODEL), D_FF, DN_BETA),
        "moe_b_down": bias((L, N_EXPERTS, D_MODEL)),
        "ln2_g": gain((L, D_MODEL)),
        "ln2_b": bias((L, D_MODEL)),
    }


def reference(x, w_in, b_in, mla_q_norm, mla_kv_norm, mla_w_uq, mla_w_ukv, w_br_attn,
              ssm_conv_w, ssm_conv_b, ssm_dt_bias, ssm_a_log, ssm_d, ssm_norm, w_br_ssm,
              cnv_dw_w, cnv_dw_b, cnv_ln_g, cnv_ln_b, w_br_conv, b_br_conv,
              w_out, ln1_g, ln1_b, router_w, router_b, moe_w_gate_up, moe_b_gate_up,
              moe_w_down, moe_b_down, ln2_g, ln2_b):
    bsz, s, d = x.shape
    cos, sin = _rope_tables(s)
    for l in range(DEPTH):
        proj = x @ w_in[l] + b_in[l]
        (c_q, c_kv, k_rope, z, xbc, dt_raw, cnv_a, cnv_g, gate_logits) = _split(proj, IN_SIZES)
        y_attn = _mla(c_q, c_kv, k_rope, mla_q_norm[l], mla_kv_norm[l], mla_w_uq[l],
                      mla_w_ukv[l], cos, sin) @ w_br_attn[l]
        y_ssm = _mamba2(z, xbc, dt_raw, ssm_conv_w[l], ssm_conv_b[l], ssm_dt_bias[l],
                        ssm_a_log[l], ssm_d[l], ssm_norm[l]) @ w_br_ssm[l]
        y_conv = _conformer_conv(cnv_a, cnv_g, cnv_dw_w[l], cnv_dw_b[l], cnv_ln_g[l],
                                 cnv_ln_b[l]) @ w_br_conv[l] + b_br_conv[l]
        g_attn, g_ssm, g_conv = jnp.split(jax.nn.sigmoid(gate_logits), 3, axis=-1)
        mixed = (g_attn * y_attn + g_ssm * y_ssm + g_conv * y_conv) @ w_out[l]
        x = _layernorm(DN_ALPHA * x + mixed, ln1_g[l], ln1_b[l])
        ffn = _moe(x.reshape(bsz * s, d), router_w[l], router_b[l], moe_w_gate_up[l],
                   moe_b_gate_up[l], moe_w_down[l], moe_b_down[l]).reshape(bsz, s, d)
        x = _layernorm(DN_ALPHA * x + ffn, ln2_g[l], ln2_b[l])
    return x
```

```python
import functools
import math

import numpy as np
import jax
import jax.numpy as jnp
from jax import lax
from jax.experimental import pallas as pl
from jax.experimental.pallas import tpu as pltpu

F32 = jnp.float32
BF16 = jnp.bfloat16
HIGHEST = lax.Precision.HIGHEST

LANES = 128
SUBLANES = 8
VMEM_LIMIT_BYTES = 56 * 1024 * 1024

D_MODEL = 1024
DEPTH = 2
MLA_HEADS = 8
MLA_Q_LORA = 384
MLA_KV_LORA = 256
MLA_DN = 64
MLA_DR = 32
MLA_DV = 64
ROPE_THETA = 10000.0
SSM_HEADS = 16
SSM_HEAD_DIM = 64
SSM_D_INNER = SSM_HEADS * SSM_HEAD_DIM
SSM_GROUPS = 4
SSM_STATE = 64
SSM_CONV = 5
SSM_CHUNK = 128
SSM_BC = SSM_GROUPS * SSM_STATE
SSM_CONV_DIM = SSM_D_INNER + 2 * SSM_BC
CNV_CH = 512
CNV_WIDTH = 31
N_EXPERTS = 32
TOP_K = 4
D_FF = 1024
SWIGLU_LIMIT = 7.0
SWIGLU_ALPHA = 1.702
DN_ALPHA = (2 * DEPTH) ** 0.25
IN_SIZES = (MLA_Q_LORA, MLA_KV_LORA, MLA_DR, SSM_D_INNER, SSM_CONV_DIM, 2 * SSM_HEADS,
            CNV_CH, CNV_CH, 3 * D_MODEL)
HEAD_PAD = LANES
MLA_W = MLA_HEADS * HEAD_PAD
GROUP_A = 1024
CONV_HALO = 16
MOE_ROWS = 256
ROW_TILES = D_MODEL // LANES
NEG_BIG = -1e30


def _params(*sem):
    return pltpu.CompilerParams(dimension_semantics=sem, vmem_limit_bytes=VMEM_LIMIT_BYTES)


def _linear_kernel(x_ref, w_ref, b_ref, o_ref):
    acc = jnp.dot(x_ref[...], w_ref[...], preferred_element_type=F32)
    o_ref[...] = (acc + b_ref[...]).astype(o_ref.dtype)


def _linear(x, w, b, out_dtype, tm=512, tn=None):
    m, k = x.shape
    n = w.shape[1]
    tm = min(tm, m)
    tn = n if tn is None else tn
    return pl.pallas_call(
        _linear_kernel,
        grid=(n // tn, m // tm),
        in_specs=[pl.BlockSpec((tm, k), lambda j, i: (i, 0)),
                  pl.BlockSpec((k, tn), lambda j, i: (0, j)),
                  pl.BlockSpec((1, tn), lambda j, i: (0, j))],
        out_specs=pl.BlockSpec((tm, tn), lambda j, i: (i, j)),
        out_shape=jax.ShapeDtypeStruct((m, n), out_dtype),
        compiler_params=_params("parallel", "parallel"),
        name="linear",
    )(x, w, b)


def _rms(x, g, eps=1e-6):
    return x * lax.rsqrt(jnp.mean(x * x, -1, keepdims=True) + eps) * g


def _mla_prep_kernel(a_ref, gq_ref, gkv_ref, wq_ref, wqs_ref, wk_ref, wv_ref, e2_ref,
                     cq_ref, sq_ref, tk_ref, vone_ref, q_ref, k_ref, v_ref):
    a = a_ref[...]
    c_q = a[:, :MLA_Q_LORA]
    c_kv = a[:, MLA_Q_LORA:MLA_Q_LORA + MLA_KV_LORA]
    kr = a[:, MLA_Q_LORA + MLA_KV_LORA:]
    qn = _rms(c_q, gq_ref[...]).astype(BF16)
    kvn = _rms(c_kv, gkv_ref[...]).astype(BF16)
    q = (jnp.dot(qn, wq_ref[...], preferred_element_type=F32) * cq_ref[...]
         + jnp.dot(qn, wqs_ref[...], preferred_element_type=F32) * sq_ref[...])
    q_ref[...] = q.astype(BF16)
    krp = (kr * tk_ref[...]).astype(BF16)
    k = (jnp.dot(kvn, wk_ref[...], preferred_element_type=F32)
         + jnp.dot(krp, e2_ref[...], preferred_element_type=F32))
    k_ref[...] = k.astype(BF16)
    v = jnp.dot(kvn, wv_ref[...], preferred_element_type=F32) + vone_ref[...]
    v_ref[...] = v.astype(BF16)


def _mla_prep(a, seq, gq, gkv, wq, wqs, wk, wv, e2, cq, sq, tk, vone, tm=512):
    t = a.shape[0]
    tm = min(tm, seq)
    nper = seq // tm
    wa = MLA_Q_LORA + MLA_KV_LORA + LANES

    def const(shape):
        return pl.BlockSpec(shape, lambda i: (0, 0))

    def tab(width):
        return pl.BlockSpec((tm, width), lambda i: (i % nper, 0))

    out = jax.ShapeDtypeStruct((t, MLA_W), BF16)
    return pl.pallas_call(
        _mla_prep_kernel,
        grid=(t // tm,),
        in_specs=[pl.BlockSpec((tm, wa), lambda i: (i, 0)),
                  const((1, MLA_Q_LORA)), const((1, MLA_KV_LORA)),
                  const((MLA_Q_LORA, MLA_W)), const((MLA_Q_LORA, MLA_W)),
                  const((MLA_KV_LORA, MLA_W)), const((MLA_KV_LORA, MLA_W)),
                  const((LANES, MLA_W)),
                  tab(MLA_W), tab(MLA_W), tab(LANES), const((1, MLA_W))],
        out_specs=[pl.BlockSpec((tm, MLA_W), lambda i: (i, 0))] * 3,
        out_shape=[out, out, out],
        compiler_params=_params("parallel"),
        name="mla_prep",
    )(a, gq, gkv, wq, wqs, wk, wv, e2, cq, sq, tk, vone)


def _attn_kernel(q_ref, k_ref, v_ref, o_ref):
    s = lax.dot_general(q_ref[0], k_ref[0], (((1,), (1,)), ((), ())),
                        preferred_element_type=F32)
    m = jnp.max(s, -1, keepdims=True)
    p = jnp.exp(s - m).astype(BF16)
    o = jnp.dot(p, v_ref[0], preferred_element_type=F32)
    o_ref[0] = (o / o[:, MLA_DV:MLA_DV + 1]).astype(o_ref.dtype)


def _attention(q, k, v, tq=256):
    b, s, _ = q.shape
    tq = min(tq, s)
    return pl.pallas_call(
        _attn_kernel,
        grid=(b, MLA_HEADS, s // tq),
        in_specs=[pl.BlockSpec((1, tq, HEAD_PAD), lambda bi, h, i: (bi, i, h)),
                  pl.BlockSpec((1, s, HEAD_PAD), lambda bi, h, i: (bi, 0, h)),
                  pl.BlockSpec((1, s, HEAD_PAD), lambda bi, h, i: (bi, 0, h))],
        out_specs=pl.BlockSpec((1, tq, HEAD_PAD), lambda bi, h, i: (bi, i, h)),
        out_shape=jax.ShapeDtypeStruct((b, s, MLA_W), BF16),
        compiler_params=_params("parallel", "parallel", "parallel"),
        name="attention",
    )(q, k, v)


def _dwconv_kernel(*refs, width, glu, silu_out, seq, rows):
    if glu:
        a_ref, g_ref, w_ref, b_ref, o_ref, pad_ref = refs
        pre = a_ref[0] * jax.nn.sigmoid(g_ref[0])
    else:
        x_ref, w_ref, b_ref, o_ref, pad_ref = refs
        pre = x_ref[0]
    ch = o_ref.shape[-1]
    halo = jnp.zeros((CONV_HALO, ch), F32)
    pad_ref[0:CONV_HALO, :] = halo
    pad_ref[CONV_HALO + seq:2 * CONV_HALO + seq, :] = halo
    pad_ref[CONV_HALO:CONV_HALO + seq, :] = pre
    half = (width - 1) // 2
    win_rows = rows + 2 * CONV_HALO

    def body(c, carry):
        base = pl.multiple_of(c * rows, rows)
        win = pad_ref[pl.ds(base, win_rows), :]
        acc = jnp.zeros((rows, ch), F32) + b_ref[...]
        for t in range(width):
            off = CONV_HALO - half + t
            acc = acc + pltpu.roll(win, win_rows - off, 0)[:rows] * w_ref[t:t + 1, :]
        if silu_out:
            acc = acc * jax.nn.sigmoid(acc)
        o_ref[0, pl.ds(base, rows), :] = acc
        return carry

    lax.fori_loop(0, seq // rows, body, 0)


def _dwconv(x, w, b, *, glu, silu_out, rows=64):
    bsz, seq, cin = x.shape
    width, ch = w.shape
    nct = ch // LANES
    kern = functools.partial(_dwconv_kernel, width=width, glu=glu, silu_out=silu_out, seq=seq,
                             rows=min(rows, seq))
    xspec = pl.BlockSpec((1, seq, LANES), lambda bi, j: (bi, 0, j))
    in_specs = [xspec]
    args = [x]
    if glu:
        in_specs.append(pl.BlockSpec((1, seq, LANES), lambda bi, j: (bi, 0, j + nct)))
        args.append(x)
    in_specs += [pl.BlockSpec((width, LANES), lambda bi, j: (0, j)),
                 pl.BlockSpec((1, LANES), lambda bi, j: (0, j))]
    return pl.pallas_call(
        kern,
        grid=(bsz, nct),
        in_specs=in_specs,
        out_specs=pl.BlockSpec((1, seq, LANES), lambda bi, j: (bi, 0, j)),
        out_shape=jax.ShapeDtypeStruct((bsz, seq, ch), F32),
        scratch_shapes=[pltpu.VMEM((seq + 2 * CONV_HALO, LANES), F32)],
        compiler_params=_params("parallel", "parallel"),
        name="dwconv",
    )(*args, w, b.reshape(1, ch))


def _softplus(x):
    return jnp.maximum(x, 0.0) + jnp.log1p(jnp.exp(-jnp.abs(x)))


def _ssd_kernel(xs_ref, bm_ref, cm_ref, dt_ref, dtb_ref, nega_ref, y_ref, h_ref, *, reverse):
    ln = SSM_CHUNK

    @pl.when(pl.program_id(1) == 0)
    def _():
        h_ref[...] = jnp.zeros_like(h_ref)

    dt = _softplus(dt_ref[0] + dtb_ref[...])
    a = dt * nega_ref[...]
    row = lax.broadcasted_iota(jnp.int32, (ln, ln), 0)
    col = lax.broadcasted_iota(jnp.int32, (ln, ln), 1)
    tri = (col >= row) if reverse else (col <= row)
    acs = jnp.dot(tri.astype(F32), a, precision=HIGHEST, preferred_element_type=F32)
    tot = acs[0:1] if reverse else acs[ln - 1:ln]
    wst = dt * jnp.exp(tot - acs)
    etot = jnp.exp(tot)
    acs_t = acs.T
    dt_t = dt.T
    wst_t = wst.T

    xs_b = xs_ref[0].astype(BF16)
    cm_b = cm_ref[0].astype(BF16)
    bm_t = bm_ref[0].T
    bm_tb = bm_t.astype(BF16)
    y_off = jnp.dot(cm_b, h_ref[...].astype(BF16), preferred_element_type=F32)

    glane = lax.broadcasted_iota(jnp.int32, (ln, SSM_BC), 1) // SSM_STATE
    lo = lax.broadcasted_iota(jnp.int32, (ln, LANES), 1) < SSM_HEAD_DIM
    lo_s = lax.broadcasted_iota(jnp.int32, (SSM_STATE, LANES), 1) < SSM_HEAD_DIM
    lo_1 = lax.broadcasted_iota(jnp.int32, (1, LANES), 1) < SSM_HEAD_DIM
    heads_per_group = SSM_HEADS // SSM_GROUPS
    for g in range(SSM_GROUPS):
        cmg = jnp.where(glane == g, cm_b, jnp.zeros_like(cm_b))
        cb = jnp.dot(cmg, bm_tb, preferred_element_type=F32)
        bm_tg = bm_t[g * SSM_STATE:(g + 1) * SSM_STATE, :]
        for j in range(heads_per_group // 2):
            pair = g * (heads_per_group // 2) + j
            lanes = slice(pair * LANES, (pair + 1) * LANES)
            xp = xs_b[:, lanes]
            diag, ecol, st, et = [], [], [], []
            for hh in range(2):
                h = 2 * pair + hh
                colb = jnp.broadcast_to(acs[:, h:h + 1], (ln, LANES))
                seg = colb - acs_t[h:h + 1, :]
                decay = jnp.exp(jnp.where(tri, seg, -jnp.inf))
                mat = (cb * decay * dt_t[h:h + 1, :]).astype(BF16)
                diag.append(jnp.dot(mat, xp, preferred_element_type=F32))
                ecol.append(jnp.exp(colb))
                st.append(jnp.dot((bm_tg * wst_t[h:h + 1, :]).astype(BF16), xp,
                                  preferred_element_type=F32))
                et.append(jnp.broadcast_to(etot[:, h:h + 1], (1, LANES)))
            y_ref[0, :, lanes] = (jnp.where(lo, diag[0], diag[1])
                                  + y_off[:, lanes] * jnp.where(lo, ecol[0], ecol[1]))
            rows = slice(g * SSM_STATE, (g + 1) * SSM_STATE)
            h_ref[rows, lanes] = (h_ref[rows, lanes] * jnp.where(lo_1, et[0], et[1])
                                  + jnp.where(lo_s, st[0], st[1]))


def _ssd(xbc, a_grp, dt_bias, neg_a, *, reverse):
    bsz, seq, _ = xbc.shape
    nc = seq // SSM_CHUNK
    dt_tile = 7 if reverse else 6

    def cidx(c):
        return (nc - 1 - c) if reverse else c

    return pl.pallas_call(
        functools.partial(_ssd_kernel, reverse=reverse),
        grid=(bsz, nc),
        in_specs=[pl.BlockSpec((1, SSM_CHUNK, SSM_D_INNER), lambda b, c: (b, cidx(c), 0)),
                  pl.BlockSpec((1, SSM_CHUNK, SSM_BC), lambda b, c: (b, cidx(c), SSM_D_INNER // SSM_BC)),
                  pl.BlockSpec((1, SSM_CHUNK, SSM_BC), lambda b, c: (b, cidx(c), SSM_D_INNER // SSM_BC + 1)),
                  pl.BlockSpec((1, SSM_CHUNK, LANES), lambda b, c: (b, cidx(c), dt_tile)),
                  pl.BlockSpec((1, LANES), lambda b, c: (0, 0)),
                  pl.BlockSpec((1, LANES), lambda b, c: (0, 0))],
        out_specs=pl.BlockSpec((1, SSM_CHUNK, SSM_D_INNER), lambda b, c: (b, cidx(c), 0)),
        out_shape=jax.ShapeDtypeStruct((bsz, seq, SSM_D_INNER), F32),
        scratch_shapes=[pltpu.VMEM((SSM_BC, SSM_D_INNER), F32)],
        compiler_params=_params("parallel", "arbitrary"),
        name="ssd_bwd" if reverse else "ssd_fwd",
    )(xbc, xbc, xbc, a_grp, dt_bias, neg_a)


def _layernorm(x, g, b, eps=1e-5):
    mu = jnp.mean(x, -1, keepdims=True)
    xc = x - mu
    var = jnp.mean(xc * xc, -1, keepdims=True)
    return xc * lax.rsqrt(var + eps) * g + b


def _merge_kernel(x_ref, o_ref, yf_ref, yb_ref, xs_ref, z_ref, u_ref, gl_ref,
                  wa_ref, ws_ref, wc_ref, wo_ref, dsk_ref, ng_ref, cg_ref, cb_ref, bc_ref,
                  l1g_ref, l1b_ref, rw_ref, rb_ref,
                  x1_ref, x1b_ref, x1g_ref, route_ref, cnt_ref, carry_ref):
    tm = x_ref.shape[0]

    @pl.when(pl.program_id(0) == 0)
    def _():
        carry_ref[...] = jnp.zeros_like(carry_ref)

    y_attn = jnp.dot(o_ref[...], wa_ref[...], preferred_element_type=F32)
    z = z_ref[...].astype(F32)
    ys = (yf_ref[...] + yb_ref[...] + xs_ref[...] * dsk_ref[...]) * (z * jax.nn.sigmoid(z))
    gw = SSM_D_INNER // SSM_GROUPS
    ys = jnp.concatenate(
        [_rms(ys[:, g * gw:(g + 1) * gw], ng_ref[:, g * gw:(g + 1) * gw]) for g in range(SSM_GROUPS)], -1)
    y_ssm = jnp.dot(ys.astype(BF16), ws_ref[...], preferred_element_type=F32)
    uc = _layernorm(u_ref[...], cg_ref[...], cb_ref[...])
    uc = uc * jax.nn.sigmoid(uc)
    y_conv = jnp.dot(uc.astype(BF16), wc_ref[...], preferred_element_type=F32) + bc_ref[...]
    gl = gl_ref[...].astype(F32)
    mixed = (jax.nn.sigmoid(gl[:, :D_MODEL]) * y_attn
             + jax.nn.sigmoid(gl[:, D_MODEL:2 * D_MODEL]) * y_ssm
             + jax.nn.sigmoid(gl[:, 2 * D_MODEL:]) * y_conv)
    mixed = jnp.dot(mixed.astype(BF16), wo_ref[...], preferred_element_type=F32)
    x1 = _layernorm(DN_ALPHA * x_ref[...] + mixed, l1g_ref[...], l1b_ref[...])
    x1_ref[...] = x1
    x1b_ref[...] = x1.astype(BF16)
    for t in range(ROW_TILES):
        x1g_ref[pl.ds(t, tm, stride=ROW_TILES), :] = x1[:, t * LANES:(t + 1) * LANES]

    lg = jnp.dot(x1, rw_ref[...], precision=HIGHEST, preferred_element_type=F32) + rb_ref[...]
    lane = lax.broadcasted_iota(jnp.int32, (tm, LANES), 1).astype(F32)
    sels, vals, idxs = [], [], []
    for _ in range(TOP_K):
        m = jnp.max(lg, -1, keepdims=True)
        idx = jnp.min(jnp.where(lg == m, lane, float(LANES)), -1, keepdims=True)
        sel = lane == idx
        lg = jnp.where(sel, NEG_BIG * 2, lg)
        sels.append(sel)
        vals.append(m)
        idxs.append(idx)
    es = [jnp.exp(v - vals[0]) for v in vals]
    den = es[0] + es[1] + es[2] + es[3]
    hot = jnp.zeros((tm, LANES), F32)
    for sel in sels:
        hot = hot + sel.astype(F32)
    r = lax.broadcasted_iota(jnp.int32, (tm, tm), 0)
    c = lax.broadcasted_iota(jnp.int32, (tm, tm), 1)
    excl = jnp.dot((c < r).astype(BF16), hot.astype(BF16), preferred_element_type=F32) + carry_ref[...]
    route = jnp.zeros((tm, LANES), F32)
    for kk in range(TOP_K):
        rank = jnp.sum(jnp.where(sels[kk], excl, 0.0), -1, keepdims=True)
        route = jnp.where(lane == float(kk), idxs[kk], route)
        route = jnp.where(lane == float(TOP_K + kk), rank, route)
        route = jnp.where(lane == float(2 * TOP_K + kk), es[kk] / den, route)
    route_ref[...] = route
    carry_ref[...] = carry_ref[...] + jnp.sum(hot, 0, keepdims=True)
    cnt_ref[...] = carry_ref[...]


def _merge(x, o, yf, yb, xbc, z, u, gl, wa, ws, wc, wo, dsk, ng, cg, cb, bc, l1g, l1b, rw, rb, tm=256):
    t = x.shape[0]
    tm = min(tm, t)

    def rowb(width, col=0):
        return pl.BlockSpec((tm, width), lambda i: (i, col))

    def const(shape):
        return pl.BlockSpec(shape, lambda i: (0, 0))

    return pl.pallas_call(
        _merge_kernel,
        grid=(t // tm,),
        in_specs=[rowb(D_MODEL), rowb(MLA_W), rowb(SSM_D_INNER), rowb(SSM_D_INNER), rowb(SSM_D_INNER),
                  rowb(SSM_D_INNER), rowb(CNV_CH), rowb(3 * D_MODEL),
                  const((MLA_W, D_MODEL)), const((SSM_D_INNER, D_MODEL)), const((CNV_CH, D_MODEL)),
                  const((D_MODEL, D_MODEL)), const((1, SSM_D_INNER)), const((1, SSM_D_INNER)),
                  const((1, CNV_CH)), const((1, CNV_CH)), const((1, D_MODEL)),
                  const((1, D_MODEL)), const((1, D_MODEL)), const((D_MODEL, LANES)), const((1, LANES))],
        out_specs=[rowb(D_MODEL), rowb(D_MODEL),
                   pl.BlockSpec((tm * ROW_TILES, LANES), lambda i: (i, 0)),
                   rowb(LANES), const((1, LANES))],
        out_shape=[jax.ShapeDtypeStruct((t, D_MODEL), F32), jax.ShapeDtypeStruct((t, D_MODEL), BF16),
                   jax.ShapeDtypeStruct((t * ROW_TILES, LANES), F32),
                   jax.ShapeDtypeStruct((t, LANES), F32), jax.ShapeDtypeStruct((1, LANES), F32)],
        scratch_shapes=[pltpu.VMEM((1, LANES), F32)],
        compiler_params=_params("arbitrary"),
        name="merge",
    )(x, o, yf, yb, xbc, z, u, gl, wa, ws, wc, wo, dsk, ng, cg, cb, bc, l1g, l1b, rw, rb)


def _gather_rows(idx_ref, count, src_hbm, dst, sem):
    def body(r, carry):
        tok = idx_ref[0, 0, r]
        pltpu.make_async_copy(src_hbm.at[pl.ds(pl.multiple_of(tok * ROW_TILES, ROW_TILES), ROW_TILES)],
                              dst.at[pl.ds(pl.multiple_of(r * ROW_TILES, ROW_TILES), ROW_TILES)],
                              sem).start()
        return carry
    lax.fori_loop(0, count, body, 0)


def _wait_rows(src_hbm, dst, sem):
    pltpu.make_async_copy(src_hbm.at[pl.ds(0, dst.shape[0])], dst, sem).wait()


def _from_row_tiles(ref, rows):
    return jnp.concatenate([ref[pl.ds(t, rows, stride=ROW_TILES), :] for t in range(ROW_TILES)], -1)


def _moe_kernel(blk_e_ref, nused_ref, tok_first_ref, tok_next_ref, x_hbm, wgu_ref, bgu_ref, wdn_ref,
                bdn_ref, gate_ref, o_ref, buf, sem):
    i = pl.program_id(0)
    nused = nused_ref[0]
    slot = i % 2

    @pl.when(i == 0)
    def _():
        _gather_rows(tok_first_ref, MOE_ROWS, x_hbm, buf.at[0], sem.at[0])

    @pl.when(i + 1 < nused)
    def _():
        _gather_rows(tok_next_ref, MOE_ROWS, x_hbm, buf.at[1 - slot], sem.at[1 - slot])

    @pl.when(i < nused)
    def _():
        _wait_rows(x_hbm, buf.at[slot], sem.at[slot])
        xb = _from_row_tiles(buf.at[slot], MOE_ROWS).astype(BF16)
        h = jnp.dot(xb, wgu_ref[0], preferred_element_type=F32) + bgu_ref[0]
        gate = jnp.minimum(h[:, :D_FF], SWIGLU_LIMIT)
        up = jnp.clip(h[:, D_FF:], -SWIGLU_LIMIT, SWIGLU_LIMIT)
        act = (up + 1.0) * (gate * jax.nn.sigmoid(SWIGLU_ALPHA * gate))
        y = jnp.dot(act.astype(BF16), wdn_ref[0], preferred_element_type=F32) + bdn_ref[0]
        y = y * gate_ref[...]
        for t in range(ROW_TILES):
            o_ref[pl.ds(t, MOE_ROWS, stride=ROW_TILES), :] = y[:, t * LANES:(t + 1) * LANES]

    @pl.when(i >= nused)
    def _():
        o_ref[...] = jnp.zeros_like(o_ref)


def _moe_experts(blk_e, row_tok, nused, x1g, wgu, bgu, wdn, bdn, row_gate):
    n_rows = row_tok.shape[0]
    n_blocks = n_rows // MOE_ROWS
    tok3 = row_tok.reshape(n_blocks, 1, MOE_ROWS)
    grid_spec = pltpu.PrefetchScalarGridSpec(
        num_scalar_prefetch=2,
        grid=(n_blocks,),
        in_specs=[pl.BlockSpec((1, 1, MOE_ROWS), lambda i, be, nu: (0, 0, 0), memory_space=pltpu.SMEM),
                  pl.BlockSpec((1, 1, MOE_ROWS), lambda i, be, nu: (jnp.minimum(i + 1, n_blocks - 1), 0, 0),
                               memory_space=pltpu.SMEM),
                  pl.BlockSpec(memory_space=pl.ANY),
                  pl.BlockSpec((1, D_MODEL, 2 * D_FF), lambda i, be, nu: (be[i], 0, 0)),
                  pl.BlockSpec((1, 1, 2 * D_FF), lambda i, be, nu: (be[i], 0, 0)),
                  pl.BlockSpec((1, D_FF, D_MODEL), lambda i, be, nu: (be[i], 0, 0)),
                  pl.BlockSpec((1, 1, D_MODEL), lambda i, be, nu: (be[i], 0, 0)),
                  pl.BlockSpec((MOE_ROWS, 1), lambda i, be, nu: (i, 0))],
        out_specs=pl.BlockSpec((MOE_ROWS * ROW_TILES, LANES), lambda i, be, nu: (i, 0)),
        scratch_shapes=[pltpu.VMEM((2, MOE_ROWS * ROW_TILES, LANES), F32),
                        pltpu.SemaphoreType.DMA((2,))],
    )
    return pl.pallas_call(
        _moe_kernel,
        grid_spec=grid_spec,
        out_shape=jax.ShapeDtypeStruct((n_rows * ROW_TILES, LANES), F32),
        compiler_params=_params("arbitrary"),
        name="moe_experts",
    )(blk_e, nused, tok3, tok3, x1g, wgu, bgu, wdn, bdn, row_gate)


def _combine_kernel(dest_first_ref, dest_next_ref, yb_hbm, x1_ref, g_ref, b_ref, x2_ref, x2b_ref, buf, sem,
                    *, tm):
    i = pl.program_id(0)
    n = pl.num_programs(0)
    slot = i % 2
    per_step = tm * TOP_K

    @pl.when(i == 0)
    def _():
        _gather_rows(dest_first_ref, per_step, yb_hbm, buf.at[0], sem.at[0])

    @pl.when(i + 1 < n)
    def _():
        _gather_rows(dest_next_ref, per_step, yb_hbm, buf.at[1 - slot], sem.at[1 - slot])

    _wait_rows(yb_hbm, buf.at[slot], sem.at[slot])
    ffn = jnp.zeros((tm, D_MODEL), F32)
    for kk in range(TOP_K):
        ffn = ffn + jnp.concatenate(
            [buf[slot, pl.ds(kk * ROW_TILES + t, tm, stride=TOP_K * ROW_TILES), :] for t in range(ROW_TILES)], -1)
    x2 = _layernorm(DN_ALPHA * x1_ref[...] + ffn, g_ref[...], b_ref[...])
    x2_ref[...] = x2
    x2b_ref[...] = x2.astype(BF16)


def _combine(dest, yb, x1, g, b, tm=256):
    t = x1.shape[0]
    tm = min(tm, t)
    nt = t // tm
    dest3 = dest.reshape(nt, 1, tm * TOP_K)
    return pl.pallas_call(
        functools.partial(_combine_kernel, tm=tm),
        grid=(nt,),
        in_specs=[pl.BlockSpec((1, 1, tm * TOP_K), lambda i: (0, 0, 0), memory_space=pltpu.SMEM),
                  pl.BlockSpec((1, 1, tm * TOP_K), lambda i: (jnp.minimum(i + 1, nt - 1), 0, 0),
                               memory_space=pltpu.SMEM),
                  pl.BlockSpec(memory_space=pl.ANY),
                  pl.BlockSpec((tm, D_MODEL), lambda i: (i, 0)),
                  pl.BlockSpec((1, D_MODEL), lambda i: (0, 0)),
                  pl.BlockSpec((1, D_MODEL), lambda i: (0, 0))],
        out_specs=[pl.BlockSpec((tm, D_MODEL), lambda i: (i, 0)),
                   pl.BlockSpec((tm, D_MODEL), lambda i: (i, 0))],
        out_shape=[jax.ShapeDtypeStruct((t, D_MODEL), F32), jax.ShapeDtypeStruct((t, D_MODEL), BF16)],
        scratch_shapes=[pltpu.VMEM((2, tm * TOP_K * ROW_TILES, LANES), F32),
                        pltpu.SemaphoreType.DMA((2,))],
        compiler_params=_params("arbitrary"),
        name="combine",
    )(dest3, dest3, yb, x1, g, b)


def _pad_cols(w, width):
    return jnp.pad(w, ((0, 0), (0, width - w.shape[1])))


def _pack_in_proj(w_in, b_in):
    offs = np.concatenate([[0], np.cumsum(IN_SIZES)])
    wb = jnp.concatenate([w_in, b_in[None, :]], 0)

    def piece(i):
        return wb[:, offs[i]:offs[i + 1]]

    kr = piece(2)
    kr_swapped = jnp.concatenate([kr[:, MLA_DR // 2:], kr[:, :MLA_DR // 2]], 1)
    dt = piece(5)
    grp_a = jnp.concatenate([piece(0), piece(1), _pad_cols(jnp.concatenate([kr, kr_swapped], 1), LANES),
                             _pad_cols(dt[:, :SSM_HEADS], LANES), _pad_cols(dt[:, SSM_HEADS:], LANES)], 1)
    groups = {"a": (grp_a, F32), "z": (piece(3), BF16), "xbc": (piece(4), F32),
              "cnv": (jnp.concatenate([piece(6), piece(7)], 1), F32), "gate": (piece(8), BF16)}
    return {k: (v[:-1].astype(BF16), v[-1:], dt_) for k, (v, dt_) in groups.items()}


def _pack_mla(w_uq, w_ukv, w_br_attn, seq):
    hq = MLA_DN + MLA_DR
    half = MLA_DR // 2
    wq = w_uq.reshape(MLA_Q_LORA, MLA_HEADS, hq)
    zq = jnp.zeros((MLA_Q_LORA, MLA_HEADS, HEAD_PAD - hq), F32)
    wq_main = jnp.concatenate([wq, zq], -1).reshape(MLA_Q_LORA, MLA_W)
    wq_swap = jnp.concatenate([jnp.zeros((MLA_Q_LORA, MLA_HEADS, MLA_DN), F32),
                               wq[..., MLA_DN + half:], wq[..., MLA_DN:MLA_DN + half], zq],
                              -1).reshape(MLA_Q_LORA, MLA_W)
    wkv = w_ukv.reshape(MLA_KV_LORA, MLA_HEADS, MLA_DN + MLA_DV)
    zk = jnp.zeros((MLA_KV_LORA, MLA_HEADS, HEAD_PAD - MLA_DN), F32)
    wk = jnp.concatenate([wkv[..., :MLA_DN], zk], -1).reshape(MLA_KV_LORA, MLA_W)
    wv = jnp.concatenate([wkv[..., MLA_DN:], zk], -1).reshape(MLA_KV_LORA, MLA_W)
    e2 = np.zeros((LANES, MLA_HEADS, HEAD_PAD), np.float32)
    for j in range(MLA_DR):
        e2[j, :, MLA_DN + j] = 1.0
        e2[MLA_DR + j, :, MLA_DN + j] = 1.0
    vone = np.zeros((MLA_HEADS, HEAD_PAD), np.float32)
    vone[:, MLA_DV] = 1.0
    wbr = jnp.concatenate([w_br_attn.reshape(MLA_HEADS, MLA_DV, D_MODEL),
                           jnp.zeros((MLA_HEADS, HEAD_PAD - MLA_DV, D_MODEL), F32)], 1).reshape(MLA_W, D_MODEL)
    pos = jnp.arange(seq, dtype=F32)
    inv = ROPE_THETA ** (-jnp.arange(0, MLA_DR, 2, dtype=F32) / MLA_DR)
    ang = pos[:, None] * inv[None, :]
    cos, sin = jnp.cos(ang), jnp.sin(ang)
    scale = (MLA_DN + MLA_DR) ** -0.5
    ones = jnp.ones((seq, MLA_DN), F32)
    zpad = jnp.zeros((seq, HEAD_PAD - hq), F32)
    cq = jnp.tile(jnp.concatenate([ones, cos, cos, zpad], 1) * scale, (1, MLA_HEADS))
    sq = jnp.tile(jnp.concatenate([0 * ones, -sin, sin, zpad], 1) * scale, (1, MLA_HEADS))
    tk = jnp.concatenate([cos, cos, -sin, sin, jnp.zeros((seq, LANES - 2 * MLA_DR), F32)], 1)
    return dict(wq=wq_main.astype(BF16), wqs=wq_swap.astype(BF16), wk=wk.astype(BF16), wv=wv.astype(BF16),
                e2=jnp.asarray(e2.reshape(LANES, MLA_W), BF16), vone=jnp.asarray(vone.reshape(1, MLA_W)),
                wbr=wbr.astype(BF16), cq=cq, sq=sq, tk=tk)


def _route_tables(route, cnt, n_tok):
    idx = route[:, :TOP_K].astype(jnp.int32)
    rank = route[:, TOP_K:2 * TOP_K].astype(jnp.int32)
    gate = route[:, 2 * TOP_K:3 * TOP_K]
    counts = cnt[0, :N_EXPERTS].astype(jnp.int32)
    n_blocks = -(-(n_tok * TOP_K + N_EXPERTS * (MOE_ROWS - 1)) // MOE_ROWS)
    n_rows = n_blocks * MOE_ROWS
    padded = (counts + MOE_ROWS - 1) // MOE_ROWS * MOE_ROWS
    pad_end = jnp.cumsum(padded)
    pad_start = pad_end - padded
    dest = (pad_start[idx] + rank).reshape(-1)
    tok = jnp.repeat(jnp.arange(n_tok, dtype=jnp.int32), TOP_K)
    row_tok = jnp.zeros((n_rows,), jnp.int32).at[dest].set(tok)
    row_gate = jnp.zeros((n_rows,), F32).at[dest].set(gate.reshape(-1))
    blk_e = jnp.minimum(jnp.searchsorted(pad_end, jnp.arange(n_blocks, dtype=jnp.int32) * MOE_ROWS,
                                         side="right"), N_EXPERTS - 1).astype(jnp.int32)
    nused = (pad_end[-1:] // MOE_ROWS).astype(jnp.int32)
    return dest.astype(jnp.int32), row_tok, row_gate.reshape(n_rows, 1), blk_e, nused


def kernel(x, w_in, b_in, mla_q_norm, mla_kv_norm, mla_w_uq, mla_w_ukv, w_br_attn, ssm_conv_w, ssm_conv_b, ssm_dt_bias, ssm_a_log, ssm_d, ssm_norm, w_br_ssm, cnv_dw_w, cnv_dw_b, cnv_ln_g, cnv_ln_b, w_br_conv, b_br_conv, w_out, ln1_g, ln1_b, router_w, router_b, moe_w_gate_up, moe_b_gate_up, moe_w_down, moe_b_down, ln2_g, ln2_b):
    bsz, seq, d = x.shape
    n_tok = bsz * seq
    xf = x.reshape(n_tok, d)
    xb = xf.astype(BF16)
    for l in range(DEPTH):
        proj = _pack_in_proj(w_in[l], b_in[l])
        mla = _pack_mla(mla_w_uq[l], mla_w_ukv[l], w_br_attn[l], seq)
        a_grp = _linear(xb, *proj["a"])
        z = _linear(xb, *proj["z"])
        xbc_raw = _linear(xb, *proj["xbc"], tn=768)
        cnv_raw = _linear(xb, *proj["cnv"])
        gate_logits = _linear(xb, *proj["gate"], tn=1024)

        q, k, v = _mla_prep(a_grp, seq, mla_q_norm[l][None], mla_kv_norm[l][None], mla["wq"], mla["wqs"],
                            mla["wk"], mla["wv"], mla["e2"], mla["cq"], mla["sq"], mla["tk"], mla["vone"])
        attn = _attention(q.reshape(bsz, seq, MLA_W), k.reshape(bsz, seq, MLA_W), v.reshape(bsz, seq, MLA_W))

        xbc = _dwconv(xbc_raw.reshape(bsz, seq, SSM_CONV_DIM), ssm_conv_w[l], ssm_conv_b[l],
                      glu=False, silu_out=True)
        a3 = a_grp.reshape(bsz, seq, GROUP_A)
        dtb = jnp.pad(ssm_dt_bias[l], ((0, 0), (0, LANES - SSM_HEADS)))
        nega = jnp.pad(-jnp.exp(ssm_a_log[l]), ((0, 0), (0, LANES - SSM_HEADS)))
        y_fwd = _ssd(xbc, a3, dtb[0:1], nega[0:1], reverse=False)
        y_bwd = _ssd(xbc, a3, dtb[1:2], nega[1:2], reverse=True)

        u = _dwconv(cnv_raw.reshape(bsz, seq, 2 * CNV_CH), cnv_dw_w[l], cnv_dw_b[l], glu=True, silu_out=False)

        rw = jnp.pad(router_w[l], ((0, 0), (0, LANES - N_EXPERTS)))
        rb = jnp.pad(router_b[l], (0, LANES - N_EXPERTS), constant_values=NEG_BIG)[None]
        x1, x1b, x1g, route, cnt = _merge(
            xf, attn.reshape(n_tok, MLA_W), y_fwd.reshape(n_tok, SSM_D_INNER), y_bwd.reshape(n_tok, SSM_D_INNER),
            xbc.reshape(n_tok, SSM_CONV_DIM), z, u.reshape(n_tok, CNV_CH), gate_logits,
            mla["wbr"], w_br_ssm[l].astype(BF16), w_br_conv[l].astype(BF16), w_out[l].astype(BF16),
            jnp.repeat(ssm_d[l], SSM_HEAD_DIM)[None], ssm_norm[l][None], cnv_ln_g[l][None], cnv_ln_b[l][None],
            b_br_conv[l][None], ln1_g[l][None], ln1_b[l][None], rw, rb)

        dest, row_tok, row_gate, blk_e, nused = _route_tables(route, cnt, n_tok)
        yb = _moe_experts(blk_e, row_tok, nused, x1g, moe_w_gate_up[l].astype(BF16),
                          moe_b_gate_up[l][:, None, :], moe_w_down[l].astype(BF16),
                          moe_b_down[l][:, None, :], row_gate)
        xf, xb = _combine(dest, yb, x1, ln2_g[l][None], ln2_b[l][None])
    return xf.reshape(bsz, seq, d)
```

```python
import functools
import math

import numpy as np
import jax
import jax.numpy as jnp
from jax import lax
from jax.experimental import pallas as pl
from jax.experimental.pallas import tpu as pltpu

F32 = jnp.float32
BF16 = jnp.bfloat16
HIGHEST = lax.Precision.HIGHEST

LANES = 128
SUBLANES = 8
VMEM_LIMIT_BYTES = 56 * 1024 * 1024

D_MODEL = 1024
DEPTH = 2
MLA_HEADS = 8
MLA_Q_LORA = 384
MLA_KV_LORA = 256
MLA_DN = 64
MLA_DR = 32
MLA_DV = 64
ROPE_THETA = 10000.0
SSM_HEADS = 16
SSM_HEAD_DIM = 64
SSM_D_INNER = SSM_HEADS * SSM_HEAD_DIM
SSM_GROUPS = 4
SSM_STATE = 64
SSM_CONV = 5
SSM_CHUNK = 128
SSM_BC = SSM_GROUPS * SSM_STATE
SSM_CONV_DIM = SSM_D_INNER + 2 * SSM_BC
CNV_CH = 512
CNV_WIDTH = 31
N_EXPERTS = 32
TOP_K = 4
D_FF = 1024
SWIGLU_LIMIT = 7.0
SWIGLU_ALPHA = 1.702
DN_ALPHA = (2 * DEPTH) ** 0.25
IN_SIZES = (MLA_Q_LORA, MLA_KV_LORA, MLA_DR, SSM_D_INNER, SSM_CONV_DIM, 2 * SSM_HEADS,
            CNV_CH, CNV_CH, 3 * D_MODEL)
HEAD_PAD = LANES
MLA_W = MLA_HEADS * HEAD_PAD
GROUP_A = 1024
CONV_HALO = 16
MOE_ROWS = 256
ISSUE_UNROLL = 8
ROW_TILES = D_MODEL // LANES
NEG_BIG = -1e30


def _params(*sem):
    return pltpu.CompilerParams(dimension_semantics=sem, vmem_limit_bytes=VMEM_LIMIT_BYTES)


def _linear_kernel(x_ref, w_ref, b_ref, o_ref):
    acc = jnp.dot(x_ref[...], w_ref[...], preferred_element_type=F32)
    o_ref[...] = (acc + b_ref[...]).astype(o_ref.dtype)


def _linear(x, w, b, out_dtype, tm=512, tn=None):
    m, k = x.shape
    n = w.shape[1]
    tm = min(tm, m)
    tn = n if tn is None else tn
    return pl.pallas_call(
        _linear_kernel,
        grid=(n // tn, m // tm),
        in_specs=[pl.BlockSpec((tm, k), lambda j, i: (i, 0)),
                  pl.BlockSpec((k, tn), lambda j, i: (0, j)),
                  pl.BlockSpec((1, tn), lambda j, i: (0, j))],
        out_specs=pl.BlockSpec((tm, tn), lambda j, i: (i, j)),
        out_shape=jax.ShapeDtypeStruct((m, n), out_dtype),
        compiler_params=_params("parallel", "parallel"),
        name="linear",
    )(x, w, b)


def _rms(x, g, eps=1e-6):
    return x * lax.rsqrt(jnp.mean(x * x, -1, keepdims=True) + eps) * g


def _mla_prep_kernel(a_ref, gq_ref, gkv_ref, wq_ref, wqs_ref, wk_ref, wv_ref, e2_ref,
                     cq_ref, sq_ref, tk_ref, vone_ref, q_ref, k_ref, v_ref):
    a = a_ref[...]
    c_q = a[:, :MLA_Q_LORA]
    c_kv = a[:, MLA_Q_LORA:MLA_Q_LORA + MLA_KV_LORA]
    kr = a[:, MLA_Q_LORA + MLA_KV_LORA:]
    qn = _rms(c_q, gq_ref[...]).astype(BF16)
    kvn = _rms(c_kv, gkv_ref[...]).astype(BF16)
    q = (jnp.dot(qn, wq_ref[...], preferred_element_type=F32) * cq_ref[...]
         + jnp.dot(qn, wqs_ref[...], preferred_element_type=F32) * sq_ref[...])
    q_ref[...] = q.astype(BF16)
    krp = (kr * tk_ref[...]).astype(BF16)
    k = (jnp.dot(kvn, wk_ref[...], preferred_element_type=F32)
         + jnp.dot(krp, e2_ref[...], preferred_element_type=F32))
    k_ref[...] = k.astype(BF16)
    v = jnp.dot(kvn, wv_ref[...], preferred_element_type=F32) + vone_ref[...]
    v_ref[...] = v.astype(BF16)


def _mla_prep(a, seq, gq, gkv, wq, wqs, wk, wv, e2, cq, sq, tk, vone, tm=512):
    t = a.shape[0]
    tm = min(tm, seq)
    nper = seq // tm
    wa = MLA_Q_LORA + MLA_KV_LORA + LANES

    def const(shape):
        return pl.BlockSpec(shape, lambda i: (0, 0))

    def tab(width):
        return pl.BlockSpec((tm, width), lambda i: (i % nper, 0))

    out = jax.ShapeDtypeStruct((t, MLA_W), BF16)
    return pl.pallas_call(
        _mla_prep_kernel,
        grid=(t // tm,),
        in_specs=[pl.BlockSpec((tm, wa), lambda i: (i, 0)),
                  const((1, MLA_Q_LORA)), const((1, MLA_KV_LORA)),
                  const((MLA_Q_LORA, MLA_W)), const((MLA_Q_LORA, MLA_W)),
                  const((MLA_KV_LORA, MLA_W)), const((MLA_KV_LORA, MLA_W)),
                  const((LANES, MLA_W)),
                  tab(MLA_W), tab(MLA_W), tab(LANES), const((1, MLA_W))],
        out_specs=[pl.BlockSpec((tm, MLA_W), lambda i: (i, 0))] * 3,
        out_shape=[out, out, out],
        compiler_params=_params("parallel"),
        name="mla_prep",
    )(a, gq, gkv, wq, wqs, wk, wv, e2, cq, sq, tk, vone)


def _attn_kernel(q_ref, k_ref, v_ref, o_ref):
    s = lax.dot_general(q_ref[0], k_ref[0], (((1,), (1,)), ((), ())),
                        preferred_element_type=F32)
    m = jnp.max(s, -1, keepdims=True)
    p = jnp.exp(s - m).astype(BF16)
    o = jnp.dot(p, v_ref[0], preferred_element_type=F32)
    o_ref[0] = (o / o[:, MLA_DV:MLA_DV + 1]).astype(o_ref.dtype)


def _attention(q, k, v, tq=256):
    b, s, _ = q.shape
    tq = min(tq, s)
    return pl.pallas_call(
        _attn_kernel,
        grid=(b, MLA_HEADS, s // tq),
        in_specs=[pl.BlockSpec((1, tq, HEAD_PAD), lambda bi, h, i: (bi, i, h)),
                  pl.BlockSpec((1, s, HEAD_PAD), lambda bi, h, i: (bi, 0, h)),
                  pl.BlockSpec((1, s, HEAD_PAD), lambda bi, h, i: (bi, 0, h))],
        out_specs=pl.BlockSpec((1, tq, HEAD_PAD), lambda bi, h, i: (bi, i, h)),
        out_shape=jax.ShapeDtypeStruct((b, s, MLA_W), BF16),
        compiler_params=_params("parallel", "parallel", "parallel"),
        name="attention",
    )(q, k, v)


def _dwconv_kernel(*refs, width, glu, silu_out, seq, rows):
    if glu:
        a_ref, g_ref, w_ref, b_ref, o_ref, pad_ref = refs
        pre = a_ref[0] * jax.nn.sigmoid(g_ref[0])
    else:
        x_ref, w_ref, b_ref, o_ref, pad_ref = refs
        pre = x_ref[0]
    ch = o_ref.shape[-1]
    halo = jnp.zeros((CONV_HALO, ch), F32)
    pad_ref[0:CONV_HALO, :] = halo
    pad_ref[CONV_HALO + seq:2 * CONV_HALO + seq, :] = halo
    pad_ref[CONV_HALO:CONV_HALO + seq, :] = pre
    half = (width - 1) // 2
    win_rows = rows + 2 * CONV_HALO

    def body(c, carry):
        base = pl.multiple_of(c * rows, rows)
        win = pad_ref[pl.ds(base, win_rows), :]
        acc = jnp.zeros((rows, ch), F32) + b_ref[...]
        for t in range(width):
            off = CONV_HALO - half + t
            acc = acc + pltpu.roll(win, win_rows - off, 0)[:rows] * w_ref[t:t + 1, :]
        if silu_out:
            acc = acc * jax.nn.sigmoid(acc)
        o_ref[0, pl.ds(base, rows), :] = acc
        return carry

    lax.fori_loop(0, seq // rows, body, 0)


def _dwconv(x, w, b, *, glu, silu_out, rows=64):
    bsz, seq, cin = x.shape
    width, ch = w.shape
    nct = ch // LANES
    kern = functools.partial(_dwconv_kernel, width=width, glu=glu, silu_out=silu_out, seq=seq,
                             rows=min(rows, seq))
    xspec = pl.BlockSpec((1, seq, LANES), lambda bi, j: (bi, 0, j))
    in_specs = [xspec]
    args = [x]
    if glu:
        in_specs.append(pl.BlockSpec((1, seq, LANES), lambda bi, j: (bi, 0, j + nct)))
        args.append(x)
    in_specs += [pl.BlockSpec((width, LANES), lambda bi, j: (0, j)),
                 pl.BlockSpec((1, LANES), lambda bi, j: (0, j))]
    return pl.pallas_call(
        kern,
        grid=(bsz, nct),
        in_specs=in_specs,
        out_specs=pl.BlockSpec((1, seq, LANES), lambda bi, j: (bi, 0, j)),
        out_shape=jax.ShapeDtypeStruct((bsz, seq, ch), F32),
        scratch_shapes=[pltpu.VMEM((seq + 2 * CONV_HALO, LANES), F32)],
        compiler_params=_params("parallel", "parallel"),
        name="dwconv",
    )(*args, w, b.reshape(1, ch))


def _softplus(x):
    return jnp.maximum(x, 0.0) + jnp.log1p(jnp.exp(-jnp.abs(x)))


def _ssd_kernel(xs_ref, bm_ref, cm_ref, dt_ref, dtb_ref, nega_ref, y_ref, h_ref, *, reverse):
    ln = SSM_CHUNK

    @pl.when(pl.program_id(1) == 0)
    def _():
        h_ref[...] = jnp.zeros_like(h_ref)

    dt = _softplus(dt_ref[0] + dtb_ref[...])
    a = dt * nega_ref[...]
    row = lax.broadcasted_iota(jnp.int32, (ln, ln), 0)
    col = lax.broadcasted_iota(jnp.int32, (ln, ln), 1)
    tri = (col >= row) if reverse else (col <= row)
    acs = jnp.dot(tri.astype(F32), a, precision=HIGHEST, preferred_element_type=F32)
    tot = acs[0:1] if reverse else acs[ln - 1:ln]
    wst = dt * jnp.exp(tot - acs)
    etot = jnp.exp(tot)
    acs_t = acs.T
    dt_t = dt.T
    wst_t = wst.T

    xs_b = xs_ref[0].astype(BF16)
    cm_b = cm_ref[0].astype(BF16)
    bm_t = bm_ref[0].T
    bm_tb = bm_t.astype(BF16)
    y_off = jnp.dot(cm_b, h_ref[...].astype(BF16), preferred_element_type=F32)

    glane = lax.broadcasted_iota(jnp.int32, (ln, SSM_BC), 1) // SSM_STATE
    lo = lax.broadcasted_iota(jnp.int32, (ln, LANES), 1) < SSM_HEAD_DIM
    lo_s = lax.broadcasted_iota(jnp.int32, (SSM_STATE, LANES), 1) < SSM_HEAD_DIM
    lo_1 = lax.broadcasted_iota(jnp.int32, (1, LANES), 1) < SSM_HEAD_DIM
    heads_per_group = SSM_HEADS // SSM_GROUPS
    for g in range(SSM_GROUPS):
        cmg = jnp.where(glane == g, cm_b, jnp.zeros_like(cm_b))
        cb = jnp.dot(cmg, bm_tb, preferred_element_type=F32)
        bm_tg = bm_t[g * SSM_STATE:(g + 1) * SSM_STATE, :]
        for j in range(heads_per_group // 2):
            pair = g * (heads_per_group // 2) + j
            lanes = slice(pair * LANES, (pair + 1) * LANES)
            xp = xs_b[:, lanes]
            diag, ecol, st, et = [], [], [], []
            for hh in range(2):
                h = 2 * pair + hh
                colb = jnp.broadcast_to(acs[:, h:h + 1], (ln, LANES))
                seg = colb - acs_t[h:h + 1, :]
                decay = jnp.exp(jnp.where(tri, seg, -jnp.inf))
                mat = (cb * decay * dt_t[h:h + 1, :]).astype(BF16)
                diag.append(jnp.dot(mat, xp, preferred_element_type=F32))
                ecol.append(jnp.exp(colb))
                st.append(jnp.dot((bm_tg * wst_t[h:h + 1, :]).astype(BF16), xp,
                                  preferred_element_type=F32))
                et.append(jnp.broadcast_to(etot[:, h:h + 1], (1, LANES)))
            y_ref[0, :, lanes] = (jnp.where(lo, diag[0], diag[1])
                                  + y_off[:, lanes] * jnp.where(lo, ecol[0], ecol[1]))
            rows = slice(g * SSM_STATE, (g + 1) * SSM_STATE)
            h_ref[rows, lanes] = (h_ref[rows, lanes] * jnp.where(lo_1, et[0], et[1])
                                  + jnp.where(lo_s, st[0], st[1]))


def _ssd(xbc, a_grp, dt_bias, neg_a, *, reverse):
    bsz, seq, _ = xbc.shape
    nc = seq // SSM_CHUNK
    dt_tile = 7 if reverse else 6

    def cidx(c):
        return (nc - 1 - c) if reverse else c

    return pl.pallas_call(
        functools.partial(_ssd_kernel, reverse=reverse),
        grid=(bsz, nc),
        in_specs=[pl.BlockSpec((1, SSM_CHUNK, SSM_D_INNER), lambda b, c: (b, cidx(c), 0)),
                  pl.BlockSpec((1, SSM_CHUNK, SSM_BC), lambda b, c: (b, cidx(c), SSM_D_INNER // SSM_BC)),
                  pl.BlockSpec((1, SSM_CHUNK, SSM_BC), lambda b, c: (b, cidx(c), SSM_D_INNER // SSM_BC + 1)),
                  pl.BlockSpec((1, SSM_CHUNK, LANES), lambda b, c: (b, cidx(c), dt_tile)),
                  pl.BlockSpec((1, LANES), lambda b, c: (0, 0)),
                  pl.BlockSpec((1, LANES), lambda b, c: (0, 0))],
        out_specs=pl.BlockSpec((1, SSM_CHUNK, SSM_D_INNER), lambda b, c: (b, cidx(c), 0)),
        out_shape=jax.ShapeDtypeStruct((bsz, seq, SSM_D_INNER), F32),
        scratch_shapes=[pltpu.VMEM((SSM_BC, SSM_D_INNER), F32)],
        compiler_params=_params("parallel", "arbitrary"),
        name="ssd_bwd" if reverse else "ssd_fwd",
    )(xbc, xbc, xbc, a_grp, dt_bias, neg_a)


def _layernorm(x, g, b, eps=1e-5):
    mu = jnp.mean(x, -1, keepdims=True)
    xc = x - mu
    var = jnp.mean(xc * xc, -1, keepdims=True)
    return xc * lax.rsqrt(var + eps) * g + b


def _merge_kernel(x_ref, o_ref, yf_ref, yb_ref, xs_ref, z_ref, u_ref, gl_ref,
                  wa_ref, ws_ref, wc_ref, wo_ref, dsk_ref, ng_ref, cg_ref, cb_ref, bc_ref,
                  l1g_ref, l1b_ref, rw_ref, rb_ref,
                  x1_ref, x1b_ref, x1g_ref, route_ref, cnt_ref, carry_ref):
    tm = x_ref.shape[0]

    @pl.when(pl.program_id(0) == 0)
    def _():
        carry_ref[...] = jnp.zeros_like(carry_ref)

    y_attn = jnp.dot(o_ref[...], wa_ref[...], preferred_element_type=F32)
    z = z_ref[...].astype(F32)
    ys = (yf_ref[...] + yb_ref[...] + xs_ref[...] * dsk_ref[...]) * (z * jax.nn.sigmoid(z))
    gw = SSM_D_INNER // SSM_GROUPS
    ys = jnp.concatenate(
        [_rms(ys[:, g * gw:(g + 1) * gw], ng_ref[:, g * gw:(g + 1) * gw]) for g in range(SSM_GROUPS)], -1)
    y_ssm = jnp.dot(ys.astype(BF16), ws_ref[...], preferred_element_type=F32)
    uc = _layernorm(u_ref[...], cg_ref[...], cb_ref[...])
    uc = uc * jax.nn.sigmoid(uc)
    y_conv = jnp.dot(uc.astype(BF16), wc_ref[...], preferred_element_type=F32) + bc_ref[...]
    gl = gl_ref[...].astype(F32)
    mixed = (jax.nn.sigmoid(gl[:, :D_MODEL]) * y_attn
             + jax.nn.sigmoid(gl[:, D_MODEL:2 * D_MODEL]) * y_ssm
             + jax.nn.sigmoid(gl[:, 2 * D_MODEL:]) * y_conv)
    mixed = jnp.dot(mixed.astype(BF16), wo_ref[...], preferred_element_type=F32)
    x1 = _layernorm(DN_ALPHA * x_ref[...] + mixed, l1g_ref[...], l1b_ref[...])
    x1_ref[...] = x1
    x1b_ref[...] = x1.astype(BF16)
    for t in range(ROW_TILES):
        x1g_ref[pl.ds(t, tm, stride=ROW_TILES), :] = x1[:, t * LANES:(t + 1) * LANES]

    lg = jnp.dot(x1, rw_ref[...], precision=HIGHEST, preferred_element_type=F32) + rb_ref[...]
    lane = lax.broadcasted_iota(jnp.int32, (tm, LANES), 1).astype(F32)
    sels, vals, idxs = [], [], []
    for _ in range(TOP_K):
        m = jnp.max(lg, -1, keepdims=True)
        idx = jnp.min(jnp.where(lg == m, lane, float(LANES)), -1, keepdims=True)
        sel = lane == idx
        lg = jnp.where(sel, NEG_BIG * 2, lg)
        sels.append(sel)
        vals.append(m)
        idxs.append(idx)
    es = [jnp.exp(v - vals[0]) for v in vals]
    den = es[0] + es[1] + es[2] + es[3]
    hot = jnp.zeros((tm, LANES), F32)
    for sel in sels:
        hot = hot + sel.astype(F32)
    r = lax.broadcasted_iota(jnp.int32, (tm, tm), 0)
    c = lax.broadcasted_iota(jnp.int32, (tm, tm), 1)
    excl = jnp.dot((c < r).astype(BF16), hot.astype(BF16), preferred_element_type=F32) + carry_ref[...]
    route = jnp.zeros((tm, LANES), F32)
    for kk in range(TOP_K):
        rank = jnp.sum(jnp.where(sels[kk], excl, 0.0), -1, keepdims=True)
        route = jnp.where(lane == float(kk), idxs[kk], route)
        route = jnp.where(lane == float(TOP_K + kk), rank, route)
        route = jnp.where(lane == float(2 * TOP_K + kk), es[kk] / den, route)
    route_ref[...] = route
    carry_ref[...] = carry_ref[...] + jnp.sum(hot, 0, keepdims=True)
    cnt_ref[...] = carry_ref[...]


def _merge(x, o, yf, yb, xbc, z, u, gl, wa, ws, wc, wo, dsk, ng, cg, cb, bc, l1g, l1b, rw, rb, tm=256):
    t = x.shape[0]
    tm = min(tm, t)

    def rowb(width, col=0):
        return pl.BlockSpec((tm, width), lambda i: (i, col))

    def const(shape):
        return pl.BlockSpec(shape, lambda i: (0, 0))

    return pl.pallas_call(
        _merge_kernel,
        grid=(t // tm,),
        in_specs=[rowb(D_MODEL), rowb(MLA_W), rowb(SSM_D_INNER), rowb(SSM_D_INNER), rowb(SSM_D_INNER),
                  rowb(SSM_D_INNER), rowb(CNV_CH), rowb(3 * D_MODEL),
                  const((MLA_W, D_MODEL)), const((SSM_D_INNER, D_MODEL)), const((CNV_CH, D_MODEL)),
                  const((D_MODEL, D_MODEL)), const((1, SSM_D_INNER)), const((1, SSM_D_INNER)),
                  const((1, CNV_CH)), const((1, CNV_CH)), const((1, D_MODEL)),
                  const((1, D_MODEL)), const((1, D_MODEL)), const((D_MODEL, LANES)), const((1, LANES))],
        out_specs=[rowb(D_MODEL), rowb(D_MODEL),
                   pl.BlockSpec((tm * ROW_TILES, LANES), lambda i: (i, 0)),
                   rowb(LANES), const((1, LANES))],
        out_shape=[jax.ShapeDtypeStruct((t, D_MODEL), F32), jax.ShapeDtypeStruct((t, D_MODEL), BF16),
                   jax.ShapeDtypeStruct((t * ROW_TILES, LANES), F32),
                   jax.ShapeDtypeStruct((t, LANES), F32), jax.ShapeDtypeStruct((1, LANES), F32)],
        scratch_shapes=[pltpu.VMEM((1, LANES), F32)],
        compiler_params=_params("arbitrary"),
        name="merge",
    )(x, o, yf, yb, xbc, z, u, gl, wa, ws, wc, wo, dsk, ng, cg, cb, bc, l1g, l1b, rw, rb)


def _row_slice(ref, row):
    return ref.at[pl.ds(pl.multiple_of(row * ROW_TILES, ROW_TILES), ROW_TILES)]


def _from_row_tiles(ref, first, rows, pitch):
    return jnp.concatenate([ref[pl.ds(first + t, rows, stride=pitch), :] for t in range(ROW_TILES)], -1)


def _dispatch_kernel(dest_ref, x_ref, zero_hbm, xs_hbm, sem, *, tm):
    del zero_hbm

    def body(c, carry):
        for u in range(ISSUE_UNROLL):
            t = c * ISSUE_UNROLL + u
            src = _row_slice(x_ref, t)
            for kk in range(TOP_K):
                pltpu.make_async_copy(src, _row_slice(xs_hbm, dest_ref[0, 0, t * TOP_K + kk]), sem).start()
        return carry

    lax.fori_loop(0, tm // ISSUE_UNROLL, body, 0)
    for _ in range(TOP_K):
        pltpu.make_async_copy(x_ref, xs_hbm.at[pl.ds(0, tm * ROW_TILES)], sem).wait()


def _dispatch(dest, x1g, n_rows, tm=256):
    t = x1g.shape[0] // ROW_TILES
    tm = min(tm, t)
    nt = t // tm
    zeros = jnp.zeros((n_rows * ROW_TILES, LANES), F32)
    return pl.pallas_call(
        functools.partial(_dispatch_kernel, tm=tm),
        grid=(nt,),
        in_specs=[pl.BlockSpec((1, 1, tm * TOP_K), lambda i: (i, 0, 0), memory_space=pltpu.SMEM),
                  pl.BlockSpec((tm * ROW_TILES, LANES), lambda i: (i, 0)),
                  pl.BlockSpec(memory_space=pl.ANY)],
        out_specs=pl.BlockSpec(memory_space=pl.ANY),
        out_shape=jax.ShapeDtypeStruct((n_rows * ROW_TILES, LANES), F32),
        scratch_shapes=[pltpu.SemaphoreType.DMA(())],
        input_output_aliases={2: 0},
        compiler_params=_params("arbitrary"),
        name="dispatch",
    )(dest.reshape(nt, 1, tm * TOP_K), x1g, zeros)


def _moe_kernel(blk_e_ref, nused_ref, x_ref, wgu_ref, bgu_ref, wdn_ref, bdn_ref, o_ref, wgu_b, wdn_b):
    i = pl.program_id(0)

    @pl.when((i == 0) | (blk_e_ref[i] != blk_e_ref[jnp.maximum(i - 1, 0)]))
    def _():
        wgu_b[...] = wgu_ref[0, 0].astype(BF16)
        wdn_b[...] = wdn_ref[0, 0].astype(BF16)

    @pl.when(i < nused_ref[0])
    def _():
        xb = _from_row_tiles(x_ref, 0, MOE_ROWS, ROW_TILES).astype(BF16)
        h = jnp.dot(xb, wgu_b[...], preferred_element_type=F32) + bgu_ref[0]
        gate = jnp.minimum(h[:, :D_FF], SWIGLU_LIMIT)
        up = jnp.clip(h[:, D_FF:], -SWIGLU_LIMIT, SWIGLU_LIMIT)
        act = (up + 1.0) * (gate * jax.nn.sigmoid(SWIGLU_ALPHA * gate))
        y = jnp.dot(act.astype(BF16), wdn_b[...], preferred_element_type=F32) + bdn_ref[0]
        for t in range(ROW_TILES):
            o_ref[pl.ds(t, MOE_ROWS, stride=ROW_TILES), :] = y[:, t * LANES:(t + 1) * LANES]

    @pl.when(i >= nused_ref[0])
    def _():
        o_ref[...] = jnp.zeros_like(o_ref)


def _moe_experts(blk_e, nused, xs, wgu, bgu, wdn, bdn, layer):
    n_blocks = blk_e.shape[0]
    grid_spec = pltpu.PrefetchScalarGridSpec(
        num_scalar_prefetch=2,
        grid=(n_blocks,),
        in_specs=[pl.BlockSpec((MOE_ROWS * ROW_TILES, LANES), lambda i, be, nu: (i, 0)),
                  pl.BlockSpec((1, 1, D_MODEL, 2 * D_FF), lambda i, be, nu: (layer, be[i], 0, 0)),
                  pl.BlockSpec((1, 1, 2 * D_FF), lambda i, be, nu: (be[i], 0, 0)),
                  pl.BlockSpec((1, 1, D_FF, D_MODEL), lambda i, be, nu: (layer, be[i], 0, 0)),
                  pl.BlockSpec((1, 1, D_MODEL), lambda i, be, nu: (be[i], 0, 0))],
        out_specs=pl.BlockSpec((MOE_ROWS * ROW_TILES, LANES), lambda i, be, nu: (i, 0)),
        scratch_shapes=[pltpu.VMEM((D_MODEL, 2 * D_FF), BF16), pltpu.VMEM((D_FF, D_MODEL), BF16)],
    )
    return pl.pallas_call(
        _moe_kernel,
        grid_spec=grid_spec,
        out_shape=jax.ShapeDtypeStruct(xs.shape, F32),
        compiler_params=_params("arbitrary"),
        name="moe_experts",
    )(blk_e, nused, xs, wgu, bgu, wdn, bdn)


def _gather_topk_rows(dest_ref, yb_hbm, dst, sem, tm):
    def body(c, carry):
        for u in range(ISSUE_UNROLL):
            t = c * ISSUE_UNROLL + u
            for kk in range(TOP_K):
                pltpu.make_async_copy(_row_slice(yb_hbm, dest_ref[0, 0, t * TOP_K + kk]),
                                      _row_slice(dst, kk * tm + t), sem).start()
        return carry
    lax.fori_loop(0, tm // ISSUE_UNROLL, body, 0)


def _combine_kernel(dest_first_ref, dest_next_ref, yb_hbm, x1_ref, route_ref, g_ref, b_ref, x2_ref, x2b_ref,
                    buf, sem, *, tm):
    i = pl.program_id(0)
    n = pl.num_programs(0)
    slot = i % 2

    @pl.when(i == 0)
    def _():
        _gather_topk_rows(dest_first_ref, yb_hbm, buf.at[0], sem.at[0], tm)

    @pl.when(i + 1 < n)
    def _():
        _gather_topk_rows(dest_next_ref, yb_hbm, buf.at[1 - slot], sem.at[1 - slot], tm)

    pltpu.make_async_copy(yb_hbm.at[pl.ds(0, tm * TOP_K * ROW_TILES)], buf.at[slot], sem.at[slot]).wait()
    route = route_ref[...]
    ffn = jnp.zeros((tm, D_MODEL), F32)
    for kk in range(TOP_K):
        rows = _from_row_tiles(buf.at[slot], kk * tm * ROW_TILES, tm, ROW_TILES)
        ffn = ffn + rows * route[:, 2 * TOP_K + kk:2 * TOP_K + kk + 1]
    x2 = _layernorm(DN_ALPHA * x1_ref[...] + ffn, g_ref[...], b_ref[...])
    x2_ref[...] = x2
    x2b_ref[...] = x2.astype(BF16)


def _combine(dest, yb, x1, route, g, b, tm=256):
    t = x1.shape[0]
    tm = min(tm, t)
    nt = t // tm
    dest3 = dest.reshape(nt, 1, tm * TOP_K)
    return pl.pallas_call(
        functools.partial(_combine_kernel, tm=tm),
        grid=(nt,),
        in_specs=[pl.BlockSpec((1, 1, tm * TOP_K), lambda i: (0, 0, 0), memory_space=pltpu.SMEM),
                  pl.BlockSpec((1, 1, tm * TOP_K), lambda i: (jnp.minimum(i + 1, nt - 1), 0, 0),
                               memory_space=pltpu.SMEM),
                  pl.BlockSpec(memory_space=pl.ANY),
                  pl.BlockSpec((tm, D_MODEL), lambda i: (i, 0)),
                  pl.BlockSpec((tm, LANES), lambda i: (i, 0)),
                  pl.BlockSpec((1, D_MODEL), lambda i: (0, 0)),
                  pl.BlockSpec((1, D_MODEL), lambda i: (0, 0))],
        out_specs=[pl.BlockSpec((tm, D_MODEL), lambda i: (i, 0)),
                   pl.BlockSpec((tm, D_MODEL), lambda i: (i, 0))],
        out_shape=[jax.ShapeDtypeStruct((t, D_MODEL), F32), jax.ShapeDtypeStruct((t, D_MODEL), BF16)],
        scratch_shapes=[pltpu.VMEM((2, tm * TOP_K * ROW_TILES, LANES), F32),
                        pltpu.SemaphoreType.DMA((2,))],
        compiler_params=_params("arbitrary"),
        name="combine",
    )(dest3, dest3, yb, x1, route, g, b)


def _pad_cols(w, width):
    return jnp.pad(w, ((0, 0), (0, width - w.shape[1])))


def _pack_in_proj(w_in, b_in):
    offs = np.concatenate([[0], np.cumsum(IN_SIZES)])
    wb = jnp.concatenate([w_in, b_in[None, :]], 0)

    def piece(i):
        return wb[:, offs[i]:offs[i + 1]]

    kr = piece(2)
    kr_swapped = jnp.concatenate([kr[:, MLA_DR // 2:], kr[:, :MLA_DR // 2]], 1)
    dt = piece(5)
    grp_a = jnp.concatenate([piece(0), piece(1), _pad_cols(jnp.concatenate([kr, kr_swapped], 1), LANES),
                             _pad_cols(dt[:, :SSM_HEADS], LANES), _pad_cols(dt[:, SSM_HEADS:], LANES)], 1)
    groups = {"a": (grp_a, F32), "z": (piece(3), BF16), "xbc": (piece(4), F32),
              "cnv": (jnp.concatenate([piece(6), piece(7)], 1), F32), "gate": (piece(8), BF16)}
    return {k: (v[:-1].astype(BF16), v[-1:], dt_) for k, (v, dt_) in groups.items()}


def _pack_mla(w_uq, w_ukv, w_br_attn, seq):
    hq = MLA_DN + MLA_DR
    half = MLA_DR // 2
    wq = w_uq.reshape(MLA_Q_LORA, MLA_HEADS, hq)
    zq = jnp.zeros((MLA_Q_LORA, MLA_HEADS, HEAD_PAD - hq), F32)
    wq_main = jnp.concatenate([wq, zq], -1).reshape(MLA_Q_LORA, MLA_W)
    wq_swap = jnp.concatenate([jnp.zeros((MLA_Q_LORA, MLA_HEADS, MLA_DN), F32),
                               wq[..., MLA_DN + half:], wq[..., MLA_DN:MLA_DN + half], zq],
                              -1).reshape(MLA_Q_LORA, MLA_W)
    wkv = w_ukv.reshape(MLA_KV_LORA, MLA_HEADS, MLA_DN + MLA_DV)
    zk = jnp.zeros((MLA_KV_LORA, MLA_HEADS, HEAD_PAD - MLA_DN), F32)
    wk = jnp.concatenate([wkv[..., :MLA_DN], zk], -1).reshape(MLA_KV_LORA, MLA_W)
    wv = jnp.concatenate([wkv[..., MLA_DN:], zk], -1).reshape(MLA_KV_LORA, MLA_W)
    e2 = np.zeros((LANES, MLA_HEADS, HEAD_PAD), np.float32)
    for j in range(MLA_DR):
        e2[j, :, MLA_DN + j] = 1.0
        e2[MLA_DR + j, :, MLA_DN + j] = 1.0
    vone = np.zeros((MLA_HEADS, HEAD_PAD), np.float32)
    vone[:, MLA_DV] = 1.0
    wbr = jnp.concatenate([w_br_attn.reshape(MLA_HEADS, MLA_DV, D_MODEL),
                           jnp.zeros((MLA_HEADS, HEAD_PAD - MLA_DV, D_MODEL), F32)], 1).reshape(MLA_W, D_MODEL)
    pos = jnp.arange(seq, dtype=F32)
    inv = ROPE_THETA ** (-jnp.arange(0, MLA_DR, 2, dtype=F32) / MLA_DR)
    ang = pos[:, None] * inv[None, :]
    cos, sin = jnp.cos(ang), jnp.sin(ang)
    scale = (MLA_DN + MLA_DR) ** -0.5
    ones = jnp.ones((seq, MLA_DN), F32)
    zpad = jnp.zeros((seq, HEAD_PAD - hq), F32)
    cq = jnp.tile(jnp.concatenate([ones, cos, cos, zpad], 1) * scale, (1, MLA_HEADS))
    sq = jnp.tile(jnp.concatenate([0 * ones, -sin, sin, zpad], 1) * scale, (1, MLA_HEADS))
    tk = jnp.concatenate([cos, cos, -sin, sin, jnp.zeros((seq, LANES - 2 * MLA_DR), F32)], 1)
    return dict(wq=wq_main.astype(BF16), wqs=wq_swap.astype(BF16), wk=wk.astype(BF16), wv=wv.astype(BF16),
                e2=jnp.asarray(e2.reshape(LANES, MLA_W), BF16), vone=jnp.asarray(vone.reshape(1, MLA_W)),
                wbr=wbr.astype(BF16), cq=cq, sq=sq, tk=tk)


def _route_tables(route, cnt, n_tok):
    idx = route[:, :TOP_K].astype(jnp.int32)
    rank = route[:, TOP_K:2 * TOP_K].astype(jnp.int32)
    counts = cnt[0, :N_EXPERTS].astype(jnp.int32)
    n_blocks = -(-(n_tok * TOP_K + N_EXPERTS * (MOE_ROWS - 1)) // MOE_ROWS)
    padded = (counts + MOE_ROWS - 1) // MOE_ROWS * MOE_ROWS
    pad_end = jnp.cumsum(padded)
    pad_start = pad_end - padded
    onehot = idx[..., None] == jnp.arange(N_EXPERTS, dtype=jnp.int32)
    dest = (jnp.sum(jnp.where(onehot, pad_start, 0), -1) + rank).reshape(-1)
    blk_first = jnp.arange(n_blocks, dtype=jnp.int32) * MOE_ROWS
    blk_e = jnp.minimum(jnp.sum((pad_end[None, :] <= blk_first[:, None]).astype(jnp.int32), -1), N_EXPERTS - 1)
    nused = pad_end[-1:] // MOE_ROWS
    return dest, blk_e, nused, n_blocks * MOE_ROWS


def kernel(x, w_in, b_in, mla_q_norm, mla_kv_norm, mla_w_uq, mla_w_ukv, w_br_attn, ssm_conv_w, ssm_conv_b, ssm_dt_bias, ssm_a_log, ssm_d, ssm_norm, w_br_ssm, cnv_dw_w, cnv_dw_b, cnv_ln_g, cnv_ln_b, w_br_conv, b_br_conv, w_out, ln1_g, ln1_b, router_w, router_b, moe_w_gate_up, moe_b_gate_up, moe_w_down, moe_b_down, ln2_g, ln2_b):
    bsz, seq, d = x.shape
    n_tok = bsz * seq
    xf = x.reshape(n_tok, d)
    xb = xf.astype(BF16)
    for l in range(DEPTH):
        proj = _pack_in_proj(w_in[l], b_in[l])
        mla = _pack_mla(mla_w_uq[l], mla_w_ukv[l], w_br_attn[l], seq)
        a_grp = _linear(xb, *proj["a"])
        z = _linear(xb, *proj["z"])
        xbc_raw = _linear(xb, *proj["xbc"], tn=768)
        cnv_raw = _linear(xb, *proj["cnv"])
        gate_logits = _linear(xb, *proj["gate"], tn=1024)

        q, k, v = _mla_prep(a_grp, seq, mla_q_norm[l][None], mla_kv_norm[l][None], mla["wq"], mla["wqs"],
                            mla["wk"], mla["wv"], mla["e2"], mla["cq"], mla["sq"], mla["tk"], mla["vone"])
        attn = _attention(q.reshape(bsz, seq, MLA_W), k.reshape(bsz, seq, MLA_W), v.reshape(bsz, seq, MLA_W))

        xbc = _dwconv(xbc_raw.reshape(bsz, seq, SSM_CONV_DIM), ssm_conv_w[l], ssm_conv_b[l],
                      glu=False, silu_out=True)
        a3 = a_grp.reshape(bsz, seq, GROUP_A)
        dtb = jnp.pad(ssm_dt_bias[l], ((0, 0), (0, LANES - SSM_HEADS)))
        nega = jnp.pad(-jnp.exp(ssm_a_log[l]), ((0, 0), (0, LANES - SSM_HEADS)))
        y_fwd = _ssd(xbc, a3, dtb[0:1], nega[0:1], reverse=False)
        y_bwd = _ssd(xbc, a3, dtb[1:2], nega[1:2], reverse=True)

        u = _dwconv(cnv_raw.reshape(bsz, seq, 2 * CNV_CH), cnv_dw_w[l], cnv_dw_b[l], glu=True, silu_out=False)

        rw = jnp.pad(router_w[l], ((0, 0), (0, LANES - N_EXPERTS)))
        rb = jnp.pad(router_b[l], (0, LANES - N_EXPERTS), constant_values=NEG_BIG)[None]
        x1, x1b, x1g, route, cnt = _merge(
            xf, attn.reshape(n_tok, MLA_W), y_fwd.reshape(n_tok, SSM_D_INNER), y_bwd.reshape(n_tok, SSM_D_INNER),
            xbc.reshape(n_tok, SSM_CONV_DIM), z, u.reshape(n_tok, CNV_CH), gate_logits,
            mla["wbr"], w_br_ssm[l].astype(BF16), w_br_conv[l].astype(BF16), w_out[l].astype(BF16),
            jnp.repeat(ssm_d[l], SSM_HEAD_DIM)[None], ssm_norm[l][None], cnv_ln_g[l][None], cnv_ln_b[l][None],
            b_br_conv[l][None], ln1_g[l][None], ln1_b[l][None], rw, rb)

        dest, blk_e, nused, n_rows = _route_tables(route, cnt, n_tok)
        xs = _dispatch(dest, x1g, n_rows)
        yb = _moe_experts(blk_e, nused, xs, moe_w_gate_up, moe_b_gate_up[l][:, None, :], moe_w_down,
                          moe_b_down[l][:, None, :], l)
        xf, xb = _combine(dest, yb, x1, route, ln2_g[l][None], ln2_b[l][None])
    return xf.reshape(bsz, seq, d)
```

```python
import functools
import math

import numpy as np
import jax
import jax.numpy as jnp
from jax import lax
from jax.experimental import pallas as pl
from jax.experimental.pallas import tpu as pltpu

F32 = jnp.float32
BF16 = jnp.bfloat16
HIGHEST = lax.Precision.HIGHEST

LANES = 128
SUBLANES = 8
VMEM_LIMIT_BYTES = 56 * 1024 * 1024

D_MODEL = 1024
DEPTH = 2
MLA_HEADS = 8
MLA_Q_LORA = 384
MLA_KV_LORA = 256
MLA_DN = 64
MLA_DR = 32
MLA_DV = 64
ROPE_THETA = 10000.0
SSM_HEADS = 16
SSM_HEAD_DIM = 64
SSM_D_INNER = SSM_HEADS * SSM_HEAD_DIM
SSM_GROUPS = 4
SSM_STATE = 64
SSM_CONV = 5
SSM_CHUNK = 128
SSM_BC = SSM_GROUPS * SSM_STATE
SSM_CONV_DIM = SSM_D_INNER + 2 * SSM_BC
CNV_CH = 512
CNV_WIDTH = 31
N_EXPERTS = 32
TOP_K = 4
D_FF = 1024
SWIGLU_LIMIT = 7.0
SWIGLU_ALPHA = 1.702
DN_ALPHA = (2 * DEPTH) ** 0.25
IN_SIZES = (MLA_Q_LORA, MLA_KV_LORA, MLA_DR, SSM_D_INNER, SSM_CONV_DIM, 2 * SSM_HEADS,
            CNV_CH, CNV_CH, 3 * D_MODEL)
HEAD_PAD = LANES
MLA_W = MLA_HEADS * HEAD_PAD
GROUP_A = 1024
CONV_HALO = 16
MOE_ROWS = 512
ISSUE_UNROLL = 8
ROW_TILES = D_MODEL // LANES
NEG_BIG = -1e30


def _params(*sem):
    return pltpu.CompilerParams(dimension_semantics=sem, vmem_limit_bytes=VMEM_LIMIT_BYTES)


def _linear_kernel(x_ref, w_ref, b_ref, o_ref):
    acc = jnp.dot(x_ref[...], w_ref[...], preferred_element_type=F32)
    o_ref[...] = (acc + b_ref[...]).astype(o_ref.dtype)


def _linear(x, w, b, out_dtype, tm=1024, tn=None):
    m, k = x.shape
    n = w.shape[1]
    tm = min(tm, m)
    tn = n if tn is None else tn
    return pl.pallas_call(
        _linear_kernel,
        grid=(n // tn, m // tm),
        in_specs=[pl.BlockSpec((tm, k), lambda j, i: (i, 0)),
                  pl.BlockSpec((k, tn), lambda j, i: (0, j)),
                  pl.BlockSpec((1, tn), lambda j, i: (0, j))],
        out_specs=pl.BlockSpec((tm, tn), lambda j, i: (i, j)),
        out_shape=jax.ShapeDtypeStruct((m, n), out_dtype),
        compiler_params=_params("parallel", "parallel"),
        name="linear",
    )(x, w, b)


def _rms(x, g, eps=1e-6):
    return x * lax.rsqrt(jnp.mean(x * x, -1, keepdims=True) + eps) * g


def _mla_prep_kernel(a_ref, gq_ref, gkv_ref, wq_ref, wqs_ref, wk_ref, wv_ref, e2_ref,
                     cq_ref, sq_ref, tk_ref, vone_ref, q_ref, k_ref, v_ref):
    a = a_ref[...]
    c_q = a[:, :MLA_Q_LORA]
    c_kv = a[:, MLA_Q_LORA:MLA_Q_LORA + MLA_KV_LORA]
    kr = a[:, MLA_Q_LORA + MLA_KV_LORA:]
    qn = _rms(c_q, gq_ref[...]).astype(BF16)
    kvn = _rms(c_kv, gkv_ref[...]).astype(BF16)
    q = (jnp.dot(qn, wq_ref[...], preferred_element_type=F32) * cq_ref[...]
         + jnp.dot(qn, wqs_ref[...], preferred_element_type=F32) * sq_ref[...])
    q_ref[...] = q.astype(BF16)
    krp = (kr * tk_ref[...]).astype(BF16)
    k = (jnp.dot(kvn, wk_ref[...], preferred_element_type=F32)
         + jnp.dot(krp, e2_ref[...], preferred_element_type=F32))
    k_ref[...] = k.astype(BF16)
    v = jnp.dot(kvn, wv_ref[...], preferred_element_type=F32) + vone_ref[...]
    v_ref[...] = v.astype(BF16)


def _mla_prep(a, seq, gq, gkv, wq, wqs, wk, wv, e2, cq, sq, tk, vone, tm=512):
    t = a.shape[0]
    tm = min(tm, seq)
    nper = seq // tm
    wa = MLA_Q_LORA + MLA_KV_LORA + LANES

    def const(shape):
        return pl.BlockSpec(shape, lambda i: (0, 0))

    def tab(width):
        return pl.BlockSpec((tm, width), lambda i: (i % nper, 0))

    out = jax.ShapeDtypeStruct((t, MLA_W), BF16)
    return pl.pallas_call(
        _mla_prep_kernel,
        grid=(t // tm,),
        in_specs=[pl.BlockSpec((tm, wa), lambda i: (i, 0)),
                  const((1, MLA_Q_LORA)), const((1, MLA_KV_LORA)),
                  const((MLA_Q_LORA, MLA_W)), const((MLA_Q_LORA, MLA_W)),
                  const((MLA_KV_LORA, MLA_W)), const((MLA_KV_LORA, MLA_W)),
                  const((LANES, MLA_W)),
                  tab(MLA_W), tab(MLA_W), tab(LANES), const((1, MLA_W))],
        out_specs=[pl.BlockSpec((tm, MLA_W), lambda i: (i, 0))] * 3,
        out_shape=[out, out, out],
        compiler_params=_params("parallel"),
        name="mla_prep",
    )(a, gq, gkv, wq, wqs, wk, wv, e2, cq, sq, tk, vone)


def _attn_kernel(q_ref, k_ref, v_ref, o_ref):
    s = lax.dot_general(q_ref[0], k_ref[0], (((1,), (1,)), ((), ())),
                        preferred_element_type=F32)
    m = jnp.max(s, -1, keepdims=True)
    p = jnp.exp(s - m).astype(BF16)
    o = jnp.dot(p, v_ref[0], preferred_element_type=F32)
    o_ref[0] = (o / o[:, MLA_DV:MLA_DV + 1]).astype(o_ref.dtype)


def _attention(q, k, v, tq=256):
    b, s, _ = q.shape
    tq = min(tq, s)
    return pl.pallas_call(
        _attn_kernel,
        grid=(b, MLA_HEADS, s // tq),
        in_specs=[pl.BlockSpec((1, tq, HEAD_PAD), lambda bi, h, i: (bi, i, h)),
                  pl.BlockSpec((1, s, HEAD_PAD), lambda bi, h, i: (bi, 0, h)),
                  pl.BlockSpec((1, s, HEAD_PAD), lambda bi, h, i: (bi, 0, h))],
        out_specs=pl.BlockSpec((1, tq, HEAD_PAD), lambda bi, h, i: (bi, i, h)),
        out_shape=jax.ShapeDtypeStruct((b, s, MLA_W), BF16),
        compiler_params=_params("parallel", "parallel", "parallel"),
        name="attention",
    )(q, k, v)


def _dwconv_kernel(*refs, width, glu, silu_out, seq, rows):
    if glu:
        a_ref, g_ref, w_ref, b_ref, o_ref, pad_ref = refs
        pre = a_ref[0] * jax.nn.sigmoid(g_ref[0])
    else:
        x_ref, w_ref, b_ref, o_ref, pad_ref = refs
        pre = x_ref[0]
    ch = o_ref.shape[-1]
    halo = jnp.zeros((CONV_HALO, ch), F32)
    pad_ref[0:CONV_HALO, :] = halo
    pad_ref[CONV_HALO + seq:2 * CONV_HALO + seq, :] = halo
    pad_ref[CONV_HALO:CONV_HALO + seq, :] = pre
    half = (width - 1) // 2
    win_rows = rows + 2 * CONV_HALO

    def body(c, carry):
        base = pl.multiple_of(c * rows, rows)
        win = pad_ref[pl.ds(base, win_rows), :]
        acc = jnp.zeros((rows, ch), F32) + b_ref[...]
        for sub in range(SUBLANES):
            taps = [t for t in range(width) if (CONV_HALO - half + t) % SUBLANES == sub]
            if not taps:
                continue
            rolled = win if sub == 0 else pltpu.roll(win, win_rows - sub, 0)
            for t in taps:
                first = CONV_HALO - half + t - sub
                acc = acc + rolled[first:first + rows] * w_ref[t:t + 1, :]
        if silu_out:
            acc = acc * jax.nn.sigmoid(acc)
        o_ref[0, pl.ds(base, rows), :] = acc
        return carry

    lax.fori_loop(0, seq // rows, body, 0)


def _dwconv(x, w, b, *, glu, silu_out, rows=64):
    bsz, seq, cin = x.shape
    width, ch = w.shape
    nct = ch // LANES
    kern = functools.partial(_dwconv_kernel, width=width, glu=glu, silu_out=silu_out, seq=seq,
                             rows=min(rows, seq))
    xspec = pl.BlockSpec((1, seq, LANES), lambda bi, j: (bi, 0, j))
    in_specs = [xspec]
    args = [x]
    if glu:
        in_specs.append(pl.BlockSpec((1, seq, LANES), lambda bi, j: (bi, 0, j + nct)))
        args.append(x)
    in_specs += [pl.BlockSpec((width, LANES), lambda bi, j: (0, j)),
                 pl.BlockSpec((1, LANES), lambda bi, j: (0, j))]
    return pl.pallas_call(
        kern,
        grid=(bsz, nct),
        in_specs=in_specs,
        out_specs=pl.BlockSpec((1, seq, LANES), lambda bi, j: (bi, 0, j)),
        out_shape=jax.ShapeDtypeStruct((bsz, seq, ch), F32),
        scratch_shapes=[pltpu.VMEM((seq + 2 * CONV_HALO, LANES), F32)],
        compiler_params=_params("parallel", "parallel"),
        name="dwconv",
    )(*args, w, b.reshape(1, ch))


def _softplus(x):
    return jnp.maximum(x, 0.0) + jnp.log1p(jnp.exp(-jnp.abs(x)))


def _ssd_kernel(xs_ref, bm_ref, cm_ref, dt_ref, dtb_ref, nega_ref, y_ref, h_ref, *, reverse):
    ln = SSM_CHUNK

    @pl.when(pl.program_id(1) == 0)
    def _():
        h_ref[...] = jnp.zeros_like(h_ref)

    dt = _softplus(dt_ref[0] + dtb_ref[...])
    a = dt * nega_ref[...]
    row = lax.broadcasted_iota(jnp.int32, (ln, ln), 0)
    col = lax.broadcasted_iota(jnp.int32, (ln, ln), 1)
    tri = (col >= row) if reverse else (col <= row)
    acs = jnp.dot(tri.astype(F32), a, precision=HIGHEST, preferred_element_type=F32)
    tot = acs[0:1] if reverse else acs[ln - 1:ln]
    wst = dt * jnp.exp(tot - acs)
    etot = jnp.exp(tot)
    acs_t = acs.T
    dt_t = dt.T
    wst_t = wst.T

    xs_b = xs_ref[0].astype(BF16)
    cm_b = cm_ref[0].astype(BF16)
    bm_t = bm_ref[0].T
    bm_tb = bm_t.astype(BF16)
    y_off = jnp.dot(cm_b, h_ref[...].astype(BF16), preferred_element_type=F32)

    glane = lax.broadcasted_iota(jnp.int32, (ln, SSM_BC), 1) // SSM_STATE
    lo = lax.broadcasted_iota(jnp.int32, (ln, LANES), 1) < SSM_HEAD_DIM
    lo_s = lax.broadcasted_iota(jnp.int32, (SSM_STATE, LANES), 1) < SSM_HEAD_DIM
    lo_1 = lax.broadcasted_iota(jnp.int32, (1, LANES), 1) < SSM_HEAD_DIM
    heads_per_group = SSM_HEADS // SSM_GROUPS
    for g in range(SSM_GROUPS):
        cmg = jnp.where(glane == g, cm_b, jnp.zeros_like(cm_b))
        cb = jnp.dot(cmg, bm_tb, preferred_element_type=F32)
        bm_tg = bm_t[g * SSM_STATE:(g + 1) * SSM_STATE, :]
        for j in range(heads_per_group // 2):
            pair = g * (heads_per_group // 2) + j
            lanes = slice(pair * LANES, (pair + 1) * LANES)
            xp = xs_b[:, lanes]
            diag, ecol, st, et = [], [], [], []
            for hh in range(2):
                h = 2 * pair + hh
                colb = jnp.broadcast_to(acs[:, h:h + 1], (ln, LANES))
                seg = colb - acs_t[h:h + 1, :]
                decay = jnp.exp(jnp.where(tri, seg, -jnp.inf))
                mat = (cb * decay * dt_t[h:h + 1, :]).astype(BF16)
                diag.append(jnp.dot(mat, xp, preferred_element_type=F32))
                ecol.append(jnp.exp(colb))
                st.append(jnp.dot((bm_tg * wst_t[h:h + 1, :]).astype(BF16), xp,
                                  preferred_element_type=F32))
                et.append(jnp.broadcast_to(etot[:, h:h + 1], (1, LANES)))
            y_ref[0, :, lanes] = (jnp.where(lo, diag[0], diag[1])
                                  + y_off[:, lanes] * jnp.where(lo, ecol[0], ecol[1]))
            rows = slice(g * SSM_STATE, (g + 1) * SSM_STATE)
            h_ref[rows, lanes] = (h_ref[rows, lanes] * jnp.where(lo_1, et[0], et[1])
                                  + jnp.where(lo_s, st[0], st[1]))


def _ssd(xbc, a_grp, dt_bias, neg_a, *, reverse):
    bsz, seq, _ = xbc.shape
    nc = seq // SSM_CHUNK
    dt_tile = 7 if reverse else 6

    def cidx(c):
        return (nc - 1 - c) if reverse else c

    return pl.pallas_call(
        functools.partial(_ssd_kernel, reverse=reverse),
        grid=(bsz, nc),
        in_specs=[pl.BlockSpec((1, SSM_CHUNK, SSM_D_INNER), lambda b, c: (b, cidx(c), 0)),
                  pl.BlockSpec((1, SSM_CHUNK, SSM_BC), lambda b, c: (b, cidx(c), SSM_D_INNER // SSM_BC)),
                  pl.BlockSpec((1, SSM_CHUNK, SSM_BC), lambda b, c: (b, cidx(c), SSM_D_INNER // SSM_BC + 1)),
                  pl.BlockSpec((1, SSM_CHUNK, LANES), lambda b, c: (b, cidx(c), dt_tile)),
                  pl.BlockSpec((1, LANES), lambda b, c: (0, 0)),
                  pl.BlockSpec((1, LANES), lambda b, c: (0, 0))],
        out_specs=pl.BlockSpec((1, SSM_CHUNK, SSM_D_INNER), lambda b, c: (b, cidx(c), 0)),
        out_shape=jax.ShapeDtypeStruct((bsz, seq, SSM_D_INNER), F32),
        scratch_shapes=[pltpu.VMEM((SSM_BC, SSM_D_INNER), F32)],
        compiler_params=_params("parallel", "arbitrary"),
        name="ssd_bwd" if reverse else "ssd_fwd",
    )(xbc, xbc, xbc, a_grp, dt_bias, neg_a)


def _layernorm(x, g, b, eps=1e-5):
    mu = jnp.mean(x, -1, keepdims=True)
    xc = x - mu
    var = jnp.mean(xc * xc, -1, keepdims=True)
    return xc * lax.rsqrt(var + eps) * g + b


def _merge_kernel(x_ref, o_ref, yf_ref, yb_ref, xs_ref, z_ref, u_ref, gl_ref,
                  wa_ref, ws_ref, wc_ref, wo_ref, dsk_ref, ng_ref, cg_ref, cb_ref, bc_ref,
                  l1g_ref, l1b_ref, rw_ref, rb_ref,
                  x1_ref, x1g_ref, route_ref, cnt_ref, carry_ref):
    tm = x_ref.shape[0]

    @pl.when(pl.program_id(0) == 0)
    def _():
        carry_ref[...] = jnp.zeros_like(carry_ref)

    y_attn = jnp.dot(o_ref[...], wa_ref[...], preferred_element_type=F32)
    z = z_ref[...].astype(F32)
    ys = (yf_ref[...] + yb_ref[...] + xs_ref[...] * dsk_ref[...]) * (z * jax.nn.sigmoid(z))
    gw = SSM_D_INNER // SSM_GROUPS
    ys = jnp.concatenate(
        [_rms(ys[:, g * gw:(g + 1) * gw], ng_ref[:, g * gw:(g + 1) * gw]) for g in range(SSM_GROUPS)], -1)
    y_ssm = jnp.dot(ys.astype(BF16), ws_ref[...], preferred_element_type=F32)
    uc = _layernorm(u_ref[...], cg_ref[...], cb_ref[...])
    uc = uc * jax.nn.sigmoid(uc)
    y_conv = jnp.dot(uc.astype(BF16), wc_ref[...], preferred_element_type=F32) + bc_ref[...]
    gl = gl_ref[...].astype(F32)
    mixed = (jax.nn.sigmoid(gl[:, :D_MODEL]) * y_attn
             + jax.nn.sigmoid(gl[:, D_MODEL:2 * D_MODEL]) * y_ssm
             + jax.nn.sigmoid(gl[:, 2 * D_MODEL:]) * y_conv)
    mixed = jnp.dot(mixed.astype(BF16), wo_ref[...], preferred_element_type=F32)
    x1 = _layernorm(DN_ALPHA * x_ref[...] + mixed, l1g_ref[...], l1b_ref[...])
    x1_ref[...] = x1
    for t in range(ROW_TILES):
        x1g_ref[pl.ds(t, tm, stride=ROW_TILES), :] = x1[:, t * LANES:(t + 1) * LANES]

    x_hi = x1.astype(BF16)
    x_lo = (x1 - x_hi.astype(F32)).astype(BF16)
    lg = jnp.dot(jnp.concatenate([x_hi, x_lo, x_hi], -1), rw_ref[...], preferred_element_type=F32) + rb_ref[...]
    lane = lax.broadcasted_iota(jnp.int32, (tm, LANES), 1).astype(F32)
    sels, vals, idxs = [], [], []
    for _ in range(TOP_K):
        m = jnp.max(lg, -1, keepdims=True)
        idx = jnp.min(jnp.where(lg == m, lane, float(LANES)), -1, keepdims=True)
        sel = lane == idx
        lg = jnp.where(sel, NEG_BIG * 2, lg)
        sels.append(sel)
        vals.append(m)
        idxs.append(idx)
    es = [jnp.exp(v - vals[0]) for v in vals]
    den = es[0] + es[1] + es[2] + es[3]
    hot = jnp.zeros((tm, LANES), F32)
    for sel in sels:
        hot = hot + sel.astype(F32)
    r = lax.broadcasted_iota(jnp.int32, (tm, tm), 0)
    c = lax.broadcasted_iota(jnp.int32, (tm, tm), 1)
    excl = jnp.dot((c < r).astype(BF16), hot.astype(BF16), preferred_element_type=F32) + carry_ref[...]
    route = jnp.zeros((tm, LANES), F32)
    for kk in range(TOP_K):
        rank = jnp.sum(jnp.where(sels[kk], excl, 0.0), -1, keepdims=True)
        route = jnp.where(lane == float(kk), idxs[kk], route)
        route = jnp.where(lane == float(TOP_K + kk), rank, route)
        route = jnp.where(lane == float(2 * TOP_K + kk), es[kk] / den, route)
    route_ref[...] = route
    carry_ref[...] = carry_ref[...] + jnp.sum(hot, 0, keepdims=True)
    cnt_ref[...] = carry_ref[...]


def _merge(x, o, yf, yb, xbc, z, u, gl, wa, ws, wc, wo, dsk, ng, cg, cb, bc, l1g, l1b, rw, rb, tm=256):
    t = x.shape[0]
    tm = min(tm, t)

    def rowb(width, col=0):
        return pl.BlockSpec((tm, width), lambda i: (i, col))

    def const(shape):
        return pl.BlockSpec(shape, lambda i: (0, 0))

    return pl.pallas_call(
        _merge_kernel,
        grid=(t // tm,),
        in_specs=[rowb(D_MODEL), rowb(MLA_W), rowb(SSM_D_INNER), rowb(SSM_D_INNER), rowb(SSM_D_INNER),
                  rowb(SSM_D_INNER), rowb(CNV_CH), rowb(3 * D_MODEL),
                  const((MLA_W, D_MODEL)), const((SSM_D_INNER, D_MODEL)), const((CNV_CH, D_MODEL)),
                  const((D_MODEL, D_MODEL)), const((1, SSM_D_INNER)), const((1, SSM_D_INNER)),
                  const((1, CNV_CH)), const((1, CNV_CH)), const((1, D_MODEL)),
                  const((1, D_MODEL)), const((1, D_MODEL)), const((3 * D_MODEL, LANES)), const((1, LANES))],
        out_specs=[rowb(D_MODEL),
                   pl.BlockSpec((tm * ROW_TILES, LANES), lambda i: (i, 0)),
                   rowb(LANES), const((1, LANES))],
        out_shape=[jax.ShapeDtypeStruct((t, D_MODEL), F32),
                   jax.ShapeDtypeStruct((t * ROW_TILES, LANES), F32),
                   jax.ShapeDtypeStruct((t, LANES), F32), jax.ShapeDtypeStruct((1, LANES), F32)],
        scratch_shapes=[pltpu.VMEM((1, LANES), F32)],
        compiler_params=_params("arbitrary"),
        name="merge",
    )(x, o, yf, yb, xbc, z, u, gl, wa, ws, wc, wo, dsk, ng, cg, cb, bc, l1g, l1b, rw, rb)


def _row_slice(ref, row):
    return ref.at[pl.ds(pl.multiple_of(row * ROW_TILES, ROW_TILES), ROW_TILES)]


def _from_row_tiles(ref, first, rows, pitch):
    return jnp.concatenate([ref[pl.ds(first + t, rows, stride=pitch), :] for t in range(ROW_TILES)], -1)


def _dispatch_kernel(dest_ref, x_ref, zero_hbm, xs_hbm, sem, *, tm):
    del zero_hbm

    def body(c, carry):
        for u in range(ISSUE_UNROLL):
            t = c * ISSUE_UNROLL + u
            src = _row_slice(x_ref, t)
            for kk in range(TOP_K):
                pltpu.make_async_copy(src, _row_slice(xs_hbm, dest_ref[0, 0, t * TOP_K + kk]), sem).start()
        return carry

    lax.fori_loop(0, tm // ISSUE_UNROLL, body, 0)
    for _ in range(TOP_K):
        pltpu.make_async_copy(x_ref, xs_hbm.at[pl.ds(0, tm * ROW_TILES)], sem).wait()


def _dispatch(dest, x1g, n_rows, tm=512):
    t = x1g.shape[0] // ROW_TILES
    tm = min(tm, t)
    nt = t // tm
    zeros = jnp.zeros((n_rows * ROW_TILES, LANES), F32)
    return pl.pallas_call(
        functools.partial(_dispatch_kernel, tm=tm),
        grid=(nt,),
        in_specs=[pl.BlockSpec((1, 1, tm * TOP_K), lambda i: (i, 0, 0), memory_space=pltpu.SMEM),
                  pl.BlockSpec((tm * ROW_TILES, LANES), lambda i: (i, 0)),
                  pl.BlockSpec(memory_space=pl.ANY)],
        out_specs=pl.BlockSpec(memory_space=pl.ANY),
        out_shape=jax.ShapeDtypeStruct((n_rows * ROW_TILES, LANES), F32),
        scratch_shapes=[pltpu.SemaphoreType.DMA(())],
        input_output_aliases={2: 0},
        compiler_params=_params("arbitrary"),
        name="dispatch",
    )(dest.reshape(nt, 1, tm * TOP_K), x1g, zeros)


def _moe_kernel(blk_e_ref, nused_ref, x_ref, wgu_ref, bgu_ref, wdn_ref, bdn_ref, o_ref, wgu_b, wdn_b):
    i = pl.program_id(0)

    @pl.when((i == 0) | (blk_e_ref[i] != blk_e_ref[jnp.maximum(i - 1, 0)]))
    def _():
        wgu_b[...] = wgu_ref[0, 0].astype(BF16)
        wdn_b[...] = wdn_ref[0, 0].astype(BF16)

    @pl.when(i < nused_ref[0])
    def _():
        xb = _from_row_tiles(x_ref, 0, MOE_ROWS, ROW_TILES).astype(BF16)
        h = jnp.dot(xb, wgu_b[...], preferred_element_type=F32) + bgu_ref[0]
        gate = jnp.minimum(h[:, :D_FF], SWIGLU_LIMIT)
        up = jnp.clip(h[:, D_FF:], -SWIGLU_LIMIT, SWIGLU_LIMIT)
        act = (up + 1.0) * (gate * jax.nn.sigmoid(SWIGLU_ALPHA * gate))
        y = jnp.dot(act.astype(BF16), wdn_b[...], preferred_element_type=F32) + bdn_ref[0]
        for t in range(ROW_TILES):
            o_ref[pl.ds(t, MOE_ROWS, stride=ROW_TILES), :] = y[:, t * LANES:(t + 1) * LANES]

    @pl.when(i >= nused_ref[0])
    def _():
        o_ref[...] = jnp.zeros_like(o_ref)


def _moe_experts(blk_e, nused, xs, wgu, bgu, wdn, bdn, layer):
    n_blocks = blk_e.shape[0]
    grid_spec = pltpu.PrefetchScalarGridSpec(
        num_scalar_prefetch=2,
        grid=(n_blocks,),
        in_specs=[pl.BlockSpec((MOE_ROWS * ROW_TILES, LANES), lambda i, be, nu: (i, 0)),
                  pl.BlockSpec((1, 1, D_MODEL, 2 * D_FF), lambda i, be, nu: (layer, be[i], 0, 0)),
                  pl.BlockSpec((1, 1, 2 * D_FF), lambda i, be, nu: (be[i], 0, 0)),
                  pl.BlockSpec((1, 1, D_FF, D_MODEL), lambda i, be, nu: (layer, be[i], 0, 0)),
                  pl.BlockSpec((1, 1, D_MODEL), lambda i, be, nu: (be[i], 0, 0))],
        out_specs=pl.BlockSpec((MOE_ROWS * ROW_TILES, LANES), lambda i, be, nu: (i, 0)),
        scratch_shapes=[pltpu.VMEM((D_MODEL, 2 * D_FF), BF16), pltpu.VMEM((D_FF, D_MODEL), BF16)],
    )
    return pl.pallas_call(
        _moe_kernel,
        grid_spec=grid_spec,
        out_shape=jax.ShapeDtypeStruct(xs.shape, F32),
        compiler_params=_params("arbitrary"),
        name="moe_experts",
    )(blk_e, nused, xs, wgu, bgu, wdn, bdn)


def _gather_topk_rows(dest_ref, yb_hbm, dst, sem, tm):
    def body(c, carry):
        for u in range(ISSUE_UNROLL):
            t = c * ISSUE_UNROLL + u
            for kk in range(TOP_K):
                pltpu.make_async_copy(_row_slice(yb_hbm, dest_ref[0, 0, t * TOP_K + kk]),
                                      _row_slice(dst, kk * tm + t), sem).start()
        return carry
    lax.fori_loop(0, tm // ISSUE_UNROLL, body, 0)


def _combine_kernel(dest_first_ref, dest_next_ref, yb_hbm, x1_ref, route_ref, g_ref, b_ref, x2_ref, x2b_ref,
                    buf, sem, *, tm):
    i = pl.program_id(0)
    n = pl.num_programs(0)
    slot = i % 2

    @pl.when(i == 0)
    def _():
        _gather_topk_rows(dest_first_ref, yb_hbm, buf.at[0], sem.at[0], tm)

    @pl.when(i + 1 < n)
    def _():
        _gather_topk_rows(dest_next_ref, yb_hbm, buf.at[1 - slot], sem.at[1 - slot], tm)

    pltpu.make_async_copy(yb_hbm.at[pl.ds(0, tm * TOP_K * ROW_TILES)], buf.at[slot], sem.at[slot]).wait()
    route = route_ref[...]
    ffn = jnp.zeros((tm, D_MODEL), F32)
    for kk in range(TOP_K):
        rows = _from_row_tiles(buf.at[slot], kk * tm * ROW_TILES, tm, ROW_TILES)
        ffn = ffn + rows * route[:, 2 * TOP_K + kk:2 * TOP_K + kk + 1]
    x2 = _layernorm(DN_ALPHA * x1_ref[...] + ffn, g_ref[...], b_ref[...])
    x2_ref[...] = x2
    x2b_ref[...] = x2.astype(BF16)


def _combine(dest, yb, x1, route, g, b, tm=512):
    t = x1.shape[0]
    tm = min(tm, t)
    nt = t // tm
    dest3 = dest.reshape(nt, 1, tm * TOP_K)
    return pl.pallas_call(
        functools.partial(_combine_kernel, tm=tm),
        grid=(nt,),
        in_specs=[pl.BlockSpec((1, 1, tm * TOP_K), lambda i: (0, 0, 0), memory_space=pltpu.SMEM),
                  pl.BlockSpec((1, 1, tm * TOP_K), lambda i: (jnp.minimum(i + 1, nt - 1), 0, 0),
                               memory_space=pltpu.SMEM),
                  pl.BlockSpec(memory_space=pl.ANY),
                  pl.BlockSpec((tm, D_MODEL), lambda i: (i, 0)),
                  pl.BlockSpec((tm, LANES), lambda i: (i, 0)),
                  pl.BlockSpec((1, D_MODEL), lambda i: (0, 0)),
                  pl.BlockSpec((1, D_MODEL), lambda i: (0, 0))],
        out_specs=[pl.BlockSpec((tm, D_MODEL), lambda i: (i, 0)),
                   pl.BlockSpec((tm, D_MODEL), lambda i: (i, 0))],
        out_shape=[jax.ShapeDtypeStruct((t, D_MODEL), F32), jax.ShapeDtypeStruct((t, D_MODEL), BF16)],
        scratch_shapes=[pltpu.VMEM((2, tm * TOP_K * ROW_TILES, LANES), F32),
                        pltpu.SemaphoreType.DMA((2,))],
        compiler_params=_params("arbitrary"),
        name="combine",
    )(dest3, dest3, yb, x1, route, g, b)


def _pad_cols(w, width):
    return jnp.pad(w, ((0, 0), (0, width - w.shape[1])))


def _pack_in_proj(w_in, b_in):
    offs = np.concatenate([[0], np.cumsum(IN_SIZES)])
    wb = jnp.concatenate([w_in, b_in[None, :]], 0)

    def piece(i):
        return wb[:, offs[i]:offs[i + 1]]

    kr = piece(2)
    kr_swapped = jnp.concatenate([kr[:, MLA_DR // 2:], kr[:, :MLA_DR // 2]], 1)
    dt = piece(5)
    grp_a = jnp.concatenate([piece(0), piece(1), _pad_cols(jnp.concatenate([kr, kr_swapped], 1), LANES),
                             _pad_cols(dt[:, :SSM_HEADS], LANES), _pad_cols(dt[:, SSM_HEADS:], LANES)], 1)
    groups = {"a": (grp_a, F32), "z": (piece(3), BF16), "xbc": (piece(4), F32),
              "cnv": (jnp.concatenate([piece(6), piece(7)], 1), F32), "gate": (piece(8), BF16)}
    return {k: (v[:-1].astype(BF16), v[-1:], dt_) for k, (v, dt_) in groups.items()}


def _pack_mla(w_uq, w_ukv, w_br_attn, seq):
    hq = MLA_DN + MLA_DR
    half = MLA_DR // 2
    wq = w_uq.reshape(MLA_Q_LORA, MLA_HEADS, hq)
    zq = jnp.zeros((MLA_Q_LORA, MLA_HEADS, HEAD_PAD - hq), F32)
    wq_main = jnp.concatenate([wq, zq], -1).reshape(MLA_Q_LORA, MLA_W)
    wq_swap = jnp.concatenate([jnp.zeros((MLA_Q_LORA, MLA_HEADS, MLA_DN), F32),
                               wq[..., MLA_DN + half:], wq[..., MLA_DN:MLA_DN + half], zq],
                              -1).reshape(MLA_Q_LORA, MLA_W)
    wkv = w_ukv.reshape(MLA_KV_LORA, MLA_HEADS, MLA_DN + MLA_DV)
    zk = jnp.zeros((MLA_KV_LORA, MLA_HEADS, HEAD_PAD - MLA_DN), F32)
    wk = jnp.concatenate([wkv[..., :MLA_DN], zk], -1).reshape(MLA_KV_LORA, MLA_W)
    wv = jnp.concatenate([wkv[..., MLA_DN:], zk], -1).reshape(MLA_KV_LORA, MLA_W)
    e2 = np.zeros((LANES, MLA_HEADS, HEAD_PAD), np.float32)
    for j in range(MLA_DR):
        e2[j, :, MLA_DN + j] = 1.0
        e2[MLA_DR + j, :, MLA_DN + j] = 1.0
    vone = np.zeros((MLA_HEADS, HEAD_PAD), np.float32)
    vone[:, MLA_DV] = 1.0
    wbr = jnp.concatenate([w_br_attn.reshape(MLA_HEADS, MLA_DV, D_MODEL),
                           jnp.zeros((MLA_HEADS, HEAD_PAD - MLA_DV, D_MODEL), F32)], 1).reshape(MLA_W, D_MODEL)
    pos = jnp.arange(seq, dtype=F32)
    inv = ROPE_THETA ** (-jnp.arange(0, MLA_DR, 2, dtype=F32) / MLA_DR)
    ang = pos[:, None] * inv[None, :]
    cos, sin = jnp.cos(ang), jnp.sin(ang)
    scale = (MLA_DN + MLA_DR) ** -0.5
    ones = jnp.ones((seq, MLA_DN), F32)
    zpad = jnp.zeros((seq, HEAD_PAD - hq), F32)
    cq = jnp.tile(jnp.concatenate([ones, cos, cos, zpad], 1) * scale, (1, MLA_HEADS))
    sq = jnp.tile(jnp.concatenate([0 * ones, -sin, sin, zpad], 1) * scale, (1, MLA_HEADS))
    tk = jnp.concatenate([cos, cos, -sin, sin, jnp.zeros((seq, LANES - 2 * MLA_DR), F32)], 1)
    return dict(wq=wq_main.astype(BF16), wqs=wq_swap.astype(BF16), wk=wk.astype(BF16), wv=wv.astype(BF16),
                e2=jnp.asarray(e2.reshape(LANES, MLA_W), BF16), vone=jnp.asarray(vone.reshape(1, MLA_W)),
                wbr=wbr.astype(BF16), cq=cq, sq=sq, tk=tk)


def _route_tables(route, cnt, n_tok):
    idx = route[:, :TOP_K].astype(jnp.int32)
    rank = route[:, TOP_K:2 * TOP_K].astype(jnp.int32)
    counts = cnt[0, :N_EXPERTS].astype(jnp.int32)
    n_blocks = -(-(n_tok * TOP_K + N_EXPERTS * (MOE_ROWS - 1)) // MOE_ROWS)
    padded = (counts + MOE_ROWS - 1) // MOE_ROWS * MOE_ROWS
    pad_end = jnp.cumsum(padded)
    pad_start = pad_end - padded
    onehot = idx[..., None] == jnp.arange(N_EXPERTS, dtype=jnp.int32)
    dest = (jnp.sum(jnp.where(onehot, pad_start, 0), -1) + rank).reshape(-1)
    blk_first = jnp.arange(n_blocks, dtype=jnp.int32) * MOE_ROWS
    blk_e = jnp.minimum(jnp.sum((pad_end[None, :] <= blk_first[:, None]).astype(jnp.int32), -1), N_EXPERTS - 1)
    nused = pad_end[-1:] // MOE_ROWS
    return dest, blk_e, nused, n_blocks * MOE_ROWS


def kernel(x, w_in, b_in, mla_q_norm, mla_kv_norm, mla_w_uq, mla_w_ukv, w_br_attn, ssm_conv_w, ssm_conv_b, ssm_dt_bias, ssm_a_log, ssm_d, ssm_norm, w_br_ssm, cnv_dw_w, cnv_dw_b, cnv_ln_g, cnv_ln_b, w_br_conv, b_br_conv, w_out, ln1_g, ln1_b, router_w, router_b, moe_w_gate_up, moe_b_gate_up, moe_w_down, moe_b_down, ln2_g, ln2_b):
    bsz, seq, d = x.shape
    n_tok = bsz * seq
    xf = x.reshape(n_tok, d)
    xb = xf.astype(BF16)
    for l in range(DEPTH):
        proj = _pack_in_proj(w_in[l], b_in[l])
        mla = _pack_mla(mla_w_uq[l], mla_w_ukv[l], w_br_attn[l], seq)
        a_grp = _linear(xb, *proj["a"])
        z = _linear(xb, *proj["z"])
        xbc_raw = _linear(xb, *proj["xbc"], tn=768)
        cnv_raw = _linear(xb, *proj["cnv"])
        gate_logits = _linear(xb, *proj["gate"], tn=1024)

        q, k, v = _mla_prep(a_grp, seq, mla_q_norm[l][None], mla_kv_norm[l][None], mla["wq"], mla["wqs"],
                            mla["wk"], mla["wv"], mla["e2"], mla["cq"], mla["sq"], mla["tk"], mla["vone"])
        attn = _attention(q.reshape(bsz, seq, MLA_W), k.reshape(bsz, seq, MLA_W), v.reshape(bsz, seq, MLA_W))

        xbc = _dwconv(xbc_raw.reshape(bsz, seq, SSM_CONV_DIM), ssm_conv_w[l], ssm_conv_b[l],
                      glu=False, silu_out=True)
        a3 = a_grp.reshape(bsz, seq, GROUP_A)
        dtb = jnp.pad(ssm_dt_bias[l], ((0, 0), (0, LANES - SSM_HEADS)))
        nega = jnp.pad(-jnp.exp(ssm_a_log[l]), ((0, 0), (0, LANES - SSM_HEADS)))
        y_fwd = _ssd(xbc, a3, dtb[0:1], nega[0:1], reverse=False)
        y_bwd = _ssd(xbc, a3, dtb[1:2], nega[1:2], reverse=True)

        u = _dwconv(cnv_raw.reshape(bsz, seq, 2 * CNV_CH), cnv_dw_w[l], cnv_dw_b[l], glu=True, silu_out=False)

        rw = jnp.pad(router_w[l], ((0, 0), (0, LANES - N_EXPERTS)))
        rw_hi = rw.astype(BF16)
        rw_lo = (rw - rw_hi.astype(F32)).astype(BF16)
        rw = jnp.concatenate([rw_hi, rw_hi, rw_lo], 0)
        rb = jnp.pad(router_b[l], (0, LANES - N_EXPERTS), constant_values=NEG_BIG)[None]
        x1, x1g, route, cnt = _merge(
            xf, attn.reshape(n_tok, MLA_W), y_fwd.reshape(n_tok, SSM_D_INNER), y_bwd.reshape(n_tok, SSM_D_INNER),
            xbc.reshape(n_tok, SSM_CONV_DIM), z, u.reshape(n_tok, CNV_CH), gate_logits,
            mla["wbr"], w_br_ssm[l].astype(BF16), w_br_conv[l].astype(BF16), w_out[l].astype(BF16),
            jnp.repeat(ssm_d[l], SSM_HEAD_DIM)[None], ssm_norm[l][None], cnv_ln_g[l][None], cnv_ln_b[l][None],
            b_br_conv[l][None], ln1_g[l][None], ln1_b[l][None], rw, rb)

        dest, blk_e, nused, n_rows = _route_tables(route, cnt, n_tok)
        xs = _dispatch(dest, x1g, n_rows)
        yb = _moe_experts(blk_e, nused, xs, moe_w_gate_up, moe_b_gate_up[l][:, None, :], moe_w_down,
                          moe_b_down[l][:, None, :], l)
        xf, xb = _combine(dest, yb, x1, route, ln2_g[l][None], ln2_b[l][None])
    return xf.reshape(bsz, seq, d)
```

```python
import functools
import math

import numpy as np
import jax
import jax.numpy as jnp
from jax import lax
from jax.experimental import pallas as pl
from jax.experimental.pallas import tpu as pltpu

F32 = jnp.float32
BF16 = jnp.bfloat16
HIGHEST = lax.Precision.HIGHEST

LANES = 128
SUBLANES = 8
VMEM_LIMIT_BYTES = 56 * 1024 * 1024

D_MODEL = 1024
DEPTH = 2
MLA_HEADS = 8
MLA_Q_LORA = 384
MLA_KV_LORA = 256
MLA_DN = 64
MLA_DR = 32
MLA_DV = 64
ROPE_THETA = 10000.0
SSM_HEADS = 16
SSM_HEAD_DIM = 64
SSM_D_INNER = SSM_HEADS * SSM_HEAD_DIM
SSM_GROUPS = 4
SSM_STATE = 64
SSM_CONV = 5
SSM_CHUNK = 128
SSM_BC = SSM_GROUPS * SSM_STATE
SSM_CONV_DIM = SSM_D_INNER + 2 * SSM_BC
CNV_CH = 512
CNV_WIDTH = 31
N_EXPERTS = 32
TOP_K = 4
D_FF = 1024
SWIGLU_LIMIT = 7.0
SWIGLU_ALPHA = 1.702
DN_ALPHA = (2 * DEPTH) ** 0.25
IN_SIZES = (MLA_Q_LORA, MLA_KV_LORA, MLA_DR, SSM_D_INNER, SSM_CONV_DIM, 2 * SSM_HEADS,
            CNV_CH, CNV_CH, 3 * D_MODEL)
HEAD_PAD = LANES
MLA_W = MLA_HEADS * HEAD_PAD
GROUP_A = 1024
CONV_HALO = 16
MOE_ROWS = 512
ISSUE_UNROLL = 8
ROW_TILES = D_MODEL // LANES
NEG_BIG = -1e30


def _params(*sem):
    return pltpu.CompilerParams(dimension_semantics=sem, vmem_limit_bytes=VMEM_LIMIT_BYTES)


def _linear_kernel(x_ref, w_ref, b_ref, o_ref):
    acc = jnp.dot(x_ref[...], w_ref[...], preferred_element_type=F32)
    o_ref[...] = (acc + b_ref[...]).astype(o_ref.dtype)


def _linear(x, w, b, out_dtype, tm=1024, tn=None):
    m, k = x.shape
    n = w.shape[1]
    tm = min(tm, m)
    tn = n if tn is None else tn
    return pl.pallas_call(
        _linear_kernel,
        grid=(n // tn, m // tm),
        in_specs=[pl.BlockSpec((tm, k), lambda j, i: (i, 0)),
                  pl.BlockSpec((k, tn), lambda j, i: (0, j)),
                  pl.BlockSpec((1, tn), lambda j, i: (0, j))],
        out_specs=pl.BlockSpec((tm, tn), lambda j, i: (i, j)),
        out_shape=jax.ShapeDtypeStruct((m, n), out_dtype),
        compiler_params=_params("parallel", "parallel"),
        name="linear",
    )(x, w, b)


def _rms(x, g, eps=1e-6):
    return x * lax.rsqrt(jnp.mean(x * x, -1, keepdims=True) + eps) * g


def _mla_prep_kernel(a_ref, gq_ref, gkv_ref, wq_ref, wqs_ref, wk_ref, wv_ref, e2_ref,
                     cq_ref, sq_ref, tk_ref, vone_ref, q_ref, k_ref, v_ref):
    a = a_ref[...]
    c_q = a[:, :MLA_Q_LORA]
    c_kv = a[:, MLA_Q_LORA:MLA_Q_LORA + MLA_KV_LORA]
    kr = a[:, MLA_Q_LORA + MLA_KV_LORA:]
    qn = _rms(c_q, gq_ref[...]).astype(BF16)
    kvn = _rms(c_kv, gkv_ref[...]).astype(BF16)
    q = (jnp.dot(qn, wq_ref[...], preferred_element_type=F32) * cq_ref[...]
         + jnp.dot(qn, wqs_ref[...], preferred_element_type=F32) * sq_ref[...])
    q_ref[...] = q.astype(BF16)
    krp = (kr * tk_ref[...]).astype(BF16)
    k = (jnp.dot(kvn, wk_ref[...], preferred_element_type=F32)
         + jnp.dot(krp, e2_ref[...], preferred_element_type=F32))
    k_ref[...] = k.astype(BF16)
    v = jnp.dot(kvn, wv_ref[...], preferred_element_type=F32) + vone_ref[...]
    v_ref[...] = v.astype(BF16)


def _mla_prep(a, seq, gq, gkv, wq, wqs, wk, wv, e2, cq, sq, tk, vone, tm=512):
    t = a.shape[0]
    tm = min(tm, seq)
    nper = seq // tm
    wa = MLA_Q_LORA + MLA_KV_LORA + LANES

    def const(shape):
        return pl.BlockSpec(shape, lambda i: (0, 0))

    def tab(width):
        return pl.BlockSpec((tm, width), lambda i: (i % nper, 0))

    out = jax.ShapeDtypeStruct((t, MLA_W), BF16)
    return pl.pallas_call(
        _mla_prep_kernel,
        grid=(t // tm,),
        in_specs=[pl.BlockSpec((tm, wa), lambda i: (i, 0)),
                  const((1, MLA_Q_LORA)), const((1, MLA_KV_LORA)),
                  const((MLA_Q_LORA, MLA_W)), const((MLA_Q_LORA, MLA_W)),
                  const((MLA_KV_LORA, MLA_W)), const((MLA_KV_LORA, MLA_W)),
                  const((LANES, MLA_W)),
                  tab(MLA_W), tab(MLA_W), tab(LANES), const((1, MLA_W))],
        out_specs=[pl.BlockSpec((tm, MLA_W), lambda i: (i, 0))] * 3,
        out_shape=[out, out, out],
        compiler_params=_params("parallel"),
        name="mla_prep",
    )(a, gq, gkv, wq, wqs, wk, wv, e2, cq, sq, tk, vone)


def _attn_kernel(q_ref, k_ref, v_ref, o_ref, *, sub):
    for r in range(q_ref.shape[1] // sub):
        rows = slice(r * sub, (r + 1) * sub)
        s = lax.dot_general(q_ref[0, rows], k_ref[0], (((1,), (1,)), ((), ())),
                            preferred_element_type=F32)
        m = jnp.max(s, -1, keepdims=True)
        p = jnp.exp(s - m).astype(BF16)
        o = jnp.dot(p, v_ref[0], preferred_element_type=F32)
        o_ref[0, rows] = (o / o[:, MLA_DV:MLA_DV + 1]).astype(o_ref.dtype)


def _attention(q, k, v, tq=1024, sub=256):
    b, s, _ = q.shape
    tq = min(tq, s)
    return pl.pallas_call(
        functools.partial(_attn_kernel, sub=min(sub, tq)),
        grid=(b, MLA_HEADS, s // tq),
        in_specs=[pl.BlockSpec((1, tq, HEAD_PAD), lambda bi, h, i: (bi, i, h)),
                  pl.BlockSpec((1, s, HEAD_PAD), lambda bi, h, i: (bi, 0, h)),
                  pl.BlockSpec((1, s, HEAD_PAD), lambda bi, h, i: (bi, 0, h))],
        out_specs=pl.BlockSpec((1, tq, HEAD_PAD), lambda bi, h, i: (bi, i, h)),
        out_shape=jax.ShapeDtypeStruct((b, s, MLA_W), BF16),
        compiler_params=_params("parallel", "parallel", "parallel"),
        name="attention",
    )(q, k, v)


def _dwconv_kernel(*refs, width, glu, silu_out, seq, rows):
    if glu:
        a_ref, g_ref, w_ref, b_ref, o_ref, pad_ref = refs
        pre = a_ref[0] * jax.nn.sigmoid(g_ref[0])
    else:
        x_ref, w_ref, b_ref, o_ref, pad_ref = refs
        pre = x_ref[0]
    ch = o_ref.shape[-1]
    halo = jnp.zeros((CONV_HALO, ch), F32)
    pad_ref[0:CONV_HALO, :] = halo
    pad_ref[CONV_HALO + seq:2 * CONV_HALO + seq, :] = halo
    pad_ref[CONV_HALO:CONV_HALO + seq, :] = pre
    half = (width - 1) // 2
    win_rows = rows + 2 * CONV_HALO

    def body(c, carry):
        base = pl.multiple_of(c * rows, rows)
        win = pad_ref[pl.ds(base, win_rows), :]
        acc = jnp.zeros((rows, ch), F32) + b_ref[...]
        for t in range(width):
            off = CONV_HALO - half + t
            acc = acc + pltpu.roll(win, win_rows - off, 0)[:rows] * w_ref[t:t + 1, :]
        if silu_out:
            acc = acc * jax.nn.sigmoid(acc)
        o_ref[0, pl.ds(base, rows), :] = acc
        return carry

    lax.fori_loop(0, seq // rows, body, 0)


def _dwconv(x, w, b, *, glu, silu_out, rows=64):
    bsz, seq, cin = x.shape
    width, ch = w.shape
    nct = ch // LANES
    kern = functools.partial(_dwconv_kernel, width=width, glu=glu, silu_out=silu_out, seq=seq,
                             rows=min(rows, seq))
    xspec = pl.BlockSpec((1, seq, LANES), lambda bi, j: (bi, 0, j))
    in_specs = [xspec]
    args = [x]
    if glu:
        in_specs.append(pl.BlockSpec((1, seq, LANES), lambda bi, j: (bi, 0, j + nct)))
        args.append(x)
    in_specs += [pl.BlockSpec((width, LANES), lambda bi, j: (0, j)),
                 pl.BlockSpec((1, LANES), lambda bi, j: (0, j))]
    return pl.pallas_call(
        kern,
        grid=(bsz, nct),
        in_specs=in_specs,
        out_specs=pl.BlockSpec((1, seq, LANES), lambda bi, j: (bi, 0, j)),
        out_shape=jax.ShapeDtypeStruct((bsz, seq, ch), F32),
        scratch_shapes=[pltpu.VMEM((seq + 2 * CONV_HALO, LANES), F32)],
        compiler_params=_params("parallel", "parallel"),
        name="dwconv",
    )(*args, w, b.reshape(1, ch))


def _softplus(x):
    return jnp.maximum(x, 0.0) + jnp.log1p(jnp.exp(-jnp.abs(x)))


def _ssd_kernel(*refs):
    f_in, b_in = refs[0:6], refs[6:12]
    yf_ref, yb_ref, hf_ref, hb_ref = refs[12:16]
    _ssd_chunk(*f_in, yf_ref, hf_ref, reverse=False)
    _ssd_chunk(*b_in, yb_ref, hb_ref, reverse=True)


def _ssd_chunk(xs_ref, bm_ref, cm_ref, dt_ref, dtb_ref, nega_ref, y_ref, h_ref, *, reverse):
    ln = SSM_CHUNK

    @pl.when(pl.program_id(1) == 0)
    def _():
        h_ref[...] = jnp.zeros_like(h_ref)

    dt = _softplus(dt_ref[0] + dtb_ref[...])
    a = dt * nega_ref[...]
    row = lax.broadcasted_iota(jnp.int32, (ln, ln), 0)
    col = lax.broadcasted_iota(jnp.int32, (ln, ln), 1)
    tri = (col >= row) if reverse else (col <= row)
    acs = jnp.dot(tri.astype(F32), a, precision=HIGHEST, preferred_element_type=F32)
    tot = acs[0:1] if reverse else acs[ln - 1:ln]
    wst = dt * jnp.exp(tot - acs)
    etot = jnp.exp(tot)
    acs_t = acs.T
    dt_t = dt.T
    wst_t = wst.T

    xs_b = xs_ref[0].astype(BF16)
    cm_b = cm_ref[0].astype(BF16)
    bm_t = bm_ref[0].T
    bm_tb = bm_t.astype(BF16)
    y_off = jnp.dot(cm_b, h_ref[...].astype(BF16), preferred_element_type=F32)

    glane = lax.broadcasted_iota(jnp.int32, (ln, SSM_BC), 1) // SSM_STATE
    lo = lax.broadcasted_iota(jnp.int32, (ln, LANES), 1) < SSM_HEAD_DIM
    lo_s = lax.broadcasted_iota(jnp.int32, (SSM_STATE, LANES), 1) < SSM_HEAD_DIM
    lo_1 = lax.broadcasted_iota(jnp.int32, (1, LANES), 1) < SSM_HEAD_DIM
    heads_per_group = SSM_HEADS // SSM_GROUPS
    for g in range(SSM_GROUPS):
        cmg = jnp.where(glane == g, cm_b, jnp.zeros_like(cm_b))
        cb = jnp.dot(cmg, bm_tb, preferred_element_type=F32)
        bm_tg = bm_t[g * SSM_STATE:(g + 1) * SSM_STATE, :]
        for j in range(heads_per_group // 2):
            pair = g * (heads_per_group // 2) + j
            lanes = slice(pair * LANES, (pair + 1) * LANES)
            xp = xs_b[:, lanes]
            diag, ecol, st, et = [], [], [], []
            for hh in range(2):
                h = 2 * pair + hh
                colb = jnp.broadcast_to(acs[:, h:h + 1], (ln, LANES))
                seg = colb - acs_t[h:h + 1, :]
                decay = jnp.exp(jnp.where(tri, seg, -jnp.inf))
                mat = (cb * decay * dt_t[h:h + 1, :]).astype(BF16)
                diag.append(jnp.dot(mat, xp, preferred_element_type=F32))
                ecol.append(jnp.exp(colb))
                st.append(jnp.dot((bm_tg * wst_t[h:h + 1, :]).astype(BF16), xp,
                                  preferred_element_type=F32))
                et.append(jnp.broadcast_to(etot[:, h:h + 1], (1, LANES)))
            y_ref[0, :, lanes] = (jnp.where(lo, diag[0], diag[1])
                                  + y_off[:, lanes] * jnp.where(lo, ecol[0], ecol[1]))
            rows = slice(g * SSM_STATE, (g + 1) * SSM_STATE)
            h_ref[rows, lanes] = (h_ref[rows, lanes] * jnp.where(lo_1, et[0], et[1])
                                  + jnp.where(lo_s, st[0], st[1]))


def _ssd(xbc, a_grp, dt_bias, neg_a):
    bsz, seq, _ = xbc.shape
    nc = seq // SSM_CHUNK

    def direction(reverse):
        dt_tile = 7 if reverse else 6
        row = 1 if reverse else 0

        def cidx(c):
            return (nc - 1 - c) if reverse else c

        ins = [pl.BlockSpec((1, SSM_CHUNK, SSM_D_INNER), lambda b, c: (b, cidx(c), 0)),
               pl.BlockSpec((1, SSM_CHUNK, SSM_BC), lambda b, c: (b, cidx(c), SSM_D_INNER // SSM_BC)),
               pl.BlockSpec((1, SSM_CHUNK, SSM_BC), lambda b, c: (b, cidx(c), SSM_D_INNER // SSM_BC + 1)),
               pl.BlockSpec((1, SSM_CHUNK, LANES), lambda b, c: (b, cidx(c), dt_tile)),
               pl.BlockSpec((1, LANES), lambda b, c: (0, 0)),
               pl.BlockSpec((1, LANES), lambda b, c: (0, 0))]
        out = pl.BlockSpec((1, SSM_CHUNK, SSM_D_INNER), lambda b, c: (b, cidx(c), 0))
        return ins, out, (xbc, xbc, xbc, a_grp, dt_bias[row:row + 1], neg_a[row:row + 1])

    f_ins, f_out, f_args = direction(False)
    b_ins, b_out, b_args = direction(True)
    y = jax.ShapeDtypeStruct((bsz, seq, SSM_D_INNER), F32)
    return pl.pallas_call(
        _ssd_kernel,
        grid=(bsz, nc),
        in_specs=f_ins + b_ins,
        out_specs=[f_out, b_out],
        out_shape=[y, y],
        scratch_shapes=[pltpu.VMEM((SSM_BC, SSM_D_INNER), F32), pltpu.VMEM((SSM_BC, SSM_D_INNER), F32)],
        compiler_params=_params("parallel", "arbitrary"),
        name="ssd",
    )(*f_args, *b_args)


def _layernorm(x, g, b, eps=1e-5):
    mu = jnp.mean(x, -1, keepdims=True)
    xc = x - mu
    var = jnp.mean(xc * xc, -1, keepdims=True)
    return xc * lax.rsqrt(var + eps) * g + b


def _merge_kernel(x_ref, o_ref, yf_ref, yb_ref, xs_ref, z_ref, u_ref, gl_ref,
                  wa_ref, ws_ref, wc_ref, wo_ref, dsk_ref, ng_ref, cg_ref, cb_ref, bc_ref,
                  l1g_ref, l1b_ref, rw_ref, rb_ref,
                  x1_ref, x1g_ref, route_ref, cnt_ref, carry_ref):
    tm = x_ref.shape[0]

    @pl.when(pl.program_id(0) == 0)
    def _():
        carry_ref[...] = jnp.zeros_like(carry_ref)

    y_attn = jnp.dot(o_ref[...], wa_ref[...], preferred_element_type=F32)
    z = z_ref[...].astype(F32)
    ys = (yf_ref[...] + yb_ref[...] + xs_ref[...] * dsk_ref[...]) * (z * jax.nn.sigmoid(z))
    gw = SSM_D_INNER // SSM_GROUPS
    ys = jnp.concatenate(
        [_rms(ys[:, g * gw:(g + 1) * gw], ng_ref[:, g * gw:(g + 1) * gw]) for g in range(SSM_GROUPS)], -1)
    y_ssm = jnp.dot(ys.astype(BF16), ws_ref[...], preferred_element_type=F32)
    uc = _layernorm(u_ref[...], cg_ref[...], cb_ref[...])
    uc = uc * jax.nn.sigmoid(uc)
    y_conv = jnp.dot(uc.astype(BF16), wc_ref[...], preferred_element_type=F32) + bc_ref[...]
    gl = gl_ref[...].astype(F32)
    mixed = (jax.nn.sigmoid(gl[:, :D_MODEL]) * y_attn
             + jax.nn.sigmoid(gl[:, D_MODEL:2 * D_MODEL]) * y_ssm
             + jax.nn.sigmoid(gl[:, 2 * D_MODEL:]) * y_conv)
    mixed = jnp.dot(mixed.astype(BF16), wo_ref[...], preferred_element_type=F32)
    x1 = _layernorm(DN_ALPHA * x_ref[...] + mixed, l1g_ref[...], l1b_ref[...])
    x1_ref[...] = x1
    for t in range(ROW_TILES):
        x1g_ref[pl.ds(t, tm, stride=ROW_TILES), :] = x1[:, t * LANES:(t + 1) * LANES]

    x_hi = x1.astype(BF16)
    x_lo = (x1 - x_hi.astype(F32)).astype(BF16)
    lg = jnp.dot(jnp.concatenate([x_hi, x_lo, x_hi], -1), rw_ref[...], preferred_element_type=F32) + rb_ref[...]
    lane = lax.broadcasted_iota(jnp.int32, (tm, LANES), 1).astype(F32)
    sels, vals, idxs = [], [], []
    for _ in range(TOP_K):
        m = jnp.max(lg, -1, keepdims=True)
        idx = jnp.min(jnp.where(lg == m, lane, float(LANES)), -1, keepdims=True)
        sel = lane == idx
        lg = jnp.where(sel, NEG_BIG * 2, lg)
        sels.append(sel)
        vals.append(m)
        idxs.append(idx)
    es = [jnp.exp(v - vals[0]) for v in vals]
    den = es[0] + es[1] + es[2] + es[3]
    hot = jnp.zeros((tm, LANES), F32)
    for sel in sels:
        hot = hot + sel.astype(F32)
    r = lax.broadcasted_iota(jnp.int32, (tm, tm), 0)
    c = lax.broadcasted_iota(jnp.int32, (tm, tm), 1)
    excl = jnp.dot((c < r).astype(BF16), hot.astype(BF16), preferred_element_type=F32) + carry_ref[...]
    route = jnp.zeros((tm, LANES), F32)
    for kk in range(TOP_K):
        rank = jnp.sum(jnp.where(sels[kk], excl, 0.0), -1, keepdims=True)
        route = jnp.where(lane == float(kk), idxs[kk], route)
        route = jnp.where(lane == float(TOP_K + kk), rank, route)
        route = jnp.where(lane == float(2 * TOP_K + kk), es[kk] / den, route)
    route_ref[...] = route
    carry_ref[...] = carry_ref[...] + jnp.sum(hot, 0, keepdims=True)
    cnt_ref[...] = carry_ref[...]


def _merge(x, o, yf, yb, xbc, z, u, gl, wa, ws, wc, wo, dsk, ng, cg, cb, bc, l1g, l1b, rw, rb, tm=256):
    t = x.shape[0]
    tm = min(tm, t)

    def rowb(width, col=0):
        return pl.BlockSpec((tm, width), lambda i: (i, col))

    def const(shape):
        return pl.BlockSpec(shape, lambda i: (0, 0))

    return pl.pallas_call(
        _merge_kernel,
        grid=(t // tm,),
        in_specs=[rowb(D_MODEL), rowb(MLA_W), rowb(SSM_D_INNER), rowb(SSM_D_INNER), rowb(SSM_D_INNER),
                  rowb(SSM_D_INNER), rowb(CNV_CH), rowb(3 * D_MODEL),
                  const((MLA_W, D_MODEL)), const((SSM_D_INNER, D_MODEL)), const((CNV_CH, D_MODEL)),
                  const((D_MODEL, D_MODEL)), const((1, SSM_D_INNER)), const((1, SSM_D_INNER)),
                  const((1, CNV_CH)), const((1, CNV_CH)), const((1, D_MODEL)),
                  const((1, D_MODEL)), const((1, D_MODEL)), const((3 * D_MODEL, LANES)), const((1, LANES))],
        out_specs=[rowb(D_MODEL),
                   pl.BlockSpec((tm * ROW_TILES, LANES), lambda i: (i, 0)),
                   rowb(LANES), const((1, LANES))],
        out_shape=[jax.ShapeDtypeStruct((t, D_MODEL), F32),
                   jax.ShapeDtypeStruct((t * ROW_TILES, LANES), F32),
                   jax.ShapeDtypeStruct((t, LANES), F32), jax.ShapeDtypeStruct((1, LANES), F32)],
        scratch_shapes=[pltpu.VMEM((1, LANES), F32)],
        compiler_params=_params("arbitrary"),
        name="merge",
    )(x, o, yf, yb, xbc, z, u, gl, wa, ws, wc, wo, dsk, ng, cg, cb, bc, l1g, l1b, rw, rb)


def _row_slice(ref, row):
    return ref.at[pl.ds(pl.multiple_of(row * ROW_TILES, ROW_TILES), ROW_TILES)]


def _from_row_tiles(ref, first, rows, pitch):
    return jnp.concatenate([ref[pl.ds(first + t, rows, stride=pitch), :] for t in range(ROW_TILES)], -1)


def _dispatch_kernel(dest_ref, x_ref, init_hbm, xs_hbm, sem, *, tm):
    del init_hbm

    def body(c, carry):
        for u in range(ISSUE_UNROLL):
            t = c * ISSUE_UNROLL + u
            src = _row_slice(x_ref, t)
            for kk in range(TOP_K):
                pltpu.make_async_copy(src, _row_slice(xs_hbm, dest_ref[0, 0, t * TOP_K + kk]),
                                      sem).start(priority=kk % 2)
        return carry

    lax.fori_loop(0, tm // ISSUE_UNROLL, body, 0)
    for _ in range(TOP_K):
        pltpu.make_async_copy(x_ref, xs_hbm.at[pl.ds(0, tm * ROW_TILES)], sem).wait()


def _dispatch(dest, x1g, n_rows, init=None, tm=512):
    t = x1g.shape[0] // ROW_TILES
    tm = min(tm, t)
    nt = t // tm
    zeros = jnp.zeros((n_rows * ROW_TILES, LANES), F32) if init is None else init
    return pl.pallas_call(
        functools.partial(_dispatch_kernel, tm=tm),
        grid=(nt,),
        in_specs=[pl.BlockSpec((1, 1, tm * TOP_K), lambda i: (i, 0, 0), memory_space=pltpu.SMEM),
                  pl.BlockSpec((tm * ROW_TILES, LANES), lambda i: (i, 0)),
                  pl.BlockSpec(memory_space=pl.ANY)],
        out_specs=pl.BlockSpec(memory_space=pl.ANY),
        out_shape=jax.ShapeDtypeStruct((n_rows * ROW_TILES, LANES), F32),
        scratch_shapes=[pltpu.SemaphoreType.DMA(())],
        input_output_aliases={2: 0},
        compiler_params=_params("arbitrary"),
        name="dispatch",
    )(dest.reshape(nt, 1, tm * TOP_K), x1g, zeros)


def _moe_kernel(blk_e_ref, nused_ref, x_ref, wgu_ref, bgu_ref, wdn_ref, bdn_ref, o_ref, wgu_b, wdn_b):
    i = pl.program_id(0)

    @pl.when((i == 0) | (blk_e_ref[i] != blk_e_ref[jnp.maximum(i - 1, 0)]))
    def _():
        wgu_b[...] = wgu_ref[0, 0].astype(BF16)
        wdn_b[...] = wdn_ref[0, 0].astype(BF16)

    @pl.when(i < nused_ref[0])
    def _():
        xb = _from_row_tiles(x_ref, 0, MOE_ROWS, ROW_TILES).astype(BF16)
        h = jnp.dot(xb, wgu_b[...], preferred_element_type=F32) + bgu_ref[0]
        gate = jnp.minimum(h[:, :D_FF], SWIGLU_LIMIT)
        up = jnp.clip(h[:, D_FF:], -SWIGLU_LIMIT, SWIGLU_LIMIT)
        act = (up + 1.0) * (gate * jax.nn.sigmoid(SWIGLU_ALPHA * gate))
        y = jnp.dot(act.astype(BF16), wdn_b[...], preferred_element_type=F32) + bdn_ref[0]
        for t in range(ROW_TILES):
            o_ref[pl.ds(t, MOE_ROWS, stride=ROW_TILES), :] = y[:, t * LANES:(t + 1) * LANES]

    @pl.when(i >= nused_ref[0])
    def _():
        o_ref[...] = jnp.zeros_like(o_ref)


def _moe_experts(blk_e, nused, xs, wgu, bgu, wdn, bdn, layer):
    n_blocks = blk_e.shape[0]
    grid_spec = pltpu.PrefetchScalarGridSpec(
        num_scalar_prefetch=2,
        grid=(n_blocks,),
        in_specs=[pl.BlockSpec((MOE_ROWS * ROW_TILES, LANES), lambda i, be, nu: (i, 0)),
                  pl.BlockSpec((1, 1, D_MODEL, 2 * D_FF), lambda i, be, nu: (layer, be[i], 0, 0)),
                  pl.BlockSpec((1, 1, 2 * D_FF), lambda i, be, nu: (be[i], 0, 0)),
                  pl.BlockSpec((1, 1, D_FF, D_MODEL), lambda i, be, nu: (layer, be[i], 0, 0)),
                  pl.BlockSpec((1, 1, D_MODEL), lambda i, be, nu: (be[i], 0, 0))],
        out_specs=pl.BlockSpec((MOE_ROWS * ROW_TILES, LANES), lambda i, be, nu: (i, 0)),
        scratch_shapes=[pltpu.VMEM((D_MODEL, 2 * D_FF), BF16), pltpu.VMEM((D_FF, D_MODEL), BF16)],
    )
    return pl.pallas_call(
        _moe_kernel,
        grid_spec=grid_spec,
        out_shape=jax.ShapeDtypeStruct(xs.shape, F32),
        compiler_params=_params("arbitrary"),
        name="moe_experts",
    )(blk_e, nused, xs, wgu, bgu, wdn, bdn)


def _gather_topk_rows(dest_ref, yb_hbm, dst, sem, tm):
    def body(c, carry):
        for u in range(ISSUE_UNROLL):
            t = c * ISSUE_UNROLL + u
            for kk in range(TOP_K):
                pltpu.make_async_copy(_row_slice(yb_hbm, dest_ref[0, 0, t * TOP_K + kk]),
                                      _row_slice(dst, kk * tm + t), sem).start(priority=kk % 2)
        return carry
    lax.fori_loop(0, tm // ISSUE_UNROLL, body, 0)


def _combine_kernel(dest_first_ref, dest_next_ref, yb_hbm, x1_ref, route_ref, g_ref, b_ref, x2_ref, x2b_ref,
                    buf, sem, *, tm):
    i = pl.program_id(0)
    n = pl.num_programs(0)
    slot = i % 2

    @pl.when(i == 0)
    def _():
        _gather_topk_rows(dest_first_ref, yb_hbm, buf.at[0], sem.at[0], tm)

    @pl.when(i + 1 < n)
    def _():
        _gather_topk_rows(dest_next_ref, yb_hbm, buf.at[1 - slot], sem.at[1 - slot], tm)

    pltpu.make_async_copy(yb_hbm.at[pl.ds(0, tm * TOP_K * ROW_TILES)], buf.at[slot], sem.at[slot]).wait()
    route = route_ref[...]
    ffn = jnp.zeros((tm, D_MODEL), F32)
    for kk in range(TOP_K):
        rows = _from_row_tiles(buf.at[slot], kk * tm * ROW_TILES, tm, ROW_TILES)
        ffn = ffn + rows * route[:, 2 * TOP_K + kk:2 * TOP_K + kk + 1]
    x2 = _layernorm(DN_ALPHA * x1_ref[...] + ffn, g_ref[...], b_ref[...])
    x2_ref[...] = x2
    x2b_ref[...] = x2.astype(BF16)


def _combine(dest, yb, x1, route, g, b, tm=512):
    t = x1.shape[0]
    tm = min(tm, t)
    nt = t // tm
    dest3 = dest.reshape(nt, 1, tm * TOP_K)
    return pl.pallas_call(
        functools.partial(_combine_kernel, tm=tm),
        grid=(nt,),
        in_specs=[pl.BlockSpec((1, 1, tm * TOP_K), lambda i: (0, 0, 0), memory_space=pltpu.SMEM),
                  pl.BlockSpec((1, 1, tm * TOP_K), lambda i: (jnp.minimum(i + 1, nt - 1), 0, 0),
                               memory_space=pltpu.SMEM),
                  pl.BlockSpec(memory_space=pl.ANY),
                  pl.BlockSpec((tm, D_MODEL), lambda i: (i, 0)),
                  pl.BlockSpec((tm, LANES), lambda i: (i, 0)),
                  pl.BlockSpec((1, D_MODEL), lambda i: (0, 0)),
                  pl.BlockSpec((1, D_MODEL), lambda i: (0, 0))],
        out_specs=[pl.BlockSpec((tm, D_MODEL), lambda i: (i, 0)),
                   pl.BlockSpec((tm, D_MODEL), lambda i: (i, 0))],
        out_shape=[jax.ShapeDtypeStruct((t, D_MODEL), F32), jax.ShapeDtypeStruct((t, D_MODEL), BF16)],
        scratch_shapes=[pltpu.VMEM((2, tm * TOP_K * ROW_TILES, LANES), F32),
                        pltpu.SemaphoreType.DMA((2,))],
        compiler_params=_params("arbitrary"),
        name="combine",
    )(dest3, dest3, yb, x1, route, g, b)


def _pad_cols(w, width):
    return jnp.pad(w, ((0, 0), (0, width - w.shape[1])))


def _pack_in_proj(w_in, b_in):
    offs = np.concatenate([[0], np.cumsum(IN_SIZES)])
    wb = jnp.concatenate([w_in, b_in[None, :]], 0)

    def piece(i):
        return wb[:, offs[i]:offs[i + 1]]

    kr = piece(2)
    kr_swapped = jnp.concatenate([kr[:, MLA_DR // 2:], kr[:, :MLA_DR // 2]], 1)
    dt = piece(5)
    grp_a = jnp.concatenate([piece(0), piece(1), _pad_cols(jnp.concatenate([kr, kr_swapped], 1), LANES),
                             _pad_cols(dt[:, :SSM_HEADS], LANES), _pad_cols(dt[:, SSM_HEADS:], LANES)], 1)
    groups = {"a": (grp_a, F32), "z": (piece(3), BF16), "xbc": (piece(4), F32),
              "cnv": (jnp.concatenate([piece(6), piece(7)], 1), F32), "gate": (piece(8), BF16)}
    return {k: (v[:-1].astype(BF16), v[-1:], dt_) for k, (v, dt_) in groups.items()}


def _pack_mla(w_uq, w_ukv, w_br_attn, seq):
    hq = MLA_DN + MLA_DR
    half = MLA_DR // 2
    wq = w_uq.reshape(MLA_Q_LORA, MLA_HEADS, hq)
    zq = jnp.zeros((MLA_Q_LORA, MLA_HEADS, HEAD_PAD - hq), F32)
    wq_main = jnp.concatenate([wq, zq], -1).reshape(MLA_Q_LORA, MLA_W)
    wq_swap = jnp.concatenate([jnp.zeros((MLA_Q_LORA, MLA_HEADS, MLA_DN), F32),
                               wq[..., MLA_DN + half:], wq[..., MLA_DN:MLA_DN + half], zq],
                              -1).reshape(MLA_Q_LORA, MLA_W)
    wkv = w_ukv.reshape(MLA_KV_LORA, MLA_HEADS, MLA_DN + MLA_DV)
    zk = jnp.zeros((MLA_KV_LORA, MLA_HEADS, HEAD_PAD - MLA_DN), F32)
    wk = jnp.concatenate([wkv[..., :MLA_DN], zk], -1).reshape(MLA_KV_LORA, MLA_W)
    wv = jnp.concatenate([wkv[..., MLA_DN:], zk], -1).reshape(MLA_KV_LORA, MLA_W)
    e2 = np.zeros((LANES, MLA_HEADS, HEAD_PAD), np.float32)
    for j in range(MLA_DR):
        e2[j, :, MLA_DN + j] = 1.0
        e2[MLA_DR + j, :, MLA_DN + j] = 1.0
    vone = np.zeros((MLA_HEADS, HEAD_PAD), np.float32)
    vone[:, MLA_DV] = 1.0
    wbr = jnp.concatenate([w_br_attn.reshape(MLA_HEADS, MLA_DV, D_MODEL),
                           jnp.zeros((MLA_HEADS, HEAD_PAD - MLA_DV, D_MODEL), F32)], 1).reshape(MLA_W, D_MODEL)
    pos = jnp.arange(seq, dtype=F32)
    inv = ROPE_THETA ** (-jnp.arange(0, MLA_DR, 2, dtype=F32) / MLA_DR)
    ang = pos[:, None] * inv[None, :]
    cos, sin = jnp.cos(ang), jnp.sin(ang)
    scale = (MLA_DN + MLA_DR) ** -0.5
    ones = jnp.ones((seq, MLA_DN), F32)
    zpad = jnp.zeros((seq, HEAD_PAD - hq), F32)
    cq = jnp.tile(jnp.concatenate([ones, cos, cos, zpad], 1) * scale, (1, MLA_HEADS))
    sq = jnp.tile(jnp.concatenate([0 * ones, -sin, sin, zpad], 1) * scale, (1, MLA_HEADS))
    tk = jnp.concatenate([cos, cos, -sin, sin, jnp.zeros((seq, LANES - 2 * MLA_DR), F32)], 1)
    return dict(wq=wq_main.astype(BF16), wqs=wq_swap.astype(BF16), wk=wk.astype(BF16), wv=wv.astype(BF16),
                e2=jnp.asarray(e2.reshape(LANES, MLA_W), BF16), vone=jnp.asarray(vone.reshape(1, MLA_W)),
                wbr=wbr.astype(BF16), cq=cq, sq=sq, tk=tk)


def _route_tables(route, cnt, n_tok):
    idx = route[:, :TOP_K].astype(jnp.int32)
    rank = route[:, TOP_K:2 * TOP_K].astype(jnp.int32)
    counts = cnt[0, :N_EXPERTS].astype(jnp.int32)
    n_blocks = -(-(n_tok * TOP_K + N_EXPERTS * (MOE_ROWS - 1)) // MOE_ROWS)
    padded = (counts + MOE_ROWS - 1) // MOE_ROWS * MOE_ROWS
    pad_end = jnp.cumsum(padded)
    pad_start = pad_end - padded
    onehot = idx[..., None] == jnp.arange(N_EXPERTS, dtype=jnp.int32)
    dest = (jnp.sum(jnp.where(onehot, pad_start, 0), -1) + rank).reshape(-1)
    blk_first = jnp.arange(n_blocks, dtype=jnp.int32) * MOE_ROWS
    blk_e = jnp.minimum(jnp.sum((pad_end[None, :] <= blk_first[:, None]).astype(jnp.int32), -1), N_EXPERTS - 1)
    nused = pad_end[-1:] // MOE_ROWS
    return dest, blk_e, nused, n_blocks * MOE_ROWS


def kernel(x, w_in, b_in, mla_q_norm, mla_kv_norm, mla_w_uq, mla_w_ukv, w_br_attn, ssm_conv_w, ssm_conv_b, ssm_dt_bias, ssm_a_log, ssm_d, ssm_norm, w_br_ssm, cnv_dw_w, cnv_dw_b, cnv_ln_g, cnv_ln_b, w_br_conv, b_br_conv, w_out, ln1_g, ln1_b, router_w, router_b, moe_w_gate_up, moe_b_gate_up, moe_w_down, moe_b_down, ln2_g, ln2_b):
    bsz, seq, d = x.shape
    n_tok = bsz * seq
    xf = x.reshape(n_tok, d)
    xb = xf.astype(BF16)
    xs = None
    for l in range(DEPTH):
        proj = _pack_in_proj(w_in[l], b_in[l])
        mla = _pack_mla(mla_w_uq[l], mla_w_ukv[l], w_br_attn[l], seq)
        a_grp = _linear(xb, *proj["a"])
        z = _linear(xb, *proj["z"])
        xbc_raw = _linear(xb, *proj["xbc"], tn=768)
        cnv_raw = _linear(xb, *proj["cnv"])
        gate_logits = _linear(xb, *proj["gate"], tn=1024)

        q, k, v = _mla_prep(a_grp, seq, mla_q_norm[l][None], mla_kv_norm[l][None], mla["wq"], mla["wqs"],
                            mla["wk"], mla["wv"], mla["e2"], mla["cq"], mla["sq"], mla["tk"], mla["vone"])
        attn = _attention(q.reshape(bsz, seq, MLA_W), k.reshape(bsz, seq, MLA_W), v.reshape(bsz, seq, MLA_W))

        xbc = _dwconv(xbc_raw.reshape(bsz, seq, SSM_CONV_DIM), ssm_conv_w[l], ssm_conv_b[l],
                      glu=False, silu_out=True)
        a3 = a_grp.reshape(bsz, seq, GROUP_A)
        dtb = jnp.pad(ssm_dt_bias[l], ((0, 0), (0, LANES - SSM_HEADS)))
        nega = jnp.pad(-jnp.exp(ssm_a_log[l]), ((0, 0), (0, LANES - SSM_HEADS)))
        y_fwd, y_bwd = _ssd(xbc, a3, dtb, nega)

        u = _dwconv(cnv_raw.reshape(bsz, seq, 2 * CNV_CH), cnv_dw_w[l], cnv_dw_b[l], glu=True, silu_out=False)

        rw = jnp.pad(router_w[l], ((0, 0), (0, LANES - N_EXPERTS)))
        rw_hi = rw.astype(BF16)
        rw_lo = (rw - rw_hi.astype(F32)).astype(BF16)
        rw = jnp.concatenate([rw_hi, rw_hi, rw_lo], 0)
        rb = jnp.pad(router_b[l], (0, LANES - N_EXPERTS), constant_values=NEG_BIG)[None]
        x1, x1g, route, cnt = _merge(
            xf, attn.reshape(n_tok, MLA_W), y_fwd.reshape(n_tok, SSM_D_INNER), y_bwd.reshape(n_tok, SSM_D_INNER),
            xbc.reshape(n_tok, SSM_CONV_DIM), z, u.reshape(n_tok, CNV_CH), gate_logits,
            mla["wbr"], w_br_ssm[l].astype(BF16), w_br_conv[l].astype(BF16), w_out[l].astype(BF16),
            jnp.repeat(ssm_d[l], SSM_HEAD_DIM)[None], ssm_norm[l][None], cnv_ln_g[l][None], cnv_ln_b[l][None],
            b_br_conv[l][None], ln1_g[l][None], ln1_b[l][None], rw, rb)

        dest, blk_e, nused, n_rows = _route_tables(route, cnt, n_tok)
        xs = _dispatch(dest, x1g, n_rows, init=xs)
        yb = _moe_experts(blk_e, nused, xs, moe_w_gate_up, moe_b_gate_up[l][:, None, :], moe_w_down,
                          moe_b_down[l][:, None, :], l)
        xf, xb = _combine(dest, yb, x1, route, ln2_g[l][None], ln2_b[l][None])
    return xf.reshape(bsz, seq, d)
```

```python
import functools
import math

import numpy as np
import jax
import jax.numpy as jnp
from jax import lax
from jax.experimental import pallas as pl
from jax.experimental.pallas import tpu as pltpu

F32 = jnp.float32
BF16 = jnp.bfloat16
HIGHEST = lax.Precision.HIGHEST

LANES = 128
SUBLANES = 8
VMEM_LIMIT_BYTES = 56 * 1024 * 1024

D_MODEL = 1024
DEPTH = 2
MLA_HEADS = 8
MLA_Q_LORA = 384
MLA_KV_LORA = 256
MLA_DN = 64
MLA_DR = 32
MLA_DV = 64
ROPE_THETA = 10000.0
SSM_HEADS = 16
SSM_HEAD_DIM = 64
SSM_D_INNER = SSM_HEADS * SSM_HEAD_DIM
SSM_GROUPS = 4
SSM_STATE = 64
SSM_CONV = 5
SSM_CHUNK = 128
SSM_BC = SSM_GROUPS * SSM_STATE
SSM_CONV_DIM = SSM_D_INNER + 2 * SSM_BC
CNV_CH = 512
CNV_WIDTH = 31
N_EXPERTS = 32
TOP_K = 4
D_FF = 1024
SWIGLU_LIMIT = 7.0
SWIGLU_ALPHA = 1.702
DN_ALPHA = (2 * DEPTH) ** 0.25
IN_SIZES = (MLA_Q_LORA, MLA_KV_LORA, MLA_DR, SSM_D_INNER, SSM_CONV_DIM, 2 * SSM_HEADS,
            CNV_CH, CNV_CH, 3 * D_MODEL)
HEAD_PAD = LANES
MLA_W = MLA_HEADS * HEAD_PAD
GROUP_A = 1024
CONV_HALO = 16
MOE_ROWS = 512
ISSUE_UNROLL = 8
ROW_TILES = D_MODEL // LANES
NEG_BIG = -1e30


def _params(*sem):
    return pltpu.CompilerParams(dimension_semantics=sem, vmem_limit_bytes=VMEM_LIMIT_BYTES)


def _linear_kernel(x_ref, w_ref, b_ref, o_ref):
    acc = jnp.dot(x_ref[...], w_ref[...], preferred_element_type=F32)
    o_ref[...] = (acc + b_ref[...]).astype(o_ref.dtype)


def _linear(x, w, b, out_dtype, tm=1024, tn=None):
    m, k = x.shape
    n = w.shape[1]
    tm = min(tm, m)
    tn = n if tn is None else tn
    return pl.pallas_call(
        _linear_kernel,
        grid=(n // tn, m // tm),
        in_specs=[pl.BlockSpec((tm, k), lambda j, i: (i, 0)),
                  pl.BlockSpec((k, tn), lambda j, i: (0, j)),
                  pl.BlockSpec((1, tn), lambda j, i: (0, j))],
        out_specs=pl.BlockSpec((tm, tn), lambda j, i: (i, j)),
        out_shape=jax.ShapeDtypeStruct((m, n), out_dtype),
        compiler_params=_params("parallel", "parallel"),
        name="linear",
    )(x, w, b)


def _rms(x, g, eps=1e-6):
    return x * lax.rsqrt(jnp.mean(x * x, -1, keepdims=True) + eps) * g


def _mla_prep_kernel(a_ref, gq_ref, gkv_ref, wq_ref, wqs_ref, wk_ref, wv_ref, e2_ref,
                     cq_ref, sq_ref, tk_ref, vone_ref, q_ref, k_ref, v_ref):
    a = a_ref[...]
    c_q = a[:, :MLA_Q_LORA]
    c_kv = a[:, MLA_Q_LORA:MLA_Q_LORA + MLA_KV_LORA]
    kr = a[:, MLA_Q_LORA + MLA_KV_LORA:]
    qn = _rms(c_q, gq_ref[...]).astype(BF16)
    kvn = _rms(c_kv, gkv_ref[...]).astype(BF16)
    q = (jnp.dot(qn, wq_ref[...], preferred_element_type=F32) * cq_ref[...]
         + jnp.dot(qn, wqs_ref[...], preferred_element_type=F32) * sq_ref[...])
    q_ref[...] = q.astype(BF16)
    krp = (kr * tk_ref[...]).astype(BF16)
    k = (jnp.dot(kvn, wk_ref[...], preferred_element_type=F32)
         + jnp.dot(krp, e2_ref[...], preferred_element_type=F32))
    k_ref[...] = k.astype(BF16)
    v = jnp.dot(kvn, wv_ref[...], preferred_element_type=F32) + vone_ref[...]
    v_ref[...] = v.astype(BF16)


def _mla_prep(a, seq, gq, gkv, wq, wqs, wk, wv, e2, cq, sq, tk, vone, tm=512):
    t = a.shape[0]
    tm = min(tm, seq)
    nper = seq // tm
    wa = MLA_Q_LORA + MLA_KV_LORA + LANES

    def const(shape):
        return pl.BlockSpec(shape, lambda i: (0, 0))

    def tab(width):
        return pl.BlockSpec((tm, width), lambda i: (i % nper, 0))

    out = jax.ShapeDtypeStruct((t, MLA_W), BF16)
    return pl.pallas_call(
        _mla_prep_kernel,
        grid=(t // tm,),
        in_specs=[pl.BlockSpec((tm, wa), lambda i: (i, 0)),
                  const((1, MLA_Q_LORA)), const((1, MLA_KV_LORA)),
                  const((MLA_Q_LORA, MLA_W)), const((MLA_Q_LORA, MLA_W)),
                  const((MLA_KV_LORA, MLA_W)), const((MLA_KV_LORA, MLA_W)),
                  const((LANES, MLA_W)),
                  tab(MLA_W), tab(MLA_W), tab(LANES), const((1, MLA_W))],
        out_specs=[pl.BlockSpec((tm, MLA_W), lambda i: (i, 0))] * 3,
        out_shape=[out, out, out],
        compiler_params=_params("parallel"),
        name="mla_prep",
    )(a, gq, gkv, wq, wqs, wk, wv, e2, cq, sq, tk, vone)


def _attn_kernel(q_ref, k_ref, v_ref, o_ref, *, sub):
    for r in range(q_ref.shape[1] // sub):
        rows = slice(r * sub, (r + 1) * sub)
        s = lax.dot_general(q_ref[0, rows], k_ref[0], (((1,), (1,)), ((), ())),
                            preferred_element_type=F32)
        m = jnp.max(s, -1, keepdims=True)
        p = jnp.exp(s - m).astype(BF16)
        o = jnp.dot(p, v_ref[0], preferred_element_type=F32)
        o_ref[0, rows] = (o / o[:, MLA_DV:MLA_DV + 1]).astype(o_ref.dtype)


def _attention(q, k, v, tq=1024, sub=256):
    b, s, _ = q.shape
    tq = min(tq, s)
    return pl.pallas_call(
        functools.partial(_attn_kernel, sub=min(sub, tq)),
        grid=(b, MLA_HEADS, s // tq),
        in_specs=[pl.BlockSpec((1, tq, HEAD_PAD), lambda bi, h, i: (bi, i, h)),
                  pl.BlockSpec((1, s, HEAD_PAD), lambda bi, h, i: (bi, 0, h)),
                  pl.BlockSpec((1, s, HEAD_PAD), lambda bi, h, i: (bi, 0, h))],
        out_specs=pl.BlockSpec((1, tq, HEAD_PAD), lambda bi, h, i: (bi, i, h)),
        out_shape=jax.ShapeDtypeStruct((b, s, MLA_W), BF16),
        compiler_params=_params("parallel", "parallel", "parallel"),
        name="attention",
    )(q, k, v)


def _dwconv_kernel(*refs, width, glu, silu_out, seq, rows):
    if glu:
        a_ref, g_ref, w_ref, b_ref, o_ref, pad_ref = refs
        pre = a_ref[0] * jax.nn.sigmoid(g_ref[0])
    else:
        x_ref, w_ref, b_ref, o_ref, pad_ref = refs
        pre = x_ref[0]
    ch = o_ref.shape[-1]
    halo = jnp.zeros((CONV_HALO, ch), F32)
    pad_ref[0:CONV_HALO, :] = halo
    pad_ref[CONV_HALO + seq:2 * CONV_HALO + seq, :] = halo
    pad_ref[CONV_HALO:CONV_HALO + seq, :] = pre
    half = (width - 1) // 2
    win_rows = rows + 2 * CONV_HALO

    def body(c, carry):
        base = pl.multiple_of(c * rows, rows)
        acc = jnp.zeros((rows, ch), F32) + b_ref[...]
        for t in range(width):
            acc = acc + pad_ref[pl.ds(base + (CONV_HALO - half + t), rows), :] * w_ref[t:t + 1, :]
        if silu_out:
            acc = acc * jax.nn.sigmoid(acc)
        o_ref[0, pl.ds(base, rows), :] = acc
        return carry

    lax.fori_loop(0, seq // rows, body, 0)


def _dwconv(x, w, b, *, glu, silu_out, rows=64):
    bsz, seq, cin = x.shape
    width, ch = w.shape
    nct = ch // LANES
    kern = functools.partial(_dwconv_kernel, width=width, glu=glu, silu_out=silu_out, seq=seq,
                             rows=min(rows, seq))
    xspec = pl.BlockSpec((1, seq, LANES), lambda bi, j: (bi, 0, j))
    in_specs = [xspec]
    args = [x]
    if glu:
        in_specs.append(pl.BlockSpec((1, seq, LANES), lambda bi, j: (bi, 0, j + nct)))
        args.append(x)
    in_specs += [pl.BlockSpec((width, LANES), lambda bi, j: (0, j)),
                 pl.BlockSpec((1, LANES), lambda bi, j: (0, j))]
    return pl.pallas_call(
        kern,
        grid=(bsz, nct),
        in_specs=in_specs,
        out_specs=pl.BlockSpec((1, seq, LANES), lambda bi, j: (bi, 0, j)),
        out_shape=jax.ShapeDtypeStruct((bsz, seq, ch), F32),
        scratch_shapes=[pltpu.VMEM((seq + 2 * CONV_HALO, LANES), F32)],
        compiler_params=_params("parallel", "parallel"),
        name="dwconv",
    )(*args, w, b.reshape(1, ch))


def _softplus(x):
    return jnp.maximum(x, 0.0) + jnp.log1p(jnp.exp(-jnp.abs(x)))


def _ssd_kernel(*refs):
    f_in, b_in = refs[0:6], refs[6:12]
    yf_ref, yb_ref, hf_ref, hb_ref = refs[12:16]
    _ssd_chunk(*f_in, yf_ref, hf_ref, reverse=False)
    _ssd_chunk(*b_in, yb_ref, hb_ref, reverse=True)


def _ssd_chunk(xs_ref, bm_ref, cm_ref, dt_ref, dtb_ref, nega_ref, y_ref, h_ref, *, reverse):
    ln = SSM_CHUNK

    @pl.when(pl.program_id(1) == 0)
    def _():
        h_ref[...] = jnp.zeros_like(h_ref)

    dt = _softplus(dt_ref[0] + dtb_ref[...])
    a = dt * nega_ref[...]
    row = lax.broadcasted_iota(jnp.int32, (ln, ln), 0)
    col = lax.broadcasted_iota(jnp.int32, (ln, ln), 1)
    tri = (col >= row) if reverse else (col <= row)
    a_hi = a.astype(BF16)
    rem = a - a_hi.astype(F32)
    a_mid = rem.astype(BF16)
    a_lo = (rem - a_mid.astype(F32)).astype(BF16)
    tri_b = jnp.where(tri, 1.0, 0.0).astype(BF16)
    acs = jnp.dot(jnp.concatenate([tri_b, tri_b, tri_b], 1), jnp.concatenate([a_hi, a_mid, a_lo], 0),
                  preferred_element_type=F32)
    tot = acs[0:1] if reverse else acs[ln - 1:ln]
    wst = dt * jnp.exp(tot - acs)
    etot = jnp.exp(tot)
    acs_t = acs.T
    dt_t = dt.T
    wst_t = wst.T

    xs = xs_ref[0]
    even = lax.broadcasted_iota(jnp.int32, xs.shape, 1) % LANES < SSM_HEAD_DIM
    xs_even = jnp.where(even, xs, 0.0).astype(BF16)
    xs_odd = jnp.where(even, 0.0, xs).astype(BF16)
    cm_b = cm_ref[0].astype(BF16)
    bm_t = bm_ref[0].T
    bm_tb = bm_t.astype(BF16)
    y_off = jnp.dot(cm_b, h_ref[...].astype(BF16), preferred_element_type=F32)

    glane = lax.broadcasted_iota(jnp.int32, (ln, SSM_BC), 1) // SSM_STATE
    lo = lax.broadcasted_iota(jnp.int32, (ln, LANES), 1) < SSM_HEAD_DIM
    lo_1 = lax.broadcasted_iota(jnp.int32, (1, LANES), 1) < SSM_HEAD_DIM
    heads_per_group = SSM_HEADS // SSM_GROUPS
    for g in range(SSM_GROUPS):
        cmg = jnp.where(glane == g, cm_b, jnp.zeros_like(cm_b))
        cb = jnp.dot(cmg, bm_tb, preferred_element_type=F32)
        bm_tg = bm_t[g * SSM_STATE:(g + 1) * SSM_STATE, :]
        for j in range(heads_per_group // 2):
            pair = g * (heads_per_group // 2) + j
            lanes = slice(pair * LANES, (pair + 1) * LANES)
            x2 = jnp.concatenate([xs_even[:, lanes], xs_odd[:, lanes]], 0)
            mats, ecol, wsts, et = [], [], [], []
            for hh in range(2):
                h = 2 * pair + hh
                colb = jnp.broadcast_to(acs[:, h:h + 1], (ln, LANES))
                decay = jnp.exp(jnp.where(tri, colb - acs_t[h:h + 1, :], -jnp.inf))
                mats.append((cb * decay * dt_t[h:h + 1, :]).astype(BF16))
                ecol.append(jnp.exp(colb))
                wsts.append((bm_tg * wst_t[h:h + 1, :]).astype(BF16))
                et.append(jnp.broadcast_to(etot[:, h:h + 1], (1, LANES)))
            diag = jnp.dot(jnp.concatenate(mats, 1), x2, preferred_element_type=F32)
            st = jnp.dot(jnp.concatenate(wsts, 1), x2, preferred_element_type=F32)
            y_ref[0, :, lanes] = diag + y_off[:, lanes] * jnp.where(lo, ecol[0], ecol[1])
            rows = slice(g * SSM_STATE, (g + 1) * SSM_STATE)
            h_ref[rows, lanes] = h_ref[rows, lanes] * jnp.where(lo_1, et[0], et[1]) + st


def _ssd(xbc, a_grp, dt_bias, neg_a):
    bsz, seq, _ = xbc.shape
    nc = seq // SSM_CHUNK

    def direction(reverse):
        dt_tile = 7 if reverse else 6
        row = 1 if reverse else 0

        def cidx(c):
            return (nc - 1 - c) if reverse else c

        ins = [pl.BlockSpec((1, SSM_CHUNK, SSM_D_INNER), lambda b, c: (b, cidx(c), 0)),
               pl.BlockSpec((1, SSM_CHUNK, SSM_BC), lambda b, c: (b, cidx(c), SSM_D_INNER // SSM_BC)),
               pl.BlockSpec((1, SSM_CHUNK, SSM_BC), lambda b, c: (b, cidx(c), SSM_D_INNER // SSM_BC + 1)),
               pl.BlockSpec((1, SSM_CHUNK, LANES), lambda b, c: (b, cidx(c), dt_tile)),
               pl.BlockSpec((1, LANES), lambda b, c: (0, 0)),
               pl.BlockSpec((1, LANES), lambda b, c: (0, 0))]
        out = pl.BlockSpec((1, SSM_CHUNK, SSM_D_INNER), lambda b, c: (b, cidx(c), 0))
        return ins, out, (xbc, xbc, xbc, a_grp, dt_bias[row:row + 1], neg_a[row:row + 1])

    f_ins, f_out, f_args = direction(False)
    b_ins, b_out, b_args = direction(True)
    y = jax.ShapeDtypeStruct((bsz, seq, SSM_D_INNER), F32)
    return pl.pallas_call(
        _ssd_kernel,
        grid=(bsz, nc),
        in_specs=f_ins + b_ins,
        out_specs=[f_out, b_out],
        out_shape=[y, y],
        scratch_shapes=[pltpu.VMEM((SSM_BC, SSM_D_INNER), F32), pltpu.VMEM((SSM_BC, SSM_D_INNER), F32)],
        compiler_params=_params("parallel", "arbitrary"),
        name="ssd",
    )(*f_args, *b_args)


def _layernorm(x, g, b, eps=1e-5):
    mu = jnp.mean(x, -1, keepdims=True)
    xc = x - mu
    var = jnp.mean(xc * xc, -1, keepdims=True)
    return xc * lax.rsqrt(var + eps) * g + b


def _merge_kernel(x_ref, o_ref, yf_ref, yb_ref, xs_ref, z_ref, u_ref, gl_ref,
                  wa_ref, ws_ref, wc_ref, wo_ref, dsk_ref, ng_ref, cg_ref, cb_ref, bc_ref,
                  l1g_ref, l1b_ref, rw_ref, rb_ref,
                  x1_ref, x1g_ref, route_ref, cnt_ref, carry_ref):
    tm = x_ref.shape[0]

    @pl.when(pl.program_id(0) == 0)
    def _():
        carry_ref[...] = jnp.zeros_like(carry_ref)

    y_attn = jnp.dot(o_ref[...], wa_ref[...], preferred_element_type=F32)
    z = z_ref[...].astype(F32)
    ys = (yf_ref[...] + yb_ref[...] + xs_ref[...] * dsk_ref[...]) * (z * jax.nn.sigmoid(z))
    gw = SSM_D_INNER // SSM_GROUPS
    ys = jnp.concatenate(
        [_rms(ys[:, g * gw:(g + 1) * gw], ng_ref[:, g * gw:(g + 1) * gw]) for g in range(SSM_GROUPS)], -1)
    y_ssm = jnp.dot(ys.astype(BF16), ws_ref[...], preferred_element_type=F32)
    uc = _layernorm(u_ref[...], cg_ref[...], cb_ref[...])
    uc = uc * jax.nn.sigmoid(uc)
    y_conv = jnp.dot(uc.astype(BF16), wc_ref[...], preferred_element_type=F32) + bc_ref[...]
    gl = gl_ref[...].astype(F32)
    mixed = (jax.nn.sigmoid(gl[:, :D_MODEL]) * y_attn
             + jax.nn.sigmoid(gl[:, D_MODEL:2 * D_MODEL]) * y_ssm
             + jax.nn.sigmoid(gl[:, 2 * D_MODEL:]) * y_conv)
    mixed = jnp.dot(mixed.astype(BF16), wo_ref[...], preferred_element_type=F32)
    x1 = _layernorm(DN_ALPHA * x_ref[...] + mixed, l1g_ref[...], l1b_ref[...])
    x1_ref[...] = x1
    for t in range(ROW_TILES):
        x1g_ref[pl.ds(t, tm, stride=ROW_TILES), :] = x1[:, t * LANES:(t + 1) * LANES]

    x_hi = x1.astype(BF16)
    x_lo = (x1 - x_hi.astype(F32)).astype(BF16)
    lg = jnp.dot(jnp.concatenate([x_hi, x_lo, x_hi], -1), rw_ref[...], preferred_element_type=F32) + rb_ref[...]
    lane = lax.broadcasted_iota(jnp.int32, (tm, LANES), 1).astype(F32)
    sels, vals, idxs = [], [], []
    for _ in range(TOP_K):
        m = jnp.max(lg, -1, keepdims=True)
        idx = jnp.min(jnp.where(lg == m, lane, float(LANES)), -1, keepdims=True)
        sel = lane == idx
        lg = jnp.where(sel, NEG_BIG * 2, lg)
        sels.append(sel)
        vals.append(m)
        idxs.append(idx)
    es = [jnp.exp(v - vals[0]) for v in vals]
    den = es[0] + es[1] + es[2] + es[3]
    hot = jnp.zeros((tm, LANES), F32)
    for sel in sels:
        hot = hot + sel.astype(F32)
    r = lax.broadcasted_iota(jnp.int32, (tm, tm), 0)
    c = lax.broadcasted_iota(jnp.int32, (tm, tm), 1)
    excl = jnp.dot((c < r).astype(BF16), hot.astype(BF16), preferred_element_type=F32) + carry_ref[...]
    route = jnp.zeros((tm, LANES), F32)
    for kk in range(TOP_K):
        rank = jnp.sum(jnp.where(sels[kk], excl, 0.0), -1, keepdims=True)
        route = jnp.where(lane == float(kk), idxs[kk], route)
        route = jnp.where(lane == float(TOP_K + kk), rank, route)
        route = jnp.where(lane == float(2 * TOP_K + kk), es[kk] / den, route)
    route_ref[...] = route
    carry_ref[...] = carry_ref[...] + jnp.sum(hot, 0, keepdims=True)
    cnt_ref[...] = carry_ref[...]


def _merge(x, o, yf, yb, xbc, z, u, gl, wa, ws, wc, wo, dsk, ng, cg, cb, bc, l1g, l1b, rw, rb, tm=256):
    t = x.shape[0]
    tm = min(tm, t)

    def rowb(width, col=0):
        return pl.BlockSpec((tm, width), lambda i: (i, col))

    def const(shape):
        return pl.BlockSpec(shape, lambda i: (0, 0))

    return pl.pallas_call(
        _merge_kernel,
        grid=(t // tm,),
        in_specs=[rowb(D_MODEL), rowb(MLA_W), rowb(SSM_D_INNER), rowb(SSM_D_INNER), rowb(SSM_D_INNER),
                  rowb(SSM_D_INNER), rowb(CNV_CH), rowb(3 * D_MODEL),
                  const((MLA_W, D_MODEL)), const((SSM_D_INNER, D_MODEL)), const((CNV_CH, D_MODEL)),
                  const((D_MODEL, D_MODEL)), const((1, SSM_D_INNER)), const((1, SSM_D_INNER)),
                  const((1, CNV_CH)), const((1, CNV_CH)), const((1, D_MODEL)),
                  const((1, D_MODEL)), const((1, D_MODEL)), const((3 * D_MODEL, LANES)), const((1, LANES))],
        out_specs=[rowb(D_MODEL),
                   pl.BlockSpec((tm * ROW_TILES, LANES), lambda i: (i, 0)),
                   rowb(LANES), const((1, LANES))],
        out_shape=[jax.ShapeDtypeStruct((t, D_MODEL), F32),
                   jax.ShapeDtypeStruct((t * ROW_TILES, LANES), F32),
                   jax.ShapeDtypeStruct((t, LANES), F32), jax.ShapeDtypeStruct((1, LANES), F32)],
        scratch_shapes=[pltpu.VMEM((1, LANES), F32)],
        compiler_params=_params("arbitrary"),
        name="merge",
    )(x, o, yf, yb, xbc, z, u, gl, wa, ws, wc, wo, dsk, ng, cg, cb, bc, l1g, l1b, rw, rb)


def _row_slice(ref, row):
    return ref.at[pl.ds(pl.multiple_of(row * ROW_TILES, ROW_TILES), ROW_TILES)]


def _from_row_tiles(ref, first, rows, pitch):
    return jnp.concatenate([ref[pl.ds(first + t, rows, stride=pitch), :] for t in range(ROW_TILES)], -1)


def _dispatch_kernel(dest_ref, x_ref, init_hbm, xs_hbm, sem, *, tm):
    del init_hbm

    def body(c, carry):
        for u in range(ISSUE_UNROLL):
            t = c * ISSUE_UNROLL + u
            src = _row_slice(x_ref, t)
            for kk in range(TOP_K):
                pltpu.make_async_copy(src, _row_slice(xs_hbm, dest_ref[0, 0, t * TOP_K + kk]),
                                      sem).start(priority=kk % 2)
        return carry

    lax.fori_loop(0, tm // ISSUE_UNROLL, body, 0)
    for _ in range(TOP_K):
        pltpu.make_async_copy(x_ref, xs_hbm.at[pl.ds(0, tm * ROW_TILES)], sem).wait()


def _dispatch(dest, x1g, n_rows, init=None, tm=512):
    t = x1g.shape[0] // ROW_TILES
    tm = min(tm, t)
    nt = t // tm
    zeros = jnp.zeros((n_rows * ROW_TILES, LANES), F32) if init is None else init
    return pl.pallas_call(
        functools.partial(_dispatch_kernel, tm=tm),
        grid=(nt,),
        in_specs=[pl.BlockSpec((1, 1, tm * TOP_K), lambda i: (i, 0, 0), memory_space=pltpu.SMEM),
                  pl.BlockSpec((tm * ROW_TILES, LANES), lambda i: (i, 0)),
                  pl.BlockSpec(memory_space=pl.ANY)],
        out_specs=pl.BlockSpec(memory_space=pl.ANY),
        out_shape=jax.ShapeDtypeStruct((n_rows * ROW_TILES, LANES), F32),
        scratch_shapes=[pltpu.SemaphoreType.DMA(())],
        input_output_aliases={2: 0},
        compiler_params=_params("arbitrary"),
        name="dispatch",
    )(dest.reshape(nt, 1, tm * TOP_K), x1g, zeros)


def _moe_kernel(blk_e_ref, nused_ref, x_ref, wgu_ref, bgu_ref, wdn_ref, bdn_ref, o_ref, wgu_b, wdn_b):
    i = pl.program_id(0)

    @pl.when((i == 0) | (blk_e_ref[i] != blk_e_ref[jnp.maximum(i - 1, 0)]))
    def _():
        wgu_b[...] = wgu_ref[0, 0].astype(BF16)
        wdn_b[...] = wdn_ref[0, 0].astype(BF16)

    @pl.when(i < nused_ref[0])
    def _():
        xb = _from_row_tiles(x_ref, 0, MOE_ROWS, ROW_TILES).astype(BF16)
        h = jnp.dot(xb, wgu_b[...], preferred_element_type=F32) + bgu_ref[0]
        gate = jnp.minimum(h[:, :D_FF], SWIGLU_LIMIT)
        up = jnp.clip(h[:, D_FF:], -SWIGLU_LIMIT, SWIGLU_LIMIT)
        act = (up + 1.0) * (gate * jax.nn.sigmoid(SWIGLU_ALPHA * gate))
        y = jnp.dot(act.astype(BF16), wdn_b[...], preferred_element_type=F32) + bdn_ref[0]
        for t in range(ROW_TILES):
            o_ref[pl.ds(t, MOE_ROWS, stride=ROW_TILES), :] = y[:, t * LANES:(t + 1) * LANES]

    @pl.when(i >= nused_ref[0])
    def _():
        o_ref[...] = jnp.zeros_like(o_ref)


def _moe_experts(blk_e, nused, xs, wgu, bgu, wdn, bdn, layer):
    n_blocks = blk_e.shape[0]
    grid_spec = pltpu.PrefetchScalarGridSpec(
        num_scalar_prefetch=2,
        grid=(n_blocks,),
        in_specs=[pl.BlockSpec((MOE_ROWS * ROW_TILES, LANES), lambda i, be, nu: (i, 0)),
                  pl.BlockSpec((1, 1, D_MODEL, 2 * D_FF), lambda i, be, nu: (layer, be[i], 0, 0)),
                  pl.BlockSpec((1, 1, 2 * D_FF), lambda i, be, nu: (be[i], 0, 0)),
                  pl.BlockSpec((1, 1, D_FF, D_MODEL), lambda i, be, nu: (layer, be[i], 0, 0)),
                  pl.BlockSpec((1, 1, D_MODEL), lambda i, be, nu: (be[i], 0, 0))],
        out_specs=pl.BlockSpec((MOE_ROWS * ROW_TILES, LANES), lambda i, be, nu: (i, 0)),
        scratch_shapes=[pltpu.VMEM((D_MODEL, 2 * D_FF), BF16), pltpu.VMEM((D_FF, D_MODEL), BF16)],
    )
    return pl.pallas_call(
        _moe_kernel,
        grid_spec=grid_spec,
        out_shape=jax.ShapeDtypeStruct(xs.shape, F32),
        compiler_params=_params("arbitrary"),
        name="moe_experts",
    )(blk_e, nused, xs, wgu, bgu, wdn, bdn)


def _gather_topk_rows(dest_ref, yb_hbm, dst, sem, tm):
    def body(c, carry):
        for u in range(ISSUE_UNROLL):
            t = c * ISSUE_UNROLL + u
            for kk in range(TOP_K):
                pltpu.make_async_copy(_row_slice(yb_hbm, dest_ref[0, 0, t * TOP_K + kk]),
                                      _row_slice(dst, kk * tm + t), sem).start(priority=kk % 2)
        return carry
    lax.fori_loop(0, tm // ISSUE_UNROLL, body, 0)


def _combine_kernel(dest_first_ref, dest_next_ref, yb_hbm, x1_ref, route_ref, g_ref, b_ref, x2_ref, x2b_ref,
                    buf, sem, *, tm):
    i = pl.program_id(0)
    n = pl.num_programs(0)
    slot = i % 2

    @pl.when(i == 0)
    def _():
        _gather_topk_rows(dest_first_ref, yb_hbm, buf.at[0], sem.at[0], tm)

    @pl.when(i + 1 < n)
    def _():
        _gather_topk_rows(dest_next_ref, yb_hbm, buf.at[1 - slot], sem.at[1 - slot], tm)

    pltpu.make_async_copy(yb_hbm.at[pl.ds(0, tm * TOP_K * ROW_TILES)], buf.at[slot], sem.at[slot]).wait()
    route = route_ref[...]
    ffn = jnp.zeros((tm, D_MODEL), F32)
    for kk in range(TOP_K):
        rows = _from_row_tiles(buf.at[slot], kk * tm * ROW_TILES, tm, ROW_TILES)
        ffn = ffn + rows * route[:, 2 * TOP_K + kk:2 * TOP_K + kk + 1]
    x2 = _layernorm(DN_ALPHA * x1_ref[...] + ffn, g_ref[...], b_ref[...])
    x2_ref[...] = x2
    x2b_ref[...] = x2.astype(BF16)


def _combine(dest, yb, x1, route, g, b, tm=512):
    t = x1.shape[0]
    tm = min(tm, t)
    nt = t // tm
    dest3 = dest.reshape(nt, 1, tm * TOP_K)
    return pl.pallas_call(
        functools.partial(_combine_kernel, tm=tm),
        grid=(nt,),
        in_specs=[pl.BlockSpec((1, 1, tm * TOP_K), lambda i: (0, 0, 0), memory_space=pltpu.SMEM),
                  pl.BlockSpec((1, 1, tm * TOP_K), lambda i: (jnp.minimum(i + 1, nt - 1), 0, 0),
                               memory_space=pltpu.SMEM),
                  pl.BlockSpec(memory_space=pl.ANY),
                  pl.BlockSpec((tm, D_MODEL), lambda i: (i, 0)),
                  pl.BlockSpec((tm, LANES), lambda i: (i, 0)),
                  pl.BlockSpec((1, D_MODEL), lambda i: (0, 0)),
                  pl.BlockSpec((1, D_MODEL), lambda i: (0, 0))],
        out_specs=[pl.BlockSpec((tm, D_MODEL), lambda i: (i, 0)),
                   pl.BlockSpec((tm, D_MODEL), lambda i: (i, 0))],
        out_shape=[jax.ShapeDtypeStruct((t, D_MODEL), F32), jax.ShapeDtypeStruct((t, D_MODEL), BF16)],
        scratch_shapes=[pltpu.VMEM((2, tm * TOP_K * ROW_TILES, LANES), F32),
                        pltpu.SemaphoreType.DMA((2,))],
        compiler_params=_params("arbitrary"),
        name="combine",
    )(dest3, dest3, yb, x1, route, g, b)


def _pad_cols(w, width):
    return jnp.pad(w, ((0, 0), (0, width - w.shape[1])))


def _pack_in_proj(w_in, b_in):
    offs = np.concatenate([[0], np.cumsum(IN_SIZES)])
    wb = jnp.concatenate([w_in, b_in[None, :]], 0)

    def piece(i):
        return wb[:, offs[i]:offs[i + 1]]

    kr = piece(2)
    kr_swapped = jnp.concatenate([kr[:, MLA_DR // 2:], kr[:, :MLA_DR // 2]], 1)
    dt = piece(5)
    grp_a = jnp.concatenate([piece(0), piece(1), _pad_cols(jnp.concatenate([kr, kr_swapped], 1), LANES),
                             _pad_cols(dt[:, :SSM_HEADS], LANES), _pad_cols(dt[:, SSM_HEADS:], LANES)], 1)
    groups = {"a": (grp_a, F32), "z": (piece(3), BF16), "xbc": (piece(4), F32),
              "cnv": (jnp.concatenate([piece(6), piece(7)], 1), F32), "gate": (piece(8), BF16)}
    return {k: (v[:-1].astype(BF16), v[-1:], dt_) for k, (v, dt_) in groups.items()}


def _pack_mla(w_uq, w_ukv, w_br_attn, seq):
    hq = MLA_DN + MLA_DR
    half = MLA_DR // 2
    wq = w_uq.reshape(MLA_Q_LORA, MLA_HEADS, hq)
    zq = jnp.zeros((MLA_Q_LORA, MLA_HEADS, HEAD_PAD - hq), F32)
    wq_main = jnp.concatenate([wq, zq], -1).reshape(MLA_Q_LORA, MLA_W)
    wq_swap = jnp.concatenate([jnp.zeros((MLA_Q_LORA, MLA_HEADS, MLA_DN), F32),
                               wq[..., MLA_DN + half:], wq[..., MLA_DN:MLA_DN + half], zq],
                              -1).reshape(MLA_Q_LORA, MLA_W)
    wkv = w_ukv.reshape(MLA_KV_LORA, MLA_HEADS, MLA_DN + MLA_DV)
    zk = jnp.zeros((MLA_KV_LORA, MLA_HEADS, HEAD_PAD - MLA_DN), F32)
    wk = jnp.concatenate([wkv[..., :MLA_DN], zk], -1).reshape(MLA_KV_LORA, MLA_W)
    wv = jnp.concatenate([wkv[..., MLA_DN:], zk], -1).reshape(MLA_KV_LORA, MLA_W)
    e2 = np.zeros((LANES, MLA_HEADS, HEAD_PAD), np.float32)
    for j in range(MLA_DR):
        e2[j, :, MLA_DN + j] = 1.0
        e2[MLA_DR + j, :, MLA_DN + j] = 1.0
    vone = np.zeros((MLA_HEADS, HEAD_PAD), np.float32)
    vone[:, MLA_DV] = 1.0
    wbr = jnp.concatenate([w_br_attn.reshape(MLA_HEADS, MLA_DV, D_MODEL),
                           jnp.zeros((MLA_HEADS, HEAD_PAD - MLA_DV, D_MODEL), F32)], 1).reshape(MLA_W, D_MODEL)
    pos = jnp.arange(seq, dtype=F32)
    inv = ROPE_THETA ** (-jnp.arange(0, MLA_DR, 2, dtype=F32) / MLA_DR)
    ang = pos[:, None] * inv[None, :]
    cos, sin = jnp.cos(ang), jnp.sin(ang)
    scale = (MLA_DN + MLA_DR) ** -0.5
    ones = jnp.ones((seq, MLA_DN), F32)
    zpad = jnp.zeros((seq, HEAD_PAD - hq), F32)
    cq = jnp.tile(jnp.concatenate([ones, cos, cos, zpad], 1) * scale, (1, MLA_HEADS))
    sq = jnp.tile(jnp.concatenate([0 * ones, -sin, sin, zpad], 1) * scale, (1, MLA_HEADS))
    tk = jnp.concatenate([cos, cos, -sin, sin, jnp.zeros((seq, LANES - 2 * MLA_DR), F32)], 1)
    return dict(wq=wq_main.astype(BF16), wqs=wq_swap.astype(BF16), wk=wk.astype(BF16), wv=wv.astype(BF16),
                e2=jnp.asarray(e2.reshape(LANES, MLA_W), BF16), vone=jnp.asarray(vone.reshape(1, MLA_W)),
                wbr=wbr.astype(BF16), cq=cq, sq=sq, tk=tk)


def _route_tables(route, cnt, n_tok):
    idx = route[:, :TOP_K].astype(jnp.int32)
    rank = route[:, TOP_K:2 * TOP_K].astype(jnp.int32)
    counts = cnt[0, :N_EXPERTS].astype(jnp.int32)
    n_blocks = -(-(n_tok * TOP_K + N_EXPERTS * (MOE_ROWS - 1)) // MOE_ROWS)
    padded = (counts + MOE_ROWS - 1) // MOE_ROWS * MOE_ROWS
    pad_end = jnp.cumsum(padded)
    pad_start = pad_end - padded
    onehot = idx[..., None] == jnp.arange(N_EXPERTS, dtype=jnp.int32)
    dest = (jnp.sum(jnp.where(onehot, pad_start, 0), -1) + rank).reshape(-1)
    blk_first = jnp.arange(n_blocks, dtype=jnp.int32) * MOE_ROWS
    blk_e = jnp.minimum(jnp.sum((pad_end[None, :] <= blk_first[:, None]).astype(jnp.int32), -1), N_EXPERTS - 1)
    nused = pad_end[-1:] // MOE_ROWS
    return dest, blk_e, nused, n_blocks * MOE_ROWS


def kernel(x, w_in, b_in, mla_q_norm, mla_kv_norm, mla_w_uq, mla_w_ukv, w_br_attn, ssm_conv_w, ssm_conv_b, ssm_dt_bias, ssm_a_log, ssm_d, ssm_norm, w_br_ssm, cnv_dw_w, cnv_dw_b, cnv_ln_g, cnv_ln_b, w_br_conv, b_br_conv, w_out, ln1_g, ln1_b, router_w, router_b, moe_w_gate_up, moe_b_gate_up, moe_w_down, moe_b_down, ln2_g, ln2_b):
    bsz, seq, d = x.shape
    n_tok = bsz * seq
    xf = x.reshape(n_tok, d)
    xb = xf.astype(BF16)
    xs = None
    for l in range(DEPTH):
        proj = _pack_in_proj(w_in[l], b_in[l])
        mla = _pack_mla(mla_w_uq[l], mla_w_ukv[l], w_br_attn[l], seq)
        a_grp = _linear(xb, *proj["a"])
        z = _linear(xb, *proj["z"])
        xbc_raw = _linear(xb, *proj["xbc"], tn=768)
        cnv_raw = _linear(xb, *proj["cnv"])
        gate_logits = _linear(xb, *proj["gate"], tn=1024)

        q, k, v = _mla_prep(a_grp, seq, mla_q_norm[l][None], mla_kv_norm[l][None], mla["wq"], mla["wqs"],
                            mla["wk"], mla["wv"], mla["e2"], mla["cq"], mla["sq"], mla["tk"], mla["vone"])
        attn = _attention(q.reshape(bsz, seq, MLA_W), k.reshape(bsz, seq, MLA_W), v.reshape(bsz, seq, MLA_W))

        xbc = _dwconv(xbc_raw.reshape(bsz, seq, SSM_CONV_DIM), ssm_conv_w[l], ssm_conv_b[l],
                      glu=False, silu_out=True)
        a3 = a_grp.reshape(bsz, seq, GROUP_A)
        dtb = jnp.pad(ssm_dt_bias[l], ((0, 0), (0, LANES - SSM_HEADS)))
        nega = jnp.pad(-jnp.exp(ssm_a_log[l]), ((0, 0), (0, LANES - SSM_HEADS)))
        y_fwd, y_bwd = _ssd(xbc, a3, dtb, nega)

        u = _dwconv(cnv_raw.reshape(bsz, seq, 2 * CNV_CH), cnv_dw_w[l], cnv_dw_b[l], glu=True, silu_out=False)

        rw = jnp.pad(router_w[l], ((0, 0), (0, LANES - N_EXPERTS)))
        rw_hi = rw.astype(BF16)
        rw_lo = (rw - rw_hi.astype(F32)).astype(BF16)
        rw = jnp.concatenate([rw_hi, rw_hi, rw_lo], 0)
        rb = jnp.pad(router_b[l], (0, LANES - N_EXPERTS), constant_values=NEG_BIG)[None]
        x1, x1g, route, cnt = _merge(
            xf, attn.reshape(n_tok, MLA_W), y_fwd.reshape(n_tok, SSM_D_INNER), y_bwd.reshape(n_tok, SSM_D_INNER),
            xbc.reshape(n_tok, SSM_CONV_DIM), z, u.reshape(n_tok, CNV_CH), gate_logits,
            mla["wbr"], w_br_ssm[l].astype(BF16), w_br_conv[l].astype(BF16), w_out[l].astype(BF16),
            jnp.repeat(ssm_d[l], SSM_HEAD_DIM)[None], ssm_norm[l][None], cnv_ln_g[l][None], cnv_ln_b[l][None],
            b_br_conv[l][None], ln1_g[l][None], ln1_b[l][None], rw, rb)

        dest, blk_e, nused, n_rows = _route_tables(route, cnt, n_tok)
        xs = _dispatch(dest, x1g, n_rows, init=xs)
        yb = _moe_experts(blk_e, nused, xs, moe_w_gate_up, moe_b_gate_up[l][:, None, :], moe_w_down,
                          moe_b_down[l][:, None, :], l)
        xf, xb = _combine(dest, yb, x1, route, ln2_g[l][None], ln2_b[l][None])
    return xf.reshape(bsz, seq, d)
```

```python
import functools
import math

import numpy as np
import jax
import jax.numpy as jnp
from jax import lax
from jax.experimental import pallas as pl
from jax.experimental.pallas import tpu as pltpu

F32 = jnp.float32
BF16 = jnp.bfloat16
HIGHEST = lax.Precision.HIGHEST

LANES = 128
SUBLANES = 8
VMEM_LIMIT_BYTES = 56 * 1024 * 1024

D_MODEL = 1024
DEPTH = 2
MLA_HEADS = 8
MLA_Q_LORA = 384
MLA_KV_LORA = 256
MLA_DN = 64
MLA_DR = 32
MLA_DV = 64
ROPE_THETA = 10000.0
SSM_HEADS = 16
SSM_HEAD_DIM = 64
SSM_D_INNER = SSM_HEADS * SSM_HEAD_DIM
SSM_GROUPS = 4
SSM_STATE = 64
SSM_CONV = 5
SSM_CHUNK = 128
SSM_BC = SSM_GROUPS * SSM_STATE
SSM_CONV_DIM = SSM_D_INNER + 2 * SSM_BC
CNV_CH = 512
CNV_WIDTH = 31
N_EXPERTS = 32
TOP_K = 4
D_FF = 1024
SWIGLU_LIMIT = 7.0
SWIGLU_ALPHA = 1.702
DN_ALPHA = (2 * DEPTH) ** 0.25
IN_SIZES = (MLA_Q_LORA, MLA_KV_LORA, MLA_DR, SSM_D_INNER, SSM_CONV_DIM, 2 * SSM_HEADS,
            CNV_CH, CNV_CH, 3 * D_MODEL)
HEAD_PAD = LANES
MLA_W = MLA_HEADS * HEAD_PAD
GROUP_A = 1024
CONV_HALO = 16
MOE_ROWS = 512
ISSUE_UNROLL = 8
ROW_TILES = D_MODEL // LANES
NEG_BIG = -1e30


def _params(*sem):
    return pltpu.CompilerParams(dimension_semantics=sem, vmem_limit_bytes=VMEM_LIMIT_BYTES)


def _linear_kernel(x_ref, w_ref, b_ref, o_ref):
    acc = jnp.dot(x_ref[...], w_ref[...], preferred_element_type=F32)
    o_ref[...] = (acc + b_ref[...]).astype(o_ref.dtype)


def _linear(x, w, b, out_dtype, tm=1024, tn=None):
    m, k = x.shape
    n = w.shape[1]
    tm = min(tm, m)
    tn = n if tn is None else tn
    return pl.pallas_call(
        _linear_kernel,
        grid=(n // tn, m // tm),
        in_specs=[pl.BlockSpec((tm, k), lambda j, i: (i, 0)),
                  pl.BlockSpec((k, tn), lambda j, i: (0, j)),
                  pl.BlockSpec((1, tn), lambda j, i: (0, j))],
        out_specs=pl.BlockSpec((tm, tn), lambda j, i: (i, j)),
        out_shape=jax.ShapeDtypeStruct((m, n), out_dtype),
        compiler_params=_params("parallel", "parallel"),
        name="linear",
    )(x, w, b)


def _rms(x, g, eps=1e-6):
    return x * lax.rsqrt(jnp.mean(x * x, -1, keepdims=True) + eps) * g


def _mla_prep_kernel(a_ref, gq_ref, gkv_ref, wq_ref, wqs_ref, wk_ref, wv_ref, e2_ref,
                     cq_ref, sq_ref, tk_ref, vone_ref, q_ref, k_ref, v_ref):
    a = a_ref[...]
    c_q = a[:, :MLA_Q_LORA]
    c_kv = a[:, MLA_Q_LORA:MLA_Q_LORA + MLA_KV_LORA]
    kr = a[:, MLA_Q_LORA + MLA_KV_LORA:]
    qn = _rms(c_q, gq_ref[...]).astype(BF16)
    kvn = _rms(c_kv, gkv_ref[...]).astype(BF16)
    q = (jnp.dot(qn, wq_ref[...], preferred_element_type=F32) * cq_ref[...]
         + jnp.dot(qn, wqs_ref[...], preferred_element_type=F32) * sq_ref[...])
    q_ref[...] = q.astype(BF16)
    krp = (kr * tk_ref[...]).astype(BF16)
    k = (jnp.dot(kvn, wk_ref[...], preferred_element_type=F32)
         + jnp.dot(krp, e2_ref[...], preferred_element_type=F32))
    k_ref[...] = k.astype(BF16)
    v = jnp.dot(kvn, wv_ref[...], preferred_element_type=F32) + vone_ref[...]
    v_ref[...] = v.astype(BF16)


def _mla_prep(a, seq, gq, gkv, wq, wqs, wk, wv, e2, cq, sq, tk, vone, tm=512):
    t = a.shape[0]
    tm = min(tm, seq)
    nper = seq // tm
    wa = MLA_Q_LORA + MLA_KV_LORA + LANES

    def const(shape):
        return pl.BlockSpec(shape, lambda i: (0, 0))

    def tab(width):
        return pl.BlockSpec((tm, width), lambda i: (i % nper, 0))

    out = jax.ShapeDtypeStruct((t, MLA_W), BF16)
    return pl.pallas_call(
        _mla_prep_kernel,
        grid=(t // tm,),
        in_specs=[pl.BlockSpec((tm, wa), lambda i: (i, 0)),
                  const((1, MLA_Q_LORA)), const((1, MLA_KV_LORA)),
                  const((MLA_Q_LORA, MLA_W)), const((MLA_Q_LORA, MLA_W)),
                  const((MLA_KV_LORA, MLA_W)), const((MLA_KV_LORA, MLA_W)),
                  const((LANES, MLA_W)),
                  tab(MLA_W), tab(MLA_W), tab(LANES), const((1, MLA_W))],
        out_specs=[pl.BlockSpec((tm, MLA_W), lambda i: (i, 0))] * 3,
        out_shape=[out, out, out],
        compiler_params=_params("parallel"),
        name="mla_prep",
    )(a, gq, gkv, wq, wqs, wk, wv, e2, cq, sq, tk, vone)


def _attn_kernel(q_ref, k_ref, v_ref, o_ref, *, sub):
    low = lax.broadcasted_iota(jnp.int32, (sub, HEAD_PAD), 1) < MLA_DV
    for r in range(q_ref.shape[1] // sub):
        rows = slice(r * sub, (r + 1) * sub)
        outs = []
        for hh in range(2):
            lanes = slice(hh * HEAD_PAD, (hh + 1) * HEAD_PAD)
            s = lax.dot_general(q_ref[0, rows, lanes], k_ref[0, :, lanes], (((1,), (1,)), ((), ())),
                                preferred_element_type=F32)
            m = jnp.max(s, -1, keepdims=True)
            p = jnp.exp(s - m).astype(BF16)
            o = jnp.dot(p, v_ref[0, :, lanes], preferred_element_type=F32)
            outs.append(o / o[:, MLA_DV:MLA_DV + 1])
        o_ref[0, rows] = jnp.where(low, outs[0], pltpu.roll(outs[1], MLA_DV, 1)).astype(o_ref.dtype)


def _attention(q, k, v, tq=1024, sub=256):
    b, s, _ = q.shape
    tq = min(tq, s)
    pair = 2 * HEAD_PAD
    return pl.pallas_call(
        functools.partial(_attn_kernel, sub=min(sub, tq)),
        grid=(b, MLA_HEADS // 2, s // tq),
        in_specs=[pl.BlockSpec((1, tq, pair), lambda bi, h, i: (bi, i, h)),
                  pl.BlockSpec((1, s, pair), lambda bi, h, i: (bi, 0, h)),
                  pl.BlockSpec((1, s, pair), lambda bi, h, i: (bi, 0, h))],
        out_specs=pl.BlockSpec((1, tq, HEAD_PAD), lambda bi, h, i: (bi, i, h)),
        out_shape=jax.ShapeDtypeStruct((b, s, MLA_HEADS * MLA_DV), BF16),
        compiler_params=_params("parallel", "parallel", "parallel"),
        name="attention",
    )(q, k, v)


def _dwconv_kernel(*refs, width, glu, silu_out, seq, rows):
    if glu:
        a_ref, g_ref, w_ref, b_ref, o_ref, pad_ref = refs
        pre = a_ref[0] * jax.nn.sigmoid(g_ref[0])
    else:
        x_ref, w_ref, b_ref, o_ref, pad_ref = refs
        pre = x_ref[0]
    ch = o_ref.shape[-1]
    halo = jnp.zeros((CONV_HALO, ch), F32)
    pad_ref[0:CONV_HALO, :] = halo
    pad_ref[CONV_HALO + seq:2 * CONV_HALO + seq, :] = halo
    pad_ref[CONV_HALO:CONV_HALO + seq, :] = pre
    half = (width - 1) // 2
    win_rows = rows + 2 * CONV_HALO

    def body(c, carry):
        base = pl.multiple_of(c * rows, rows)
        acc = jnp.zeros((rows, ch), F32) + b_ref[...]
        for t in range(width):
            acc = acc + pad_ref[pl.ds(base + (CONV_HALO - half + t), rows), :] * w_ref[t:t + 1, :]
        if silu_out:
            acc = acc * jax.nn.sigmoid(acc)
        o_ref[0, pl.ds(base, rows), :] = acc
        return carry

    lax.fori_loop(0, seq // rows, body, 0)


def _dwconv(x, w, b, *, glu, silu_out, rows=64):
    bsz, seq, cin = x.shape
    width, ch = w.shape
    nct = ch // LANES
    kern = functools.partial(_dwconv_kernel, width=width, glu=glu, silu_out=silu_out, seq=seq,
                             rows=min(rows, seq))
    xspec = pl.BlockSpec((1, seq, LANES), lambda bi, j: (bi, 0, j))
    in_specs = [xspec]
    args = [x]
    if glu:
        in_specs.append(pl.BlockSpec((1, seq, LANES), lambda bi, j: (bi, 0, j + nct)))
        args.append(x)
    in_specs += [pl.BlockSpec((width, LANES), lambda bi, j: (0, j)),
                 pl.BlockSpec((1, LANES), lambda bi, j: (0, j))]
    return pl.pallas_call(
        kern,
        grid=(bsz, nct),
        in_specs=in_specs,
        out_specs=pl.BlockSpec((1, seq, LANES), lambda bi, j: (bi, 0, j)),
        out_shape=jax.ShapeDtypeStruct((bsz, seq, ch), F32),
        scratch_shapes=[pltpu.VMEM((seq + 2 * CONV_HALO, LANES), F32)],
        compiler_params=_params("parallel", "parallel"),
        name="dwconv",
    )(*args, w, b.reshape(1, ch))


def _softplus(x):
    return jnp.maximum(x, 0.0) + jnp.log1p(jnp.exp(-jnp.abs(x)))


def _ssd_kernel(*refs):
    f_in, b_in = refs[0:6], refs[6:12]
    yf_ref, yb_ref, hf_ref, hb_ref = refs[12:16]
    _ssd_chunk(*f_in, yf_ref, hf_ref, reverse=False)
    _ssd_chunk(*b_in, yb_ref, hb_ref, reverse=True)


def _ssd_chunk(xs_ref, bm_ref, cm_ref, dt_ref, dtb_ref, nega_ref, y_ref, h_ref, *, reverse):
    ln = SSM_CHUNK

    @pl.when(pl.program_id(1) == 0)
    def _():
        h_ref[...] = jnp.zeros_like(h_ref)

    dt = _softplus(dt_ref[0] + dtb_ref[...])
    a = dt * nega_ref[...]
    row = lax.broadcasted_iota(jnp.int32, (ln, ln), 0)
    col = lax.broadcasted_iota(jnp.int32, (ln, ln), 1)
    tri = (col >= row) if reverse else (col <= row)
    a_hi = a.astype(BF16)
    rem = a - a_hi.astype(F32)
    a_mid = rem.astype(BF16)
    a_lo = (rem - a_mid.astype(F32)).astype(BF16)
    tri_b = jnp.where(tri, 1.0, 0.0).astype(BF16)
    acs = jnp.dot(jnp.concatenate([tri_b, tri_b, tri_b], 1), jnp.concatenate([a_hi, a_mid, a_lo], 0),
                  preferred_element_type=F32)
    tot = acs[0:1] if reverse else acs[ln - 1:ln]
    wst = dt * jnp.exp(tot - acs)
    etot = jnp.exp(tot)
    acs_t = acs.T
    dt_t = dt.T
    wst_t = wst.T

    xs = xs_ref[0]
    even = lax.broadcasted_iota(jnp.int32, xs.shape, 1) % LANES < SSM_HEAD_DIM
    xs_even = jnp.where(even, xs, 0.0).astype(BF16)
    xs_odd = jnp.where(even, 0.0, xs).astype(BF16)
    cm_b = cm_ref[0].astype(BF16)
    bm_t = bm_ref[0].T
    bm_tb = bm_t.astype(BF16)
    y_off = jnp.dot(cm_b, h_ref[...].astype(BF16), preferred_element_type=F32)

    glane = lax.broadcasted_iota(jnp.int32, (ln, SSM_BC), 1) // SSM_STATE
    lo = lax.broadcasted_iota(jnp.int32, (ln, LANES), 1) < SSM_HEAD_DIM
    lo_1 = lax.broadcasted_iota(jnp.int32, (1, LANES), 1) < SSM_HEAD_DIM
    heads_per_group = SSM_HEADS // SSM_GROUPS
    for g in range(SSM_GROUPS):
        cmg = jnp.where(glane == g, cm_b, jnp.zeros_like(cm_b))
        cb = jnp.dot(cmg, bm_tb, preferred_element_type=F32)
        bm_tg = bm_t[g * SSM_STATE:(g + 1) * SSM_STATE, :]
        for j in range(heads_per_group // 2):
            pair = g * (heads_per_group // 2) + j
            lanes = slice(pair * LANES, (pair + 1) * LANES)
            x2 = jnp.concatenate([xs_even[:, lanes], xs_odd[:, lanes]], 0)
            mats, ecol, wsts, et = [], [], [], []
            for hh in range(2):
                h = 2 * pair + hh
                colb = jnp.broadcast_to(acs[:, h:h + 1], (ln, LANES))
                decay = jnp.exp(jnp.where(tri, colb - acs_t[h:h + 1, :], -jnp.inf))
                mats.append((cb * decay * dt_t[h:h + 1, :]).astype(BF16))
                ecol.append(jnp.exp(colb))
                wsts.append((bm_tg * wst_t[h:h + 1, :]).astype(BF16))
                et.append(jnp.broadcast_to(etot[:, h:h + 1], (1, LANES)))
            diag = jnp.dot(jnp.concatenate(mats, 1), x2, preferred_element_type=F32)
            st = jnp.dot(jnp.concatenate(wsts, 1), x2, preferred_element_type=F32)
            y_ref[0, :, lanes] = diag + y_off[:, lanes] * jnp.where(lo, ecol[0], ecol[1])
            rows = slice(g * SSM_STATE, (g + 1) * SSM_STATE)
            h_ref[rows, lanes] = h_ref[rows, lanes] * jnp.where(lo_1, et[0], et[1]) + st


def _ssd(xbc, a_grp, dt_bias, neg_a):
    bsz, seq, _ = xbc.shape
    nc = seq // SSM_CHUNK

    def direction(reverse):
        dt_tile = 7 if reverse else 6
        row = 1 if reverse else 0

        def cidx(c):
            return (nc - 1 - c) if reverse else c

        ins = [pl.BlockSpec((1, SSM_CHUNK, SSM_D_INNER), lambda b, c: (b, cidx(c), 0)),
               pl.BlockSpec((1, SSM_CHUNK, SSM_BC), lambda b, c: (b, cidx(c), SSM_D_INNER // SSM_BC)),
               pl.BlockSpec((1, SSM_CHUNK, SSM_BC), lambda b, c: (b, cidx(c), SSM_D_INNER // SSM_BC + 1)),
               pl.BlockSpec((1, SSM_CHUNK, LANES), lambda b, c: (b, cidx(c), dt_tile)),
               pl.BlockSpec((1, LANES), lambda b, c: (0, 0)),
               pl.BlockSpec((1, LANES), lambda b, c: (0, 0))]
        out = pl.BlockSpec((1, SSM_CHUNK, SSM_D_INNER), lambda b, c: (b, cidx(c), 0))
        return ins, out, (xbc, xbc, xbc, a_grp, dt_bias[row:row + 1], neg_a[row:row + 1])

    f_ins, f_out, f_args = direction(False)
    b_ins, b_out, b_args = direction(True)
    y = jax.ShapeDtypeStruct((bsz, seq, SSM_D_INNER), F32)
    return pl.pallas_call(
        _ssd_kernel,
        grid=(bsz, nc),
        in_specs=f_ins + b_ins,
        out_specs=[f_out, b_out],
        out_shape=[y, y],
        scratch_shapes=[pltpu.VMEM((SSM_BC, SSM_D_INNER), F32), pltpu.VMEM((SSM_BC, SSM_D_INNER), F32)],
        compiler_params=_params("parallel", "arbitrary"),
        name="ssd",
    )(*f_args, *b_args)


def _layernorm(x, g, b, eps=1e-5):
    mu = jnp.mean(x, -1, keepdims=True)
    xc = x - mu
    var = jnp.mean(xc * xc, -1, keepdims=True)
    return xc * lax.rsqrt(var + eps) * g + b


def _merge_kernel(x_ref, o_ref, yf_ref, yb_ref, xs_ref, z_ref, u_ref, gl_ref,
                  wa_ref, ws_ref, wc_ref, wo_ref, dsk_ref, ng_ref, cg_ref, cb_ref, bc_ref,
                  l1g_ref, l1b_ref, rw_ref, rb_ref,
                  x1_ref, x1g_ref, route_ref, cnt_ref, carry_ref):
    tm = x_ref.shape[0]

    @pl.when(pl.program_id(0) == 0)
    def _():
        carry_ref[...] = jnp.zeros_like(carry_ref)

    y_attn = jnp.dot(o_ref[...], wa_ref[...], preferred_element_type=F32)
    z = z_ref[...].astype(F32)
    ys = (yf_ref[...] + yb_ref[...] + xs_ref[...] * dsk_ref[...]) * (z * jax.nn.sigmoid(z))
    gw = SSM_D_INNER // SSM_GROUPS
    ys = jnp.concatenate(
        [_rms(ys[:, g * gw:(g + 1) * gw], ng_ref[:, g * gw:(g + 1) * gw]) for g in range(SSM_GROUPS)], -1)
    y_ssm = jnp.dot(ys.astype(BF16), ws_ref[...], preferred_element_type=F32)
    uc = _layernorm(u_ref[...], cg_ref[...], cb_ref[...])
    uc = uc * jax.nn.sigmoid(uc)
    y_conv = jnp.dot(uc.astype(BF16), wc_ref[...], preferred_element_type=F32) + bc_ref[...]
    gl = gl_ref[...].astype(F32)
    mixed = (jax.nn.sigmoid(gl[:, :D_MODEL]) * y_attn
             + jax.nn.sigmoid(gl[:, D_MODEL:2 * D_MODEL]) * y_ssm
             + jax.nn.sigmoid(gl[:, 2 * D_MODEL:]) * y_conv)
    mixed = jnp.dot(mixed.astype(BF16), wo_ref[...], preferred_element_type=F32)
    x1 = _layernorm(DN_ALPHA * x_ref[...] + mixed, l1g_ref[...], l1b_ref[...])
    x1_ref[...] = x1
    for t in range(ROW_TILES):
        x1g_ref[pl.ds(t, tm, stride=ROW_TILES), :] = x1[:, t * LANES:(t + 1) * LANES]

    x_hi = x1.astype(BF16)
    x_lo = (x1 - x_hi.astype(F32)).astype(BF16)
    lg = jnp.dot(jnp.concatenate([x_hi, x_lo, x_hi], -1), rw_ref[...], preferred_element_type=F32) + rb_ref[...]
    lane = lax.broadcasted_iota(jnp.int32, (tm, LANES), 1).astype(F32)
    sels, vals, idxs = [], [], []
    for _ in range(TOP_K):
        m = jnp.max(lg, -1, keepdims=True)
        idx = jnp.min(jnp.where(lg == m, lane, float(LANES)), -1, keepdims=True)
        sel = lane == idx
        lg = jnp.where(sel, NEG_BIG * 2, lg)
        sels.append(sel)
        vals.append(m)
        idxs.append(idx)
    es = [jnp.exp(v - vals[0]) for v in vals]
    den = es[0] + es[1] + es[2] + es[3]
    hot = jnp.zeros((tm, LANES), F32)
    for sel in sels:
        hot = hot + sel.astype(F32)
    r = lax.broadcasted_iota(jnp.int32, (tm, tm), 0)
    c = lax.broadcasted_iota(jnp.int32, (tm, tm), 1)
    excl = jnp.dot((c < r).astype(BF16), hot.astype(BF16), preferred_element_type=F32) + carry_ref[...]
    route = jnp.zeros((tm, LANES), F32)
    for kk in range(TOP_K):
        rank = jnp.sum(jnp.where(sels[kk], excl, 0.0), -1, keepdims=True)
        route = jnp.where(lane == float(kk), idxs[kk], route)
        route = jnp.where(lane == float(TOP_K + kk), rank, route)
        route = jnp.where(lane == float(2 * TOP_K + kk), es[kk] / den, route)
    route_ref[...] = route
    carry_ref[...] = carry_ref[...] + jnp.sum(hot, 0, keepdims=True)
    cnt_ref[...] = carry_ref[...]


def _merge(x, o, yf, yb, xbc, z, u, gl, wa, ws, wc, wo, dsk, ng, cg, cb, bc, l1g, l1b, rw, rb, tm=256):
    t = x.shape[0]
    tm = min(tm, t)

    def rowb(width, col=0):
        return pl.BlockSpec((tm, width), lambda i: (i, col))

    def const(shape):
        return pl.BlockSpec(shape, lambda i: (0, 0))

    return pl.pallas_call(
        _merge_kernel,
        grid=(t // tm,),
        in_specs=[rowb(D_MODEL), rowb(MLA_HEADS * MLA_DV), rowb(SSM_D_INNER), rowb(SSM_D_INNER), rowb(SSM_D_INNER),
                  rowb(SSM_D_INNER), rowb(CNV_CH), rowb(3 * D_MODEL),
                  const((MLA_HEADS * MLA_DV, D_MODEL)), const((SSM_D_INNER, D_MODEL)), const((CNV_CH, D_MODEL)),
                  const((D_MODEL, D_MODEL)), const((1, SSM_D_INNER)), const((1, SSM_D_INNER)),
                  const((1, CNV_CH)), const((1, CNV_CH)), const((1, D_MODEL)),
                  const((1, D_MODEL)), const((1, D_MODEL)), const((3 * D_MODEL, LANES)), const((1, LANES))],
        out_specs=[rowb(D_MODEL),
                   pl.BlockSpec((tm * ROW_TILES, LANES), lambda i: (i, 0)),
                   rowb(LANES), const((1, LANES))],
        out_shape=[jax.ShapeDtypeStruct((t, D_MODEL), F32),
                   jax.ShapeDtypeStruct((t * ROW_TILES, LANES), F32),
                   jax.ShapeDtypeStruct((t, LANES), F32), jax.ShapeDtypeStruct((1, LANES), F32)],
        scratch_shapes=[pltpu.VMEM((1, LANES), F32)],
        compiler_params=_params("arbitrary"),
        name="merge",
    )(x, o, yf, yb, xbc, z, u, gl, wa, ws, wc, wo, dsk, ng, cg, cb, bc, l1g, l1b, rw, rb)


def _row_slice(ref, row):
    return ref.at[pl.ds(pl.multiple_of(row * ROW_TILES, ROW_TILES), ROW_TILES)]


def _from_row_tiles(ref, first, rows, pitch):
    return jnp.concatenate([ref[pl.ds(first + t, rows, stride=pitch), :] for t in range(ROW_TILES)], -1)


def _fill_padding(fill_start_ref, fill_n_ref, zero_ref, xs_hbm, sem):
    zero_ref[...] = jnp.zeros_like(zero_ref)
    bits = [1 << b for b in reversed(range((MOE_ROWS - 1).bit_length()))]

    def copies(e):
        n_e = fill_n_ref[e]
        for bit in bits:
            done = n_e & ~(2 * bit - 1)
            yield (n_e & bit) != 0, pltpu.make_async_copy(
                zero_ref.at[pl.ds(0, bit * ROW_TILES)],
                xs_hbm.at[pl.ds(pl.multiple_of((fill_start_ref[e] + done) * ROW_TILES, ROW_TILES),
                                bit * ROW_TILES)], sem)

    def start(e, carry):
        for cond, cp in copies(e):
            pl.when(cond)(cp.start)
        return carry

    def wait(e, carry):
        for cond, cp in copies(e):
            pl.when(cond)(cp.wait)
        return carry

    def tail(j):
        first = (fill_start_ref[N_EXPERTS] * ROW_TILES + j * zero_ref.shape[0])
        return pltpu.make_async_copy(zero_ref, xs_hbm.at[pl.ds(pl.multiple_of(first, ROW_TILES),
                                                               zero_ref.shape[0])], sem)

    def tail_start(j, carry):
        tail(j).start()
        return carry

    def tail_wait(j, carry):
        tail(j).wait()
        return carry

    lax.fori_loop(0, N_EXPERTS, start, 0)
    lax.fori_loop(0, fill_n_ref[N_EXPERTS], tail_start, 0)
    lax.fori_loop(0, N_EXPERTS, wait, 0)
    lax.fori_loop(0, fill_n_ref[N_EXPERTS], tail_wait, 0)


def _dispatch_kernel(fill_start_ref, fill_n_ref, dest_ref, x_ref, xs_hbm, sem, fill_sem, zero_ref, *, tm):
    @pl.when(pl.program_id(0) == pl.num_programs(0) - 1)
    def _():
        _fill_padding(fill_start_ref, fill_n_ref, zero_ref, xs_hbm, fill_sem)

    def body(c, carry):
        for u in range(ISSUE_UNROLL):
            t = c * ISSUE_UNROLL + u
            src = _row_slice(x_ref, t)
            for kk in range(TOP_K):
                pltpu.make_async_copy(src, _row_slice(xs_hbm, dest_ref[0, 0, t * TOP_K + kk]),
                                      sem).start(priority=kk % 2)
        return carry

    lax.fori_loop(0, tm // ISSUE_UNROLL, body, 0)
    for _ in range(TOP_K):
        pltpu.make_async_copy(x_ref, xs_hbm.at[pl.ds(0, tm * ROW_TILES)], sem).wait()


def _dispatch(dest, fill_start, fill_n, x1g, n_rows, tm=512):
    t = x1g.shape[0] // ROW_TILES
    tm = min(tm, t)
    nt = t // tm
    grid_spec = pltpu.PrefetchScalarGridSpec(
        num_scalar_prefetch=2,
        grid=(nt,),
        in_specs=[pl.BlockSpec((1, 1, tm * TOP_K), lambda i, fs, fn: (i, 0, 0), memory_space=pltpu.SMEM),
                  pl.BlockSpec((tm * ROW_TILES, LANES), lambda i, fs, fn: (i, 0))],
        out_specs=pl.BlockSpec(memory_space=pl.ANY),
        scratch_shapes=[pltpu.SemaphoreType.DMA(()), pltpu.SemaphoreType.DMA(()),
                        pltpu.VMEM((MOE_ROWS // 2 * ROW_TILES, LANES), F32)],
    )
    return pl.pallas_call(
        functools.partial(_dispatch_kernel, tm=tm),
        grid_spec=grid_spec,
        out_shape=jax.ShapeDtypeStruct((n_rows * ROW_TILES, LANES), F32),
        compiler_params=_params("arbitrary"),
        name="dispatch",
    )(fill_start, fill_n, dest.reshape(nt, 1, tm * TOP_K), x1g)


def _moe_kernel(blk_e_ref, nused_ref, x_ref, wgu_ref, bgu_ref, wdn_ref, bdn_ref, o_ref, wgu_b, wdn_b):
    i = pl.program_id(0)

    @pl.when((i == 0) | (blk_e_ref[i] != blk_e_ref[jnp.maximum(i - 1, 0)]))
    def _():
        wgu_b[...] = wgu_ref[0, 0].astype(BF16)
        wdn_b[...] = wdn_ref[0, 0].astype(BF16)

    @pl.when(i < nused_ref[0])
    def _():
        xb = _from_row_tiles(x_ref, 0, MOE_ROWS, ROW_TILES).astype(BF16)
        h = jnp.dot(xb, wgu_b[...], preferred_element_type=F32) + bgu_ref[0]
        gate = jnp.minimum(h[:, :D_FF], SWIGLU_LIMIT)
        up = jnp.clip(h[:, D_FF:], -SWIGLU_LIMIT, SWIGLU_LIMIT)
        act = (up + 1.0) * (gate * jax.nn.sigmoid(SWIGLU_ALPHA * gate))
        y = jnp.dot(act.astype(BF16), wdn_b[...], preferred_element_type=F32) + bdn_ref[0]
        for t in range(ROW_TILES):
            o_ref[pl.ds(t, MOE_ROWS, stride=ROW_TILES), :] = y[:, t * LANES:(t + 1) * LANES]

    @pl.when(i >= nused_ref[0])
    def _():
        o_ref[...] = jnp.zeros_like(o_ref)


def _moe_experts(blk_e, nused, xs, wgu, bgu, wdn, bdn, layer):
    n_blocks = blk_e.shape[0]
    grid_spec = pltpu.PrefetchScalarGridSpec(
        num_scalar_prefetch=2,
        grid=(n_blocks,),
        in_specs=[pl.BlockSpec((MOE_ROWS * ROW_TILES, LANES), lambda i, be, nu: (jnp.minimum(i, nu[0] - 1), 0)),
                  pl.BlockSpec((1, 1, D_MODEL, 2 * D_FF), lambda i, be, nu: (layer, be[i], 0, 0)),
                  pl.BlockSpec((1, 1, 2 * D_FF), lambda i, be, nu: (be[i], 0, 0)),
                  pl.BlockSpec((1, 1, D_FF, D_MODEL), lambda i, be, nu: (layer, be[i], 0, 0)),
                  pl.BlockSpec((1, 1, D_MODEL), lambda i, be, nu: (be[i], 0, 0))],
        out_specs=pl.BlockSpec((MOE_ROWS * ROW_TILES, LANES), lambda i, be, nu: (i, 0)),
        scratch_shapes=[pltpu.VMEM((D_MODEL, 2 * D_FF), BF16), pltpu.VMEM((D_FF, D_MODEL), BF16)],
    )
    return pl.pallas_call(
        _moe_kernel,
        grid_spec=grid_spec,
        out_shape=jax.ShapeDtypeStruct(xs.shape, F32),
        compiler_params=_params("arbitrary"),
        name="moe_experts",
    )(blk_e, nused, xs, wgu, bgu, wdn, bdn)


def _gather_topk_rows(dest_ref, yb_hbm, dst, sem, tm):
    def body(c, carry):
        for u in range(ISSUE_UNROLL):
            t = c * ISSUE_UNROLL + u
            for kk in range(TOP_K):
                pltpu.make_async_copy(_row_slice(yb_hbm, dest_ref[0, 0, t * TOP_K + kk]),
                                      _row_slice(dst, kk * tm + t), sem).start(priority=kk % 2)
        return carry
    lax.fori_loop(0, tm // ISSUE_UNROLL, body, 0)


def _combine_kernel(dest_first_ref, dest_next_ref, yb_hbm, x1_ref, route_ref, g_ref, b_ref, x2_ref, x2b_ref,
                    buf, sem, *, tm):
    i = pl.program_id(0)
    n = pl.num_programs(0)
    slot = i % 2

    @pl.when(i == 0)
    def _():
        _gather_topk_rows(dest_first_ref, yb_hbm, buf.at[0], sem.at[0], tm)

    @pl.when(i + 1 < n)
    def _():
        _gather_topk_rows(dest_next_ref, yb_hbm, buf.at[1 - slot], sem.at[1 - slot], tm)

    pltpu.make_async_copy(yb_hbm.at[pl.ds(0, tm * TOP_K * ROW_TILES)], buf.at[slot], sem.at[slot]).wait()
    route = route_ref[...]
    ffn = jnp.zeros((tm, D_MODEL), F32)
    for kk in range(TOP_K):
        rows = _from_row_tiles(buf.at[slot], kk * tm * ROW_TILES, tm, ROW_TILES)
        ffn = ffn + rows * route[:, 2 * TOP_K + kk:2 * TOP_K + kk + 1]
    x2 = _layernorm(DN_ALPHA * x1_ref[...] + ffn, g_ref[...], b_ref[...])
    x2_ref[...] = x2
    x2b_ref[...] = x2.astype(BF16)


def _combine(dest, yb, x1, route, g, b, tm=512):
    t = x1.shape[0]
    tm = min(tm, t)
    nt = t // tm
    dest3 = dest.reshape(nt, 1, tm * TOP_K)
    return pl.pallas_call(
        functools.partial(_combine_kernel, tm=tm),
        grid=(nt,),
        in_specs=[pl.BlockSpec((1, 1, tm * TOP_K), lambda i: (0, 0, 0), memory_space=pltpu.SMEM),
                  pl.BlockSpec((1, 1, tm * TOP_K), lambda i: (jnp.minimum(i + 1, nt - 1), 0, 0),
                               memory_space=pltpu.SMEM),
                  pl.BlockSpec(memory_space=pl.ANY),
                  pl.BlockSpec((tm, D_MODEL), lambda i: (i, 0)),
                  pl.BlockSpec((tm, LANES), lambda i: (i, 0)),
                  pl.BlockSpec((1, D_MODEL), lambda i: (0, 0)),
                  pl.BlockSpec((1, D_MODEL), lambda i: (0, 0))],
        out_specs=[pl.BlockSpec((tm, D_MODEL), lambda i: (i, 0)),
                   pl.BlockSpec((tm, D_MODEL), lambda i: (i, 0))],
        out_shape=[jax.ShapeDtypeStruct((t, D_MODEL), F32), jax.ShapeDtypeStruct((t, D_MODEL), BF16)],
        scratch_shapes=[pltpu.VMEM((2, tm * TOP_K * ROW_TILES, LANES), F32),
                        pltpu.SemaphoreType.DMA((2,))],
        compiler_params=_params("arbitrary"),
        name="combine",
    )(dest3, dest3, yb, x1, route, g, b)


def _pad_cols(w, width):
    return jnp.pad(w, ((0, 0), (0, width - w.shape[1])))


def _pack_in_proj(w_in, b_in):
    offs = np.concatenate([[0], np.cumsum(IN_SIZES)])
    wb = jnp.concatenate([w_in, b_in[None, :]], 0)

    def piece(i):
        return wb[:, offs[i]:offs[i + 1]]

    kr = piece(2)
    kr_swapped = jnp.concatenate([kr[:, MLA_DR // 2:], kr[:, :MLA_DR // 2]], 1)
    dt = piece(5)
    grp_a = jnp.concatenate([piece(0), piece(1), _pad_cols(jnp.concatenate([kr, kr_swapped], 1), LANES),
                             _pad_cols(dt[:, :SSM_HEADS], LANES), _pad_cols(dt[:, SSM_HEADS:], LANES)], 1)
    groups = {"a": (grp_a, F32), "z": (piece(3), BF16), "xbc": (piece(4), F32),
              "cnv": (jnp.concatenate([piece(6), piece(7)], 1), F32), "gate": (piece(8), BF16)}
    return {k: (v[:-1].astype(BF16), v[-1:], dt_) for k, (v, dt_) in groups.items()}


def _pack_mla(w_uq, w_ukv, w_br_attn, seq):
    hq = MLA_DN + MLA_DR
    half = MLA_DR // 2
    wq = w_uq.reshape(MLA_Q_LORA, MLA_HEADS, hq)
    zq = jnp.zeros((MLA_Q_LORA, MLA_HEADS, HEAD_PAD - hq), F32)
    wq_main = jnp.concatenate([wq, zq], -1).reshape(MLA_Q_LORA, MLA_W)
    wq_swap = jnp.concatenate([jnp.zeros((MLA_Q_LORA, MLA_HEADS, MLA_DN), F32),
                               wq[..., MLA_DN + half:], wq[..., MLA_DN:MLA_DN + half], zq],
                              -1).reshape(MLA_Q_LORA, MLA_W)
    wkv = w_ukv.reshape(MLA_KV_LORA, MLA_HEADS, MLA_DN + MLA_DV)
    zk = jnp.zeros((MLA_KV_LORA, MLA_HEADS, HEAD_PAD - MLA_DN), F32)
    wk = jnp.concatenate([wkv[..., :MLA_DN], zk], -1).reshape(MLA_KV_LORA, MLA_W)
    wv = jnp.concatenate([wkv[..., MLA_DN:], zk], -1).reshape(MLA_KV_LORA, MLA_W)
    e2 = np.zeros((LANES, MLA_HEADS, HEAD_PAD), np.float32)
    for j in range(MLA_DR):
        e2[j, :, MLA_DN + j] = 1.0
        e2[MLA_DR + j, :, MLA_DN + j] = 1.0
    vone = np.zeros((MLA_HEADS, HEAD_PAD), np.float32)
    vone[:, MLA_DV] = 1.0
    pos = jnp.arange(seq, dtype=F32)
    inv = ROPE_THETA ** (-jnp.arange(0, MLA_DR, 2, dtype=F32) / MLA_DR)
    ang = pos[:, None] * inv[None, :]
    cos, sin = jnp.cos(ang), jnp.sin(ang)
    scale = (MLA_DN + MLA_DR) ** -0.5
    ones = jnp.ones((seq, MLA_DN), F32)
    zpad = jnp.zeros((seq, HEAD_PAD - hq), F32)
    cq = jnp.tile(jnp.concatenate([ones, cos, cos, zpad], 1) * scale, (1, MLA_HEADS))
    sq = jnp.tile(jnp.concatenate([0 * ones, -sin, sin, zpad], 1) * scale, (1, MLA_HEADS))
    tk = jnp.concatenate([cos, cos, -sin, sin, jnp.zeros((seq, LANES - 2 * MLA_DR), F32)], 1)
    return dict(wq=wq_main.astype(BF16), wqs=wq_swap.astype(BF16), wk=wk.astype(BF16), wv=wv.astype(BF16),
                e2=jnp.asarray(e2.reshape(LANES, MLA_W), BF16), vone=jnp.asarray(vone.reshape(1, MLA_W)),
                wbr=w_br_attn.astype(BF16), cq=cq, sq=sq, tk=tk)


def _route_tables(route, cnt, n_tok):
    idx = route[:, :TOP_K].astype(jnp.int32)
    rank = route[:, TOP_K:2 * TOP_K].astype(jnp.int32)
    counts = cnt[0, :N_EXPERTS].astype(jnp.int32)
    n_blocks = -(-(n_tok * TOP_K + N_EXPERTS * (MOE_ROWS - 1)) // MOE_ROWS)
    padded = (counts + MOE_ROWS - 1) // MOE_ROWS * MOE_ROWS
    pad_end = jnp.cumsum(padded)
    pad_start = pad_end - padded
    onehot = idx[..., None] == jnp.arange(N_EXPERTS, dtype=jnp.int32)
    dest = (jnp.sum(jnp.where(onehot, pad_start, 0), -1) + rank).reshape(-1)
    blk_first = jnp.arange(n_blocks, dtype=jnp.int32) * MOE_ROWS
    blk_e = jnp.minimum(jnp.sum((pad_end[None, :] <= blk_first[:, None]).astype(jnp.int32), -1), N_EXPERTS - 1)
    nused = pad_end[-1:] // MOE_ROWS
    n_rows = n_blocks * MOE_ROWS
    fill_start = jnp.concatenate([pad_start + counts, pad_end[-1:]])
    fill_n = jnp.concatenate([padded - counts, (n_rows - pad_end[-1:]) // (MOE_ROWS // 2)])
    return dest, blk_e, nused, fill_start, fill_n, n_rows


def kernel(x, w_in, b_in, mla_q_norm, mla_kv_norm, mla_w_uq, mla_w_ukv, w_br_attn, ssm_conv_w, ssm_conv_b, ssm_dt_bias, ssm_a_log, ssm_d, ssm_norm, w_br_ssm, cnv_dw_w, cnv_dw_b, cnv_ln_g, cnv_ln_b, w_br_conv, b_br_conv, w_out, ln1_g, ln1_b, router_w, router_b, moe_w_gate_up, moe_b_gate_up, moe_w_down, moe_b_down, ln2_g, ln2_b):
    bsz, seq, d = x.shape
    n_tok = bsz * seq
    xf = x.reshape(n_tok, d)
    xb = xf.astype(BF16)
    for l in range(DEPTH):
        proj = _pack_in_proj(w_in[l], b_in[l])
        mla = _pack_mla(mla_w_uq[l], mla_w_ukv[l], w_br_attn[l], seq)
        a_grp = _linear(xb, *proj["a"])
        z = _linear(xb, *proj["z"])
        xbc_raw = _linear(xb, *proj["xbc"], tn=768)
        cnv_raw = _linear(xb, *proj["cnv"])
        gate_logits = _linear(xb, *proj["gate"], tn=1024)

        q, k, v = _mla_prep(a_grp, seq, mla_q_norm[l][None], mla_kv_norm[l][None], mla["wq"], mla["wqs"],
                            mla["wk"], mla["wv"], mla["e2"], mla["cq"], mla["sq"], mla["tk"], mla["vone"])
        attn = _attention(q.reshape(bsz, seq, MLA_W), k.reshape(bsz, seq, MLA_W), v.reshape(bsz, seq, MLA_W))

        xbc = _dwconv(xbc_raw.reshape(bsz, seq, SSM_CONV_DIM), ssm_conv_w[l], ssm_conv_b[l],
                      glu=False, silu_out=True, rows=256)
        a3 = a_grp.reshape(bsz, seq, GROUP_A)
        dtb = jnp.pad(ssm_dt_bias[l], ((0, 0), (0, LANES - SSM_HEADS)))
        nega = jnp.pad(-jnp.exp(ssm_a_log[l]), ((0, 0), (0, LANES - SSM_HEADS)))
        y_fwd, y_bwd = _ssd(xbc, a3, dtb, nega)

        u = _dwconv(cnv_raw.reshape(bsz, seq, 2 * CNV_CH), cnv_dw_w[l], cnv_dw_b[l], glu=True, silu_out=False,
                    rows=128)

        rw = jnp.pad(router_w[l], ((0, 0), (0, LANES - N_EXPERTS)))
        rw_hi = rw.astype(BF16)
        rw_lo = (rw - rw_hi.astype(F32)).astype(BF16)
        rw = jnp.concatenate([rw_hi, rw_hi, rw_lo], 0)
        rb = jnp.pad(router_b[l], (0, LANES - N_EXPERTS), constant_values=NEG_BIG)[None]
        x1, x1g, route, cnt = _merge(
            xf, attn.reshape(n_tok, MLA_HEADS * MLA_DV), y_fwd.reshape(n_tok, SSM_D_INNER), y_bwd.reshape(n_tok, SSM_D_INNER),
            xbc.reshape(n_tok, SSM_CONV_DIM), z, u.reshape(n_tok, CNV_CH), gate_logits,
            mla["wbr"], w_br_ssm[l].astype(BF16), w_br_conv[l].astype(BF16), w_out[l].astype(BF16),
            jnp.repeat(ssm_d[l], SSM_HEAD_DIM)[None], ssm_norm[l][None], cnv_ln_g[l][None], cnv_ln_b[l][None],
            b_br_conv[l][None], ln1_g[l][None], ln1_b[l][None], rw, rb)

        dest, blk_e, nused, fill_start, fill_n, n_rows = _route_tables(route, cnt, n_tok)
        xs = _dispatch(dest, fill_start, fill_n, x1g, n_rows)
        yb = _moe_experts(blk_e, nused, xs, moe_w_gate_up, moe_b_gate_up[l][:, None, :], moe_w_down,
                          moe_b_down[l][:, None, :], l)
        xf, xb = _combine(dest, yb, x1, route, ln2_g[l][None], ln2_b[l][None])
    return xf.reshape(bsz, seq, d)
```

```python
import functools
import math

import numpy as np
import jax
import jax.numpy as jnp
from jax import lax
from jax.experimental import pallas as pl
from jax.experimental.pallas import tpu as pltpu

F32 = jnp.float32
BF16 = jnp.bfloat16
HIGHEST = lax.Precision.HIGHEST

LANES = 128
SUBLANES = 8
VMEM_LIMIT_BYTES = 56 * 1024 * 1024

D_MODEL = 1024
DEPTH = 2
MLA_HEADS = 8
MLA_Q_LORA = 384
MLA_KV_LORA = 256
MLA_DN = 64
MLA_DR = 32
MLA_DV = 64
ROPE_THETA = 10000.0
SSM_HEADS = 16
SSM_HEAD_DIM = 64
SSM_D_INNER = SSM_HEADS * SSM_HEAD_DIM
SSM_GROUPS = 4
SSM_STATE = 64
SSM_CONV = 5
SSM_CHUNK = 128
SSM_BC = SSM_GROUPS * SSM_STATE
SSM_CONV_DIM = SSM_D_INNER + 2 * SSM_BC
CNV_CH = 512
CNV_WIDTH = 31
N_EXPERTS = 32
TOP_K = 4
D_FF = 1024
SWIGLU_LIMIT = 7.0
SWIGLU_ALPHA = 1.702
DN_ALPHA = (2 * DEPTH) ** 0.25
IN_SIZES = (MLA_Q_LORA, MLA_KV_LORA, MLA_DR, SSM_D_INNER, SSM_CONV_DIM, 2 * SSM_HEADS,
            CNV_CH, CNV_CH, 3 * D_MODEL)
HEAD_PAD = LANES
MLA_W = MLA_HEADS * HEAD_PAD
GROUP_A = 1024
CONV_HALO = 16
MOE_ROWS = 512
ISSUE_UNROLL = 8
ROW_TILES = D_MODEL // LANES
NEG_BIG = -1e30


def _params(*sem):
    return pltpu.CompilerParams(dimension_semantics=sem, vmem_limit_bytes=VMEM_LIMIT_BYTES)


def _linear_kernel(x_ref, w_ref, b_ref, o_ref):
    acc = jnp.dot(x_ref[...], w_ref[...], preferred_element_type=F32)
    o_ref[...] = (acc + b_ref[...]).astype(o_ref.dtype)


def _linear(x, w, b, out_dtype, tm=1024, tn=None):
    m, k = x.shape
    n = w.shape[1]
    tm = min(tm, m)
    tn = n if tn is None else tn
    return pl.pallas_call(
        _linear_kernel,
        grid=(n // tn, m // tm),
        in_specs=[pl.BlockSpec((tm, k), lambda j, i: (i, 0)),
                  pl.BlockSpec((k, tn), lambda j, i: (0, j)),
                  pl.BlockSpec((1, tn), lambda j, i: (0, j))],
        out_specs=pl.BlockSpec((tm, tn), lambda j, i: (i, j)),
        out_shape=jax.ShapeDtypeStruct((m, n), out_dtype),
        compiler_params=_params("parallel", "parallel"),
        name="linear",
    )(x, w, b)


def _rms(x, g, eps=1e-6):
    return x * lax.rsqrt(jnp.mean(x * x, -1, keepdims=True) + eps) * g


def _mla_prep_kernel(a_ref, gq_ref, gkv_ref, wq_ref, wqs_ref, wk_ref, wv_ref, e2_ref,
                     cq_ref, sq_ref, tk_ref, vone_ref, q_ref, k_ref, v_ref):
    a = a_ref[...]
    c_q = a[:, :MLA_Q_LORA]
    c_kv = a[:, MLA_Q_LORA:MLA_Q_LORA + MLA_KV_LORA]
    kr = a[:, MLA_Q_LORA + MLA_KV_LORA:]
    qn = _rms(c_q, gq_ref[...]).astype(BF16)
    kvn = _rms(c_kv, gkv_ref[...]).astype(BF16)
    q = (jnp.dot(qn, wq_ref[...], preferred_element_type=F32) * cq_ref[...]
         + jnp.dot(qn, wqs_ref[...], preferred_element_type=F32) * sq_ref[...])
    q_ref[...] = q.astype(BF16)
    krp = (kr * tk_ref[...]).astype(BF16)
    k = (jnp.dot(kvn, wk_ref[...], preferred_element_type=F32)
         + jnp.dot(krp, e2_ref[...], preferred_element_type=F32))
    k_ref[...] = k.astype(BF16)
    v = jnp.dot(kvn, wv_ref[...], preferred_element_type=F32) + vone_ref[...]
    v_ref[...] = v.astype(BF16)


def _mla_prep(a, seq, gq, gkv, wq, wqs, wk, wv, e2, cq, sq, tk, vone, tm=512):
    t = a.shape[0]
    tm = min(tm, seq)
    nper = seq // tm
    wa = MLA_Q_LORA + MLA_KV_LORA + LANES

    def const(shape):
        return pl.BlockSpec(shape, lambda i: (0, 0))

    def tab(width):
        return pl.BlockSpec((tm, width), lambda i: (i % nper, 0))

    out = jax.ShapeDtypeStruct((t, MLA_W), BF16)
    return pl.pallas_call(
        _mla_prep_kernel,
        grid=(t // tm,),
        in_specs=[pl.BlockSpec((tm, wa), lambda i: (i, 0)),
                  const((1, MLA_Q_LORA)), const((1, MLA_KV_LORA)),
                  const((MLA_Q_LORA, MLA_W)), const((MLA_Q_LORA, MLA_W)),
                  const((MLA_KV_LORA, MLA_W)), const((MLA_KV_LORA, MLA_W)),
                  const((LANES, MLA_W)),
                  tab(MLA_W), tab(MLA_W), tab(LANES), const((1, MLA_W))],
        out_specs=[pl.BlockSpec((tm, MLA_W), lambda i: (i, 0))] * 3,
        out_shape=[out, out, out],
        compiler_params=_params("parallel"),
        name="mla_prep",
    )(a, gq, gkv, wq, wqs, wk, wv, e2, cq, sq, tk, vone)


def _attn_kernel(q_ref, k_ref, v_ref, o_ref, *, sub):
    low = lax.broadcasted_iota(jnp.int32, (sub, HEAD_PAD), 1) < MLA_DV
    for r in range(q_ref.shape[1] // sub):
        rows = slice(r * sub, (r + 1) * sub)
        outs = []
        for hh in range(2):
            lanes = slice(hh * HEAD_PAD, (hh + 1) * HEAD_PAD)
            s = lax.dot_general(q_ref[0, rows, lanes], k_ref[0, :, lanes], (((1,), (1,)), ((), ())),
                                preferred_element_type=F32)
            m = jnp.max(s, -1, keepdims=True)
            p = jnp.exp2(s - m).astype(BF16)
            o = jnp.dot(p, v_ref[0, :, lanes], preferred_element_type=F32)
            outs.append(o / o[:, MLA_DV:MLA_DV + 1])
        o_ref[0, rows] = jnp.where(low, outs[0], pltpu.roll(outs[1], MLA_DV, 1)).astype(o_ref.dtype)


def _attention(q, k, v, tq=1024, sub=256):
    b, s, _ = q.shape
    tq = min(tq, s)
    pair = 2 * HEAD_PAD
    return pl.pallas_call(
        functools.partial(_attn_kernel, sub=min(sub, tq)),
        grid=(b, MLA_HEADS // 2, s // tq),
        in_specs=[pl.BlockSpec((1, tq, pair), lambda bi, h, i: (bi, i, h)),
                  pl.BlockSpec((1, s, pair), lambda bi, h, i: (bi, 0, h)),
                  pl.BlockSpec((1, s, pair), lambda bi, h, i: (bi, 0, h))],
        out_specs=pl.BlockSpec((1, tq, HEAD_PAD), lambda bi, h, i: (bi, i, h)),
        out_shape=jax.ShapeDtypeStruct((b, s, MLA_HEADS * MLA_DV), BF16),
        compiler_params=_params("parallel", "parallel", "parallel"),
        name="attention",
    )(q, k, v)


def _dwconv_kernel(*refs, width, glu, silu_out, seq, rows):
    if glu:
        a_ref, g_ref, w_ref, b_ref, o_ref, pad_ref = refs
        pre = a_ref[0] * jax.nn.sigmoid(g_ref[0])
    else:
        x_ref, w_ref, b_ref, o_ref, pad_ref = refs
        pre = x_ref[0]
    ch = o_ref.shape[-1]
    halo = jnp.zeros((CONV_HALO, ch), F32)
    pad_ref[0:CONV_HALO, :] = halo
    pad_ref[CONV_HALO + seq:2 * CONV_HALO + seq, :] = halo
    pad_ref[CONV_HALO:CONV_HALO + seq, :] = pre
    half = (width - 1) // 2
    win_rows = rows + 2 * CONV_HALO

    def body(c, carry):
        base = pl.multiple_of(c * rows, rows)
        acc = jnp.zeros((rows, ch), F32) + b_ref[...]
        for t in range(width):
            acc = acc + pad_ref[pl.ds(base + (CONV_HALO - half + t), rows), :] * w_ref[t:t + 1, :]
        if silu_out:
            acc = acc * jax.nn.sigmoid(acc)
        o_ref[0, pl.ds(base, rows), :] = acc
        return carry

    lax.fori_loop(0, seq // rows, body, 0)


def _dwconv(x, w, b, *, glu, silu_out, rows=64):
    bsz, seq, cin = x.shape
    width, ch = w.shape
    nct = ch // LANES
    kern = functools.partial(_dwconv_kernel, width=width, glu=glu, silu_out=silu_out, seq=seq,
                             rows=min(rows, seq))
    xspec = pl.BlockSpec((1, seq, LANES), lambda bi, j: (bi, 0, j))
    in_specs = [xspec]
    args = [x]
    if glu:
        in_specs.append(pl.BlockSpec((1, seq, LANES), lambda bi, j: (bi, 0, j + nct)))
        args.append(x)
    in_specs += [pl.BlockSpec((width, LANES), lambda bi, j: (0, j)),
                 pl.BlockSpec((1, LANES), lambda bi, j: (0, j))]
    return pl.pallas_call(
        kern,
        grid=(bsz, nct),
        in_specs=in_specs,
        out_specs=pl.BlockSpec((1, seq, LANES), lambda bi, j: (bi, 0, j)),
        out_shape=jax.ShapeDtypeStruct((bsz, seq, ch), F32),
        scratch_shapes=[pltpu.VMEM((seq + 2 * CONV_HALO, LANES), F32)],
        compiler_params=_params("parallel", "parallel"),
        name="dwconv",
    )(*args, w, b.reshape(1, ch))


def _softplus(x):
    return jnp.maximum(x, 0.0) + jnp.log1p(jnp.exp(-jnp.abs(x)))


def _ssd_kernel(*refs):
    f_in, b_in = refs[0:6], refs[6:12]
    yf_ref, yb_ref, hf_ref, hb_ref = refs[12:16]
    _ssd_chunk(*f_in, yf_ref, hf_ref, reverse=False)
    _ssd_chunk(*b_in, yb_ref, hb_ref, reverse=True)


def _ssd_chunk(xs_ref, bm_ref, cm_ref, dt_ref, dtb_ref, nega_ref, y_ref, h_ref, *, reverse):
    ln = SSM_CHUNK

    @pl.when(pl.program_id(1) == 0)
    def _():
        h_ref[...] = jnp.zeros_like(h_ref)

    dt = _softplus(dt_ref[0] + dtb_ref[...])
    a = dt * nega_ref[...]
    row = lax.broadcasted_iota(jnp.int32, (ln, ln), 0)
    col = lax.broadcasted_iota(jnp.int32, (ln, ln), 1)
    tri = (col >= row) if reverse else (col <= row)
    a_hi = a.astype(BF16)
    rem = a - a_hi.astype(F32)
    a_mid = rem.astype(BF16)
    a_lo = (rem - a_mid.astype(F32)).astype(BF16)
    tri_b = jnp.where(tri, 1.0, 0.0).astype(BF16)
    acs = jnp.dot(jnp.concatenate([tri_b, tri_b, tri_b], 1), jnp.concatenate([a_hi, a_mid, a_lo], 0),
                  preferred_element_type=F32)
    tot = acs[0:1] if reverse else acs[ln - 1:ln]
    wst = dt * jnp.exp(tot - acs)
    etot = jnp.exp(tot)
    acs_t = acs.T
    dt_t = dt.T
    wst_t = wst.T

    xs = xs_ref[0]
    even = lax.broadcasted_iota(jnp.int32, xs.shape, 1) % LANES < SSM_HEAD_DIM
    xs_even = jnp.where(even, xs, 0.0).astype(BF16)
    xs_odd = jnp.where(even, 0.0, xs).astype(BF16)
    cm_b = cm_ref[0].astype(BF16)
    bm_t = bm_ref[0].T
    bm_tb = bm_t.astype(BF16)
    y_off = jnp.dot(cm_b, h_ref[...].astype(BF16), preferred_element_type=F32)

    glane = lax.broadcasted_iota(jnp.int32, (ln, SSM_BC), 1) // SSM_STATE
    lo = lax.broadcasted_iota(jnp.int32, (ln, LANES), 1) < SSM_HEAD_DIM
    lo_1 = lax.broadcasted_iota(jnp.int32, (1, LANES), 1) < SSM_HEAD_DIM
    heads_per_group = SSM_HEADS // SSM_GROUPS
    for g in range(SSM_GROUPS):
        cmg = jnp.where(glane == g, cm_b, jnp.zeros_like(cm_b))
        cb = jnp.dot(cmg, bm_tb, preferred_element_type=F32)
        bm_tg = bm_t[g * SSM_STATE:(g + 1) * SSM_STATE, :]
        for j in range(heads_per_group // 2):
            pair = g * (heads_per_group // 2) + j
            lanes = slice(pair * LANES, (pair + 1) * LANES)
            x2 = jnp.concatenate([xs_even[:, lanes], xs_odd[:, lanes]], 0)
            mats, ecol, wsts, et = [], [], [], []
            for hh in range(2):
                h = 2 * pair + hh
                colb = jnp.broadcast_to(acs[:, h:h + 1], (ln, LANES))
                decay = jnp.exp(jnp.where(tri, colb - acs_t[h:h + 1, :], -jnp.inf))
                mats.append((cb * decay * dt_t[h:h + 1, :]).astype(BF16))
                ecol.append(jnp.exp(colb))
                wsts.append((bm_tg * wst_t[h:h + 1, :]).astype(BF16))
                et.append(jnp.broadcast_to(etot[:, h:h + 1], (1, LANES)))
            diag = jnp.dot(jnp.concatenate(mats, 1), x2, preferred_element_type=F32)
            st = jnp.dot(jnp.concatenate(wsts, 1), x2, preferred_element_type=F32)
            y_ref[0, :, lanes] = diag + y_off[:, lanes] * jnp.where(lo, ecol[0], ecol[1])
            rows = slice(g * SSM_STATE, (g + 1) * SSM_STATE)
            h_ref[rows, lanes] = h_ref[rows, lanes] * jnp.where(lo_1, et[0], et[1]) + st


def _ssd(xbc, a_grp, dt_bias, neg_a):
    bsz, seq, _ = xbc.shape
    nc = seq // SSM_CHUNK

    def direction(reverse):
        dt_tile = 7 if reverse else 6
        row = 1 if reverse else 0

        def cidx(c):
            return (nc - 1 - c) if reverse else c

        ins = [pl.BlockSpec((1, SSM_CHUNK, SSM_D_INNER), lambda b, c: (b, cidx(c), 0)),
               pl.BlockSpec((1, SSM_CHUNK, SSM_BC), lambda b, c: (b, cidx(c), SSM_D_INNER // SSM_BC)),
               pl.BlockSpec((1, SSM_CHUNK, SSM_BC), lambda b, c: (b, cidx(c), SSM_D_INNER // SSM_BC + 1)),
               pl.BlockSpec((1, SSM_CHUNK, LANES), lambda b, c: (b, cidx(c), dt_tile)),
               pl.BlockSpec((1, LANES), lambda b, c: (0, 0)),
               pl.BlockSpec((1, LANES), lambda b, c: (0, 0))]
        out = pl.BlockSpec((1, SSM_CHUNK, SSM_D_INNER), lambda b, c: (b, cidx(c), 0))
        return ins, out, (xbc, xbc, xbc, a_grp, dt_bias[row:row + 1], neg_a[row:row + 1])

    f_ins, f_out, f_args = direction(False)
    b_ins, b_out, b_args = direction(True)
    y = jax.ShapeDtypeStruct((bsz, seq, SSM_D_INNER), F32)
    return pl.pallas_call(
        _ssd_kernel,
        grid=(bsz, nc),
        in_specs=f_ins + b_ins,
        out_specs=[f_out, b_out],
        out_shape=[y, y],
        scratch_shapes=[pltpu.VMEM((SSM_BC, SSM_D_INNER), F32), pltpu.VMEM((SSM_BC, SSM_D_INNER), F32)],
        compiler_params=_params("parallel", "arbitrary"),
        name="ssd",
    )(*f_args, *b_args)


def _layernorm(x, g, b, eps=1e-5):
    mu = jnp.mean(x, -1, keepdims=True)
    xc = x - mu
    var = jnp.mean(xc * xc, -1, keepdims=True)
    return xc * lax.rsqrt(var + eps) * g + b


def _merge_kernel(x_ref, o_ref, yf_ref, yb_ref, xs_ref, z_ref, u_ref, gl_ref,
                  wa_ref, ws_ref, wc_ref, wo_ref, dsk_ref, ng_ref, cg_ref, cb_ref, bc_ref,
                  l1g_ref, l1b_ref, rw_ref, rb_ref,
                  x1_ref, x1g_ref, route_ref, cnt_ref, carry_ref):
    tm = x_ref.shape[0]

    @pl.when(pl.program_id(0) == 0)
    def _():
        carry_ref[...] = jnp.zeros_like(carry_ref)

    y_attn = jnp.dot(o_ref[...], wa_ref[...], preferred_element_type=F32)
    z = z_ref[...].astype(F32)
    ys = (yf_ref[...] + yb_ref[...] + xs_ref[...] * dsk_ref[...]) * (z * jax.nn.sigmoid(z))
    gw = SSM_D_INNER // SSM_GROUPS
    ys = jnp.concatenate(
        [_rms(ys[:, g * gw:(g + 1) * gw], ng_ref[:, g * gw:(g + 1) * gw]) for g in range(SSM_GROUPS)], -1)
    y_ssm = jnp.dot(ys.astype(BF16), ws_ref[...], preferred_element_type=F32)
    uc = _layernorm(u_ref[...], cg_ref[...], cb_ref[...])
    uc = uc * jax.nn.sigmoid(uc)
    y_conv = jnp.dot(uc.astype(BF16), wc_ref[...], preferred_element_type=F32) + bc_ref[...]
    gl = gl_ref[...].astype(F32)
    mixed = (jax.nn.sigmoid(gl[:, :D_MODEL]) * y_attn
             + jax.nn.sigmoid(gl[:, D_MODEL:2 * D_MODEL]) * y_ssm
             + jax.nn.sigmoid(gl[:, 2 * D_MODEL:]) * y_conv)
    mixed = jnp.dot(mixed.astype(BF16), wo_ref[...], preferred_element_type=F32)
    x1 = _layernorm(DN_ALPHA * x_ref[...] + mixed, l1g_ref[...], l1b_ref[...])
    x1_ref[...] = x1
    for t in range(ROW_TILES):
        x1g_ref[pl.ds(t, tm, stride=ROW_TILES), :] = x1[:, t * LANES:(t + 1) * LANES]

    x_hi = x1.astype(BF16)
    x_lo = (x1 - x_hi.astype(F32)).astype(BF16)
    lg = jnp.dot(jnp.concatenate([x_hi, x_lo, x_hi], -1), rw_ref[...], preferred_element_type=F32) + rb_ref[...]
    lane = lax.broadcasted_iota(jnp.int32, (tm, LANES), 1).astype(F32)
    sels, vals, idxs = [], [], []
    for _ in range(TOP_K):
        m = jnp.max(lg, -1, keepdims=True)
        idx = jnp.min(jnp.where(lg == m, lane, float(LANES)), -1, keepdims=True)
        sel = lane == idx
        lg = jnp.where(sel, NEG_BIG * 2, lg)
        sels.append(sel)
        vals.append(m)
        idxs.append(idx)
    es = [jnp.exp(v - vals[0]) for v in vals]
    den = es[0] + es[1] + es[2] + es[3]
    hot = jnp.zeros((tm, LANES), F32)
    for sel in sels:
        hot = hot + sel.astype(F32)
    r = lax.broadcasted_iota(jnp.int32, (tm, tm), 0)
    c = lax.broadcasted_iota(jnp.int32, (tm, tm), 1)
    excl = jnp.dot((c < r).astype(BF16), hot.astype(BF16), preferred_element_type=F32) + carry_ref[...]
    route = jnp.zeros((tm, LANES), F32)
    for kk in range(TOP_K):
        rank = jnp.sum(jnp.where(sels[kk], excl, 0.0), -1, keepdims=True)
        route = jnp.where(lane == float(kk), idxs[kk], route)
        route = jnp.where(lane == float(TOP_K + kk), rank, route)
        route = jnp.where(lane == float(2 * TOP_K + kk), es[kk] / den, route)
    route_ref[...] = route
    carry_ref[...] = carry_ref[...] + jnp.sum(hot, 0, keepdims=True)
    cnt_ref[...] = carry_ref[...]


def _merge(x, o, yf, yb, xbc, z, u, gl, wa, ws, wc, wo, dsk, ng, cg, cb, bc, l1g, l1b, rw, rb, tm=256):
    t = x.shape[0]
    tm = min(tm, t)

    def rowb(width, col=0):
        return pl.BlockSpec((tm, width), lambda i: (i, col))

    def const(shape):
        return pl.BlockSpec(shape, lambda i: (0, 0))

    return pl.pallas_call(
        _merge_kernel,
        grid=(t // tm,),
        in_specs=[rowb(D_MODEL), rowb(MLA_HEADS * MLA_DV), rowb(SSM_D_INNER), rowb(SSM_D_INNER), rowb(SSM_D_INNER),
                  rowb(SSM_D_INNER), rowb(CNV_CH), rowb(3 * D_MODEL),
                  const((MLA_HEADS * MLA_DV, D_MODEL)), const((SSM_D_INNER, D_MODEL)), const((CNV_CH, D_MODEL)),
                  const((D_MODEL, D_MODEL)), const((1, SSM_D_INNER)), const((1, SSM_D_INNER)),
                  const((1, CNV_CH)), const((1, CNV_CH)), const((1, D_MODEL)),
                  const((1, D_MODEL)), const((1, D_MODEL)), const((3 * D_MODEL, LANES)), const((1, LANES))],
        out_specs=[rowb(D_MODEL),
                   pl.BlockSpec((tm * ROW_TILES, LANES), lambda i: (i, 0)),
                   rowb(LANES), const((1, LANES))],
        out_shape=[jax.ShapeDtypeStruct((t, D_MODEL), F32),
                   jax.ShapeDtypeStruct((t * ROW_TILES, LANES), F32),
                   jax.ShapeDtypeStruct((t, LANES), F32), jax.ShapeDtypeStruct((1, LANES), F32)],
        scratch_shapes=[pltpu.VMEM((1, LANES), F32)],
        compiler_params=_params("arbitrary"),
        name="merge",
    )(x, o, yf, yb, xbc, z, u, gl, wa, ws, wc, wo, dsk, ng, cg, cb, bc, l1g, l1b, rw, rb)


def _row_slice(ref, row):
    return ref.at[pl.ds(pl.multiple_of(row * ROW_TILES, ROW_TILES), ROW_TILES)]


def _from_row_tiles(ref, first, rows, pitch):
    return jnp.concatenate([ref[pl.ds(first + t, rows, stride=pitch), :] for t in range(ROW_TILES)], -1)


def _fill_padding(fill_start_ref, fill_n_ref, zero_ref, xs_hbm, sem):
    zero_ref[...] = jnp.zeros_like(zero_ref)
    bits = [1 << b for b in reversed(range((MOE_ROWS - 1).bit_length()))]

    def copies(e):
        n_e = fill_n_ref[e]
        for bit in bits:
            done = n_e & ~(2 * bit - 1)
            yield (n_e & bit) != 0, pltpu.make_async_copy(
                zero_ref.at[pl.ds(0, bit * ROW_TILES)],
                xs_hbm.at[pl.ds(pl.multiple_of((fill_start_ref[e] + done) * ROW_TILES, ROW_TILES),
                                bit * ROW_TILES)], sem)

    def start(e, carry):
        for cond, cp in copies(e):
            pl.when(cond)(cp.start)
        return carry

    def wait(e, carry):
        for cond, cp in copies(e):
            pl.when(cond)(cp.wait)
        return carry

    def tail(j):
        first = (fill_start_ref[N_EXPERTS] * ROW_TILES + j * zero_ref.shape[0])
        return pltpu.make_async_copy(zero_ref, xs_hbm.at[pl.ds(pl.multiple_of(first, ROW_TILES),
                                                               zero_ref.shape[0])], sem)

    def tail_start(j, carry):
        tail(j).start()
        return carry

    def tail_wait(j, carry):
        tail(j).wait()
        return carry

    lax.fori_loop(0, N_EXPERTS, start, 0)
    lax.fori_loop(0, fill_n_ref[N_EXPERTS], tail_start, 0)
    lax.fori_loop(0, N_EXPERTS, wait, 0)
    lax.fori_loop(0, fill_n_ref[N_EXPERTS], tail_wait, 0)


def _dispatch_kernel(fill_start_ref, fill_n_ref, dest_ref, x_ref, xs_hbm, sem, fill_sem, zero_ref, *, tm):
    @pl.when(pl.program_id(0) == pl.num_programs(0) - 1)
    def _():
        _fill_padding(fill_start_ref, fill_n_ref, zero_ref, xs_hbm, fill_sem)

    def body(c, carry):
        for u in range(ISSUE_UNROLL):
            t = c * ISSUE_UNROLL + u
            src = _row_slice(x_ref, t)
            for kk in range(TOP_K):
                pltpu.make_async_copy(src, _row_slice(xs_hbm, dest_ref[0, 0, t * TOP_K + kk]),
                                      sem).start(priority=kk % 2)
        return carry

    lax.fori_loop(0, tm // ISSUE_UNROLL, body, 0)
    for _ in range(TOP_K):
        pltpu.make_async_copy(x_ref, xs_hbm.at[pl.ds(0, tm * ROW_TILES)], sem).wait()


def _dispatch(dest, fill_start, fill_n, x1g, n_rows, tm=512):
    t = x1g.shape[0] // ROW_TILES
    tm = min(tm, t)
    nt = t // tm
    grid_spec = pltpu.PrefetchScalarGridSpec(
        num_scalar_prefetch=2,
        grid=(nt,),
        in_specs=[pl.BlockSpec((1, 1, tm * TOP_K), lambda i, fs, fn: (i, 0, 0), memory_space=pltpu.SMEM),
                  pl.BlockSpec((tm * ROW_TILES, LANES), lambda i, fs, fn: (i, 0))],
        out_specs=pl.BlockSpec(memory_space=pl.ANY),
        scratch_shapes=[pltpu.SemaphoreType.DMA(()), pltpu.SemaphoreType.DMA(()),
                        pltpu.VMEM((MOE_ROWS // 2 * ROW_TILES, LANES), F32)],
    )
    return pl.pallas_call(
        functools.partial(_dispatch_kernel, tm=tm),
        grid_spec=grid_spec,
        out_shape=jax.ShapeDtypeStruct((n_rows * ROW_TILES, LANES), F32),
        compiler_params=_params("arbitrary"),
        name="dispatch",
    )(fill_start, fill_n, dest.reshape(nt, 1, tm * TOP_K), x1g)


def _moe_kernel(blk_e_ref, nused_ref, x_ref, wgu_ref, bgu_ref, wdn_ref, bdn_ref, o_ref, wgu_b, wdn_b):
    i = pl.program_id(0)

    @pl.when((i == 0) | (blk_e_ref[i] != blk_e_ref[jnp.maximum(i - 1, 0)]))
    def _():
        wgu_b[...] = wgu_ref[0, 0].astype(BF16)
        wdn_b[...] = wdn_ref[0, 0].astype(BF16)

    @pl.when(i < nused_ref[0])
    def _():
        xb = _from_row_tiles(x_ref, 0, MOE_ROWS, ROW_TILES).astype(BF16)
        h = jnp.dot(xb, wgu_b[...], preferred_element_type=F32) + bgu_ref[0]
        gate = jnp.minimum(h[:, :D_FF], SWIGLU_LIMIT)
        up = jnp.clip(h[:, D_FF:], -SWIGLU_LIMIT, SWIGLU_LIMIT)
        act = (up + 1.0) * (gate * jax.nn.sigmoid(SWIGLU_ALPHA * gate))
        y = jnp.dot(act.astype(BF16), wdn_b[...], preferred_element_type=F32) + bdn_ref[0]
        for t in range(ROW_TILES):
            o_ref[pl.ds(t, MOE_ROWS, stride=ROW_TILES), :] = y[:, t * LANES:(t + 1) * LANES]

    @pl.when(i >= nused_ref[0])
    def _():
        o_ref[...] = jnp.zeros_like(o_ref)


def _moe_experts(blk_e, nused, xs, wgu, bgu, wdn, bdn, layer):
    n_blocks = blk_e.shape[0]
    grid_spec = pltpu.PrefetchScalarGridSpec(
        num_scalar_prefetch=2,
        grid=(n_blocks,),
        in_specs=[pl.BlockSpec((MOE_ROWS * ROW_TILES, LANES), lambda i, be, nu: (jnp.minimum(i, nu[0] - 1), 0)),
                  pl.BlockSpec((1, 1, D_MODEL, 2 * D_FF), lambda i, be, nu: (layer, be[i], 0, 0)),
                  pl.BlockSpec((1, 1, 2 * D_FF), lambda i, be, nu: (be[i], 0, 0)),
                  pl.BlockSpec((1, 1, D_FF, D_MODEL), lambda i, be, nu: (layer, be[i], 0, 0)),
                  pl.BlockSpec((1, 1, D_MODEL), lambda i, be, nu: (be[i], 0, 0))],
        out_specs=pl.BlockSpec((MOE_ROWS * ROW_TILES, LANES), lambda i, be, nu: (i, 0)),
        scratch_shapes=[pltpu.VMEM((D_MODEL, 2 * D_FF), BF16), pltpu.VMEM((D_FF, D_MODEL), BF16)],
    )
    return pl.pallas_call(
        _moe_kernel,
        grid_spec=grid_spec,
        out_shape=jax.ShapeDtypeStruct(xs.shape, F32),
        compiler_params=_params("arbitrary"),
        name="moe_experts",
    )(blk_e, nused, xs, wgu, bgu, wdn, bdn)


def _gather_topk_rows(dest_ref, yb_hbm, dst, sem, tm):
    def body(c, carry):
        for u in range(ISSUE_UNROLL):
            t = c * ISSUE_UNROLL + u
            for kk in range(TOP_K):
                pltpu.make_async_copy(_row_slice(yb_hbm, dest_ref[0, 0, t * TOP_K + kk]),
                                      _row_slice(dst, kk * tm + t), sem).start(priority=kk % 2)
        return carry
    lax.fori_loop(0, tm // ISSUE_UNROLL, body, 0)


def _combine_kernel(dest_first_ref, dest_next_ref, yb_hbm, x1_ref, route_ref, g_ref, b_ref, x2_ref, x2b_ref,
                    buf, sem, *, tm):
    i = pl.program_id(0)
    n = pl.num_programs(0)
    slot = i % 2

    @pl.when(i == 0)
    def _():
        _gather_topk_rows(dest_first_ref, yb_hbm, buf.at[0], sem.at[0], tm)

    @pl.when(i + 1 < n)
    def _():
        _gather_topk_rows(dest_next_ref, yb_hbm, buf.at[1 - slot], sem.at[1 - slot], tm)

    pltpu.make_async_copy(yb_hbm.at[pl.ds(0, tm * TOP_K * ROW_TILES)], buf.at[slot], sem.at[slot]).wait()
    route = route_ref[...]
    ffn = jnp.zeros((tm, D_MODEL), F32)
    for kk in range(TOP_K):
        rows = _from_row_tiles(buf.at[slot], kk * tm * ROW_TILES, tm, ROW_TILES)
        ffn = ffn + rows * route[:, 2 * TOP_K + kk:2 * TOP_K + kk + 1]
    x2 = _layernorm(DN_ALPHA * x1_ref[...] + ffn, g_ref[...], b_ref[...])
    x2_ref[...] = x2
    x2b_ref[...] = x2.astype(BF16)


def _combine(dest, yb, x1, route, g, b, tm=512):
    t = x1.shape[0]
    tm = min(tm, t)
    nt = t // tm
    dest3 = dest.reshape(nt, 1, tm * TOP_K)
    return pl.pallas_call(
        functools.partial(_combine_kernel, tm=tm),
        grid=(nt,),
        in_specs=[pl.BlockSpec((1, 1, tm * TOP_K), lambda i: (0, 0, 0), memory_space=pltpu.SMEM),
                  pl.BlockSpec((1, 1, tm * TOP_K), lambda i: (jnp.minimum(i + 1, nt - 1), 0, 0),
                               memory_space=pltpu.SMEM),
                  pl.BlockSpec(memory_space=pl.ANY),
                  pl.BlockSpec((tm, D_MODEL), lambda i: (i, 0)),
                  pl.BlockSpec((tm, LANES), lambda i: (i, 0)),
                  pl.BlockSpec((1, D_MODEL), lambda i: (0, 0)),
                  pl.BlockSpec((1, D_MODEL), lambda i: (0, 0))],
        out_specs=[pl.BlockSpec((tm, D_MODEL), lambda i: (i, 0)),
                   pl.BlockSpec((tm, D_MODEL), lambda i: (i, 0))],
        out_shape=[jax.ShapeDtypeStruct((t, D_MODEL), F32), jax.ShapeDtypeStruct((t, D_MODEL), BF16)],
        scratch_shapes=[pltpu.VMEM((2, tm * TOP_K * ROW_TILES, LANES), F32),
                        pltpu.SemaphoreType.DMA((2,))],
        compiler_params=_params("arbitrary"),
        name="combine",
    )(dest3, dest3, yb, x1, route, g, b)


def _pad_cols(w, width):
    return jnp.pad(w, ((0, 0), (0, width - w.shape[1])))


def _pack_in_proj(w_in, b_in):
    offs = np.concatenate([[0], np.cumsum(IN_SIZES)])
    wb = jnp.concatenate([w_in, b_in[None, :]], 0)

    def piece(i):
        return wb[:, offs[i]:offs[i + 1]]

    kr = piece(2)
    kr_swapped = jnp.concatenate([kr[:, MLA_DR // 2:], kr[:, :MLA_DR // 2]], 1)
    dt = piece(5)
    grp_a = jnp.concatenate([piece(0), piece(1), _pad_cols(jnp.concatenate([kr, kr_swapped], 1), LANES),
                             _pad_cols(dt[:, :SSM_HEADS], LANES), _pad_cols(dt[:, SSM_HEADS:], LANES)], 1)
    groups = {"a": (grp_a, F32), "z": (piece(3), BF16), "xbc": (piece(4), F32),
              "cnv": (jnp.concatenate([piece(6), piece(7)], 1), F32), "gate": (piece(8), BF16)}
    return {k: (v[:-1].astype(BF16), v[-1:], dt_) for k, (v, dt_) in groups.items()}


def _pack_mla(w_uq, w_ukv, w_br_attn, seq):
    hq = MLA_DN + MLA_DR
    half = MLA_DR // 2
    wq = w_uq.reshape(MLA_Q_LORA, MLA_HEADS, hq)
    zq = jnp.zeros((MLA_Q_LORA, MLA_HEADS, HEAD_PAD - hq), F32)
    wq_main = jnp.concatenate([wq, zq], -1).reshape(MLA_Q_LORA, MLA_W)
    wq_swap = jnp.concatenate([jnp.zeros((MLA_Q_LORA, MLA_HEADS, MLA_DN), F32),
                               wq[..., MLA_DN + half:], wq[..., MLA_DN:MLA_DN + half], zq],
                              -1).reshape(MLA_Q_LORA, MLA_W)
    wkv = w_ukv.reshape(MLA_KV_LORA, MLA_HEADS, MLA_DN + MLA_DV)
    zk = jnp.zeros((MLA_KV_LORA, MLA_HEADS, HEAD_PAD - MLA_DN), F32)
    wk = jnp.concatenate([wkv[..., :MLA_DN], zk], -1).reshape(MLA_KV_LORA, MLA_W)
    wv = jnp.concatenate([wkv[..., MLA_DN:], zk], -1).reshape(MLA_KV_LORA, MLA_W)
    e2 = np.zeros((LANES, MLA_HEADS, HEAD_PAD), np.float32)
    for j in range(MLA_DR):
        e2[j, :, MLA_DN + j] = 1.0
        e2[MLA_DR + j, :, MLA_DN + j] = 1.0
    vone = np.zeros((MLA_HEADS, HEAD_PAD), np.float32)
    vone[:, MLA_DV] = 1.0
    pos = jnp.arange(seq, dtype=F32)
    inv = ROPE_THETA ** (-jnp.arange(0, MLA_DR, 2, dtype=F32) / MLA_DR)
    ang = pos[:, None] * inv[None, :]
    cos, sin = jnp.cos(ang), jnp.sin(ang)
    scale = (MLA_DN + MLA_DR) ** -0.5 * math.log2(math.e)
    ones = jnp.ones((seq, MLA_DN), F32)
    zpad = jnp.zeros((seq, HEAD_PAD - hq), F32)
    cq = jnp.tile(jnp.concatenate([ones, cos, cos, zpad], 1) * scale, (1, MLA_HEADS))
    sq = jnp.tile(jnp.concatenate([0 * ones, -sin, sin, zpad], 1) * scale, (1, MLA_HEADS))
    tk = jnp.concatenate([cos, cos, -sin, sin, jnp.zeros((seq, LANES - 2 * MLA_DR), F32)], 1)
    return dict(wq=wq_main.astype(BF16), wqs=wq_swap.astype(BF16), wk=wk.astype(BF16), wv=wv.astype(BF16),
                e2=jnp.asarray(e2.reshape(LANES, MLA_W), BF16), vone=jnp.asarray(vone.reshape(1, MLA_W)),
                wbr=w_br_attn.astype(BF16), cq=cq, sq=sq, tk=tk)


def _route_tables(route, cnt, n_tok):
    idx = route[:, :TOP_K].astype(jnp.int32)
    rank = route[:, TOP_K:2 * TOP_K].astype(jnp.int32)
    counts = cnt[0, :N_EXPERTS].astype(jnp.int32)
    n_blocks = -(-(n_tok * TOP_K + N_EXPERTS * (MOE_ROWS - 1)) // MOE_ROWS)
    padded = (counts + MOE_ROWS - 1) // MOE_ROWS * MOE_ROWS
    pad_end = jnp.cumsum(padded)
    pad_start = pad_end - padded
    onehot = idx[..., None] == jnp.arange(N_EXPERTS, dtype=jnp.int32)
    dest = (jnp.sum(jnp.where(onehot, pad_start, 0), -1) + rank).reshape(-1)
    blk_first = jnp.arange(n_blocks, dtype=jnp.int32) * MOE_ROWS
    blk_e = jnp.minimum(jnp.sum((pad_end[None, :] <= blk_first[:, None]).astype(jnp.int32), -1), N_EXPERTS - 1)
    nused = pad_end[-1:] // MOE_ROWS
    n_rows = n_blocks * MOE_ROWS
    fill_start = jnp.concatenate([pad_start + counts, pad_end[-1:]])
    fill_n = jnp.concatenate([padded - counts, (n_rows - pad_end[-1:]) // (MOE_ROWS // 2)])
    return dest, blk_e, nused, fill_start, fill_n, n_rows


def kernel(x, w_in, b_in, mla_q_norm, mla_kv_norm, mla_w_uq, mla_w_ukv, w_br_attn, ssm_conv_w, ssm_conv_b, ssm_dt_bias, ssm_a_log, ssm_d, ssm_norm, w_br_ssm, cnv_dw_w, cnv_dw_b, cnv_ln_g, cnv_ln_b, w_br_conv, b_br_conv, w_out, ln1_g, ln1_b, router_w, router_b, moe_w_gate_up, moe_b_gate_up, moe_w_down, moe_b_down, ln2_g, ln2_b):
    bsz, seq, d = x.shape
    n_tok = bsz * seq
    xf = x.reshape(n_tok, d)
    xb = xf.astype(BF16)
    for l in range(DEPTH):
        proj = _pack_in_proj(w_in[l], b_in[l])
        mla = _pack_mla(mla_w_uq[l], mla_w_ukv[l], w_br_attn[l], seq)
        a_grp = _linear(xb, *proj["a"])
        z = _linear(xb, *proj["z"])
        xbc_raw = _linear(xb, *proj["xbc"], tn=768)
        cnv_raw = _linear(xb, *proj["cnv"])
        gate_logits = _linear(xb, *proj["gate"], tn=1024)

        q, k, v = _mla_prep(a_grp, seq, mla_q_norm[l][None], mla_kv_norm[l][None], mla["wq"], mla["wqs"],
                            mla["wk"], mla["wv"], mla["e2"], mla["cq"], mla["sq"], mla["tk"], mla["vone"])
        attn = _attention(q.reshape(bsz, seq, MLA_W), k.reshape(bsz, seq, MLA_W), v.reshape(bsz, seq, MLA_W))

        xbc = _dwconv(xbc_raw.reshape(bsz, seq, SSM_CONV_DIM), ssm_conv_w[l], ssm_conv_b[l],
                      glu=False, silu_out=True, rows=256)
        a3 = a_grp.reshape(bsz, seq, GROUP_A)
        dtb = jnp.pad(ssm_dt_bias[l], ((0, 0), (0, LANES - SSM_HEADS)))
        nega = jnp.pad(-jnp.exp(ssm_a_log[l]), ((0, 0), (0, LANES - SSM_HEADS)))
        y_fwd, y_bwd = _ssd(xbc, a3, dtb, nega)

        u = _dwconv(cnv_raw.reshape(bsz, seq, 2 * CNV_CH), cnv_dw_w[l], cnv_dw_b[l], glu=True, silu_out=False,
                    rows=128)

        rw = jnp.pad(router_w[l], ((0, 0), (0, LANES - N_EXPERTS)))
        rw_hi = rw.astype(BF16)
        rw_lo = (rw - rw_hi.astype(F32)).astype(BF16)
        rw = jnp.concatenate([rw_hi, rw_hi, rw_lo], 0)
        rb = jnp.pad(router_b[l], (0, LANES - N_EXPERTS), constant_values=NEG_BIG)[None]
        x1, x1g, route, cnt = _merge(
            xf, attn.reshape(n_tok, MLA_HEADS * MLA_DV), y_fwd.reshape(n_tok, SSM_D_INNER), y_bwd.reshape(n_tok, SSM_D_INNER),
            xbc.reshape(n_tok, SSM_CONV_DIM), z, u.reshape(n_tok, CNV_CH), gate_logits,
            mla["wbr"], w_br_ssm[l].astype(BF16), w_br_conv[l].astype(BF16), w_out[l].astype(BF16),
            jnp.repeat(ssm_d[l], SSM_HEAD_DIM)[None], ssm_norm[l][None], cnv_ln_g[l][None], cnv_ln_b[l][None],
            b_br_conv[l][None], ln1_g[l][None], ln1_b[l][None], rw, rb)

        dest, blk_e, nused, fill_start, fill_n, n_rows = _route_tables(route, cnt, n_tok)
        xs = _dispatch(dest, fill_start, fill_n, x1g, n_rows)
        yb = _moe_experts(blk_e, nused, xs, moe_w_gate_up, moe_b_gate_up[l][:, None, :], moe_w_down,
                          moe_b_down[l][:, None, :], l)
        xf, xb = _combine(dest, yb, x1, route, ln2_g[l][None], ln2_b[l][None])
    return xf.reshape(bsz, seq, d)
```

```python
import functools
import math

import numpy as np
import jax
import jax.numpy as jnp
from jax import lax
from jax.experimental import pallas as pl
from jax.experimental.pallas import tpu as pltpu

F32 = jnp.float32
BF16 = jnp.bfloat16
HIGHEST = lax.Precision.HIGHEST

LANES = 128
SUBLANES = 8
VMEM_LIMIT_BYTES = 56 * 1024 * 1024

D_MODEL = 1024
DEPTH = 2
MLA_HEADS = 8
MLA_Q_LORA = 384
MLA_KV_LORA = 256
MLA_DN = 64
MLA_DR = 32
MLA_DV = 64
ROPE_THETA = 10000.0
SSM_HEADS = 16
SSM_HEAD_DIM = 64
SSM_D_INNER = SSM_HEADS * SSM_HEAD_DIM
SSM_GROUPS = 4
SSM_STATE = 64
SSM_CONV = 5
SSM_CHUNK = 128
SSM_BC = SSM_GROUPS * SSM_STATE
SSM_CONV_DIM = SSM_D_INNER + 2 * SSM_BC
CNV_CH = 512
CNV_WIDTH = 31
N_EXPERTS = 32
TOP_K = 4
D_FF = 1024
SWIGLU_LIMIT = 7.0
SWIGLU_ALPHA = 1.702
DN_ALPHA = (2 * DEPTH) ** 0.25
IN_SIZES = (MLA_Q_LORA, MLA_KV_LORA, MLA_DR, SSM_D_INNER, SSM_CONV_DIM, 2 * SSM_HEADS,
            CNV_CH, CNV_CH, 3 * D_MODEL)
HEAD_PAD = LANES
MLA_W = MLA_HEADS * HEAD_PAD
GROUP_A = 1024
CONV_HALO = 16
MOE_ROWS = 512
ISSUE_UNROLL = 8
ROW_TILES = D_MODEL // LANES
NEG_BIG = -1e30


def _params(*sem):
    return pltpu.CompilerParams(dimension_semantics=sem, vmem_limit_bytes=VMEM_LIMIT_BYTES)


def _linear_kernel(x_ref, w_ref, b_ref, o_ref):
    acc = jnp.dot(x_ref[...], w_ref[...], preferred_element_type=F32)
    o_ref[...] = (acc + b_ref[...]).astype(o_ref.dtype)


def _linear(x, w, b, out_dtype, tm=1024, tn=None):
    m, k = x.shape
    n = w.shape[1]
    tm = min(tm, m)
    tn = n if tn is None else tn
    return pl.pallas_call(
        _linear_kernel,
        grid=(n // tn, m // tm),
        in_specs=[pl.BlockSpec((tm, k), lambda j, i: (i, 0)),
                  pl.BlockSpec((k, tn), lambda j, i: (0, j)),
                  pl.BlockSpec((1, tn), lambda j, i: (0, j))],
        out_specs=pl.BlockSpec((tm, tn), lambda j, i: (i, j)),
        out_shape=jax.ShapeDtypeStruct((m, n), out_dtype),
        compiler_params=_params("parallel", "parallel"),
        name="linear",
    )(x, w, b)


def _rms(x, g, eps=1e-6):
    return x * lax.rsqrt(jnp.mean(x * x, -1, keepdims=True) + eps) * g


def _mla_prep_kernel(a_ref, gq_ref, gkv_ref, wq_ref, wqs_ref, wk_ref, wv_ref, e2_ref,
                     cq_ref, sq_ref, tk_ref, vone_ref, q_ref, k_ref, v_ref):
    a = a_ref[...]
    c_q = a[:, :MLA_Q_LORA]
    c_kv = a[:, MLA_Q_LORA:MLA_Q_LORA + MLA_KV_LORA]
    kr = a[:, MLA_Q_LORA + MLA_KV_LORA:]
    qn = _rms(c_q, gq_ref[...]).astype(BF16)
    kvn = _rms(c_kv, gkv_ref[...]).astype(BF16)
    q = (jnp.dot(qn, wq_ref[...], preferred_element_type=F32) * cq_ref[...]
         + jnp.dot(qn, wqs_ref[...], preferred_element_type=F32) * sq_ref[...])
    q_ref[...] = q.astype(BF16)
    krp = (kr * tk_ref[...]).astype(BF16)
    k = (jnp.dot(kvn, wk_ref[...], preferred_element_type=F32)
         + jnp.dot(krp, e2_ref[...], preferred_element_type=F32))
    k_ref[...] = k.astype(BF16)
    v = jnp.dot(kvn, wv_ref[...], preferred_element_type=F32) + vone_ref[...]
    v_ref[...] = v.astype(BF16)


def _mla_prep(a, seq, gq, gkv, wq, wqs, wk, wv, e2, cq, sq, tk, vone, tm=512):
    t = a.shape[0]
    tm = min(tm, seq)
    nper = seq // tm
    wa = MLA_Q_LORA + MLA_KV_LORA + LANES

    def const(shape):
        return pl.BlockSpec(shape, lambda i: (0, 0))

    def tab(width):
        return pl.BlockSpec((tm, width), lambda i: (i % nper, 0))

    out = jax.ShapeDtypeStruct((t, MLA_W), BF16)
    return pl.pallas_call(
        _mla_prep_kernel,
        grid=(t // tm,),
        in_specs=[pl.BlockSpec((tm, wa), lambda i: (i, 0)),
                  const((1, MLA_Q_LORA)), const((1, MLA_KV_LORA)),
                  const((MLA_Q_LORA, MLA_W)), const((MLA_Q_LORA, MLA_W)),
                  const((MLA_KV_LORA, MLA_W)), const((MLA_KV_LORA, MLA_W)),
                  const((LANES, MLA_W)),
                  tab(MLA_W), tab(MLA_W), tab(LANES), const((1, MLA_W))],
        out_specs=[pl.BlockSpec((tm, MLA_W), lambda i: (i, 0))] * 3,
        out_shape=[out, out, out],
        compiler_params=_params("parallel"),
        name="mla_prep",
    )(a, gq, gkv, wq, wqs, wk, wv, e2, cq, sq, tk, vone)


def _attn_kernel(q_ref, k_ref, v_ref, o_ref, *, sub):
    low = lax.broadcasted_iota(jnp.int32, (sub, HEAD_PAD), 1) < MLA_DV
    for r in range(q_ref.shape[1] // sub):
        rows = slice(r * sub, (r + 1) * sub)
        outs = []
        for hh in range(2):
            lanes = slice(hh * HEAD_PAD, (hh + 1) * HEAD_PAD)
            s = lax.dot_general(q_ref[0, rows, lanes], k_ref[0, :, lanes], (((1,), (1,)), ((), ())),
                                preferred_element_type=F32)
            m = jnp.max(s, -1, keepdims=True)
            p = jnp.exp2(s - m).astype(BF16)
            o = jnp.dot(p, v_ref[0, :, lanes], preferred_element_type=F32)
            outs.append(o / o[:, MLA_DV:MLA_DV + 1])
        o_ref[0, rows] = jnp.where(low, outs[0], pltpu.roll(outs[1], MLA_DV, 1)).astype(o_ref.dtype)


def _attention(q, k, v, tq=1024, sub=256):
    b, s, _ = q.shape
    tq = min(tq, s)
    pair = 2 * HEAD_PAD
    return pl.pallas_call(
        functools.partial(_attn_kernel, sub=min(sub, tq)),
        grid=(b, MLA_HEADS // 2, s // tq),
        in_specs=[pl.BlockSpec((1, tq, pair), lambda bi, h, i: (bi, i, h)),
                  pl.BlockSpec((1, s, pair), lambda bi, h, i: (bi, 0, h)),
                  pl.BlockSpec((1, s, pair), lambda bi, h, i: (bi, 0, h))],
        out_specs=pl.BlockSpec((1, tq, HEAD_PAD), lambda bi, h, i: (bi, i, h)),
        out_shape=jax.ShapeDtypeStruct((b, s, MLA_HEADS * MLA_DV), BF16),
        compiler_params=_params("parallel", "parallel", "parallel"),
        name="attention",
    )(q, k, v)


def _dwconv_kernel(*refs, width, glu, silu_out, seq, rows):
    if glu:
        a_ref, g_ref, w_ref, b_ref, o_ref, pad_ref = refs
        pre = a_ref[0] * jax.nn.sigmoid(g_ref[0])
    else:
        x_ref, w_ref, b_ref, o_ref, pad_ref = refs
        pre = x_ref[0]
    ch = o_ref.shape[-1]
    halo = jnp.zeros((CONV_HALO, ch), F32)
    pad_ref[0:CONV_HALO, :] = halo
    pad_ref[CONV_HALO + seq:2 * CONV_HALO + seq, :] = halo
    pad_ref[CONV_HALO:CONV_HALO + seq, :] = pre
    half = (width - 1) // 2
    win_rows = rows + 2 * CONV_HALO

    def body(c, carry):
        base = pl.multiple_of(c * rows, rows)
        acc = jnp.zeros((rows, ch), F32) + b_ref[...]
        for t in range(width):
            acc = acc + pad_ref[pl.ds(base + (CONV_HALO - half + t), rows), :] * w_ref[t:t + 1, :]
        if silu_out:
            acc = acc * jax.nn.sigmoid(acc)
        o_ref[0, pl.ds(base, rows), :] = acc
        return carry

    lax.fori_loop(0, seq // rows, body, 0)


def _dwconv(x, w, b, *, glu, silu_out, rows=64):
    bsz, seq, cin = x.shape
    width, ch = w.shape
    nct = ch // LANES
    kern = functools.partial(_dwconv_kernel, width=width, glu=glu, silu_out=silu_out, seq=seq,
                             rows=min(rows, seq))
    xspec = pl.BlockSpec((1, seq, LANES), lambda bi, j: (bi, 0, j))
    in_specs = [xspec]
    args = [x]
    if glu:
        in_specs.append(pl.BlockSpec((1, seq, LANES), lambda bi, j: (bi, 0, j + nct)))
        args.append(x)
    in_specs += [pl.BlockSpec((width, LANES), lambda bi, j: (0, j)),
                 pl.BlockSpec((1, LANES), lambda bi, j: (0, j))]
    return pl.pallas_call(
        kern,
        grid=(bsz, nct),
        in_specs=in_specs,
        out_specs=pl.BlockSpec((1, seq, LANES), lambda bi, j: (bi, 0, j)),
        out_shape=jax.ShapeDtypeStruct((bsz, seq, ch), F32),
        scratch_shapes=[pltpu.VMEM((seq + 2 * CONV_HALO, LANES), F32)],
        compiler_params=_params("parallel", "parallel"),
        name="dwconv",
    )(*args, w, b.reshape(1, ch))


def _softplus(x):
    return jnp.maximum(x, 0.0) + jnp.log1p(jnp.exp(-jnp.abs(x)))


def _ssd_kernel(*refs, nb):
    n_in = 8
    ins = (refs[0:n_in], refs[n_in:2 * n_in])
    ys = refs[2 * n_in:2 * n_in + 2]
    scratch = refs[2 * n_in + 2:]
    n_s = len(scratch) // (2 * nb)
    chains = [(d, bi, scratch[(d * nb + bi) * n_s:(d * nb + bi + 1) * n_s]) for bi in range(nb) for d in range(2)]

    @pl.when(pl.program_id(1) == 0)
    def _():
        for _, _, sc in chains:
            sc[0][...] = jnp.zeros_like(sc[0])

    etots = [_ssd_prep(bi, *ins[d], *sc, reverse=bool(d)) for d, bi, sc in chains]
    for g in range(SSM_GROUPS):
        gens = [_ssd_group(g, etot, bi, *ins[d], ys[d], *sc) for etot, (d, bi, sc) in zip(etots, chains)]
        for _ in zip(*gens):
            pass


def _ssd_prep(bi, xs_ref, bm_ref, cm_ref, dt_ref, dtb_ref, nega_ref, cum_ref, mask_ref,
              h_ref, acs_s, acst_s, dtt_s, wstt_s, bmt_s, bmtb_s, *, reverse):
    ln = SSM_CHUNK
    dt = _softplus(dt_ref[bi] + dtb_ref[...])
    a = dt * nega_ref[...]
    a_hi = a.astype(BF16)
    rem = a - a_hi.astype(F32)
    a_mid = rem.astype(BF16)
    a_lo = (rem - a_mid.astype(F32)).astype(BF16)
    acs = jnp.dot(cum_ref[...], jnp.concatenate([a_hi, a_mid, a_lo], 0), preferred_element_type=F32)
    tot = acs[0:1] if reverse else acs[ln - 1:ln]
    etot = jnp.exp(tot)
    acs_s[...] = acs
    acst_s[...] = acs.T
    dtt_s[...] = dt.T
    wstt_s[...] = (dt * jnp.exp(tot - acs)).T
    bm_t = bm_ref[bi].T
    bmt_s[...] = bm_t
    bmtb_s[...] = bm_t.astype(BF16)
    return etot


def _ssd_group(g, etot, bi, xs_ref, bm_ref, cm_ref, dt_ref, dtb_ref, nega_ref, cum_ref, mask_ref, y_ref,
               h_ref, acs_s, acst_s, dtt_s, wstt_s, bmt_s, bmtb_s):
    ln = SSM_CHUNK
    glane = lax.broadcasted_iota(jnp.int32, (ln, SSM_BC), 1) // SSM_STATE
    lo = lax.broadcasted_iota(jnp.int32, (ln, LANES), 1) < SSM_HEAD_DIM
    lo_1 = lax.broadcasted_iota(jnp.int32, (1, LANES), 1) < SSM_HEAD_DIM
    heads_per_group = SSM_HEADS // SSM_GROUPS
    group_w = heads_per_group * SSM_HEAD_DIM
    cm_b = cm_ref[bi].astype(BF16)
    cb = jnp.dot(jnp.where(glane == g, cm_b, jnp.zeros_like(cm_b)), bmtb_s[...],
                 preferred_element_type=F32)
    glanes = slice(g * group_w, (g + 1) * group_w)
    y_off = jnp.dot(cm_b, h_ref[:, glanes].astype(BF16), preferred_element_type=F32)
    rows = slice(g * SSM_STATE, (g + 1) * SSM_STATE)
    for j in range(heads_per_group // 2):
        pair = g * (heads_per_group // 2) + j
        lanes = slice(pair * LANES, (pair + 1) * LANES)
        x = xs_ref[bi, :, lanes]
        x2 = jnp.concatenate([jnp.where(lo, x, 0.0).astype(BF16), jnp.where(lo, 0.0, x).astype(BF16)], 0)
        mats, ecol, wsts, et = [], [], [], []
        for hh in range(2):
            h = 2 * pair + hh
            colb = jnp.broadcast_to(acs_s[:, h:h + 1], (ln, LANES))
            decay = jnp.exp(colb - acst_s[h:h + 1, :] + mask_ref[...])
            mats.append((cb * decay * dtt_s[h:h + 1, :]).astype(BF16))
            ecol.append(jnp.exp(colb))
            wsts.append((bmt_s[rows, :] * wstt_s[h:h + 1, :]).astype(BF16))
            et.append(jnp.broadcast_to(etot[:, h:h + 1], (1, LANES)))
        diag = jnp.dot(jnp.concatenate(mats, 1), x2, preferred_element_type=F32)
        st = jnp.dot(jnp.concatenate(wsts, 1), x2, preferred_element_type=F32)
        y_ref[bi, :, lanes] = diag + y_off[:, j * LANES:(j + 1) * LANES] * jnp.where(lo, ecol[0], ecol[1])
        h_ref[rows, lanes] = h_ref[rows, lanes] * jnp.where(lo_1, et[0], et[1]) + st
        yield


def _ssd(xbc, a_grp, dt_bias, neg_a, nb=2):
    bsz, seq, _ = xbc.shape
    nc = seq // SSM_CHUNK
    nb = nb if bsz % nb == 0 else 1

    def direction(reverse):
        dt_tile = 7 if reverse else 6
        row = 1 if reverse else 0

        def cidx(c):
            return (nc - 1 - c) if reverse else c

        ins = [pl.BlockSpec((nb, SSM_CHUNK, SSM_D_INNER), lambda b, c: (b, cidx(c), 0)),
               pl.BlockSpec((nb, SSM_CHUNK, SSM_BC), lambda b, c: (b, cidx(c), SSM_D_INNER // SSM_BC)),
               pl.BlockSpec((nb, SSM_CHUNK, SSM_BC), lambda b, c: (b, cidx(c), SSM_D_INNER // SSM_BC + 1)),
               pl.BlockSpec((nb, SSM_CHUNK, LANES), lambda b, c: (b, cidx(c), dt_tile)),
               pl.BlockSpec((1, LANES), lambda b, c: (0, 0)),
               pl.BlockSpec((1, LANES), lambda b, c: (0, 0)),
               pl.BlockSpec((SSM_CHUNK, 3 * SSM_CHUNK), lambda b, c: (0, 0)),
               pl.BlockSpec((SSM_CHUNK, SSM_CHUNK), lambda b, c: (0, 0))]
        out = pl.BlockSpec((nb, SSM_CHUNK, SSM_D_INNER), lambda b, c: (b, cidx(c), 0))
        li = np.arange(SSM_CHUNK)
        keep = (li[None, :] >= li[:, None]) if reverse else (li[None, :] <= li[:, None])
        cum = jnp.asarray(np.tile(keep.astype(np.float32), (1, 3)), BF16)
        mask = jnp.asarray(np.where(keep, 0.0, -np.inf).astype(np.float32))
        return ins, out, (xbc, xbc, xbc, a_grp, dt_bias[row:row + 1], neg_a[row:row + 1], cum, mask)

    f_ins, f_out, f_args = direction(False)
    b_ins, b_out, b_args = direction(True)
    y = jax.ShapeDtypeStruct((bsz, seq, SSM_D_INNER), F32)
    per_chain = [pltpu.VMEM((SSM_BC, SSM_D_INNER), F32),
                 pltpu.VMEM((SSM_CHUNK, LANES), F32),
                 pltpu.VMEM((LANES, SSM_CHUNK), F32),
                 pltpu.VMEM((LANES, SSM_CHUNK), F32),
                 pltpu.VMEM((LANES, SSM_CHUNK), F32),
                 pltpu.VMEM((SSM_BC, SSM_CHUNK), F32),
                 pltpu.VMEM((SSM_BC, SSM_CHUNK), BF16)]
    return pl.pallas_call(
        functools.partial(_ssd_kernel, nb=nb),
        grid=(bsz // nb, nc),
        in_specs=f_ins + b_ins,
        out_specs=[f_out, b_out],
        out_shape=[y, y],
        scratch_shapes=per_chain * (2 * nb),
        compiler_params=_params("parallel", "arbitrary"),
        name="ssd",
    )(*f_args, *b_args)


def _layernorm(x, g, b, eps=1e-5):
    mu = jnp.mean(x, -1, keepdims=True)
    xc = x - mu
    var = jnp.mean(xc * xc, -1, keepdims=True)
    return xc * lax.rsqrt(var + eps) * g + b


def _merge_kernel(x_ref, o_ref, yf_ref, yb_ref, xs_ref, z_ref, u_ref, gl_ref,
                  wa_ref, ws_ref, wc_ref, wo_ref, dsk_ref, ng_ref, cg_ref, cb_ref, bc_ref,
                  l1g_ref, l1b_ref, rw_ref, rb_ref,
                  x1_ref, x1g_ref, route_ref, cnt_ref, carry_ref):
    tm = x_ref.shape[0]

    @pl.when(pl.program_id(0) == 0)
    def _():
        carry_ref[...] = jnp.zeros_like(carry_ref)

    y_attn = jnp.dot(o_ref[...], wa_ref[...], preferred_element_type=F32)
    z = z_ref[...].astype(F32)
    ys = (yf_ref[...] + yb_ref[...] + xs_ref[...] * dsk_ref[...]) * (z * jax.nn.sigmoid(z))
    gw = SSM_D_INNER // SSM_GROUPS
    ys = jnp.concatenate(
        [_rms(ys[:, g * gw:(g + 1) * gw], ng_ref[:, g * gw:(g + 1) * gw]) for g in range(SSM_GROUPS)], -1)
    y_ssm = jnp.dot(ys.astype(BF16), ws_ref[...], preferred_element_type=F32)
    uc = _layernorm(u_ref[...], cg_ref[...], cb_ref[...])
    uc = uc * jax.nn.sigmoid(uc)
    y_conv = jnp.dot(uc.astype(BF16), wc_ref[...], preferred_element_type=F32) + bc_ref[...]
    gl = gl_ref[...].astype(F32)
    mixed = (jax.nn.sigmoid(gl[:, :D_MODEL]) * y_attn
             + jax.nn.sigmoid(gl[:, D_MODEL:2 * D_MODEL]) * y_ssm
             + jax.nn.sigmoid(gl[:, 2 * D_MODEL:]) * y_conv)
    mixed = jnp.dot(mixed.astype(BF16), wo_ref[...], preferred_element_type=F32)
    x1 = _layernorm(DN_ALPHA * x_ref[...] + mixed, l1g_ref[...], l1b_ref[...])
    x1_ref[...] = x1
    for t in range(ROW_TILES):
        x1g_ref[pl.ds(t, tm, stride=ROW_TILES), :] = x1[:, t * LANES:(t + 1) * LANES]

    x_hi = x1.astype(BF16)
    x_lo = (x1 - x_hi.astype(F32)).astype(BF16)
    lg = jnp.dot(jnp.concatenate([x_hi, x_lo, x_hi], -1), rw_ref[...], preferred_element_type=F32) + rb_ref[...]
    lane = lax.broadcasted_iota(jnp.int32, (tm, LANES), 1).astype(F32)
    sels, vals, idxs = [], [], []
    for _ in range(TOP_K):
        m = jnp.max(lg, -1, keepdims=True)
        idx = jnp.min(jnp.where(lg == m, lane, float(LANES)), -1, keepdims=True)
        sel = lane == idx
        lg = jnp.where(sel, NEG_BIG * 2, lg)
        sels.append(sel)
        vals.append(m)
        idxs.append(idx)
    es = [jnp.exp(v - vals[0]) for v in vals]
    den = es[0] + es[1] + es[2] + es[3]
    hot = jnp.zeros((tm, LANES), F32)
    for sel in sels:
        hot = hot + sel.astype(F32)
    r = lax.broadcasted_iota(jnp.int32, (tm, tm), 0)
    c = lax.broadcasted_iota(jnp.int32, (tm, tm), 1)
    excl = jnp.dot((c < r).astype(BF16), hot.astype(BF16), preferred_element_type=F32) + carry_ref[...]
    route = jnp.zeros((tm, LANES), F32)
    for kk in range(TOP_K):
        rank = jnp.sum(jnp.where(sels[kk], excl, 0.0), -1, keepdims=True)
        route = jnp.where(lane == float(kk), idxs[kk], route)
        route = jnp.where(lane == float(TOP_K + kk), rank, route)
        route = jnp.where(lane == float(2 * TOP_K + kk), es[kk] / den, route)
    route_ref[...] = route
    carry_ref[...] = carry_ref[...] + jnp.sum(hot, 0, keepdims=True)
    cnt_ref[...] = carry_ref[...]


def _merge(x, o, yf, yb, xbc, z, u, gl, wa, ws, wc, wo, dsk, ng, cg, cb, bc, l1g, l1b, rw, rb, tm=256):
    t = x.shape[0]
    tm = min(tm, t)

    def rowb(width, col=0):
        return pl.BlockSpec((tm, width), lambda i: (i, col))

    def const(shape):
        return pl.BlockSpec(shape, lambda i: (0, 0))

    return pl.pallas_call(
        _merge_kernel,
        grid=(t // tm,),
        in_specs=[rowb(D_MODEL), rowb(MLA_HEADS * MLA_DV), rowb(SSM_D_INNER), rowb(SSM_D_INNER), rowb(SSM_D_INNER),
                  rowb(SSM_D_INNER), rowb(CNV_CH), rowb(3 * D_MODEL),
                  const((MLA_HEADS * MLA_DV, D_MODEL)), const((SSM_D_INNER, D_MODEL)), const((CNV_CH, D_MODEL)),
                  const((D_MODEL, D_MODEL)), const((1, SSM_D_INNER)), const((1, SSM_D_INNER)),
                  const((1, CNV_CH)), const((1, CNV_CH)), const((1, D_MODEL)),
                  const((1, D_MODEL)), const((1, D_MODEL)), const((3 * D_MODEL, LANES)), const((1, LANES))],
        out_specs=[rowb(D_MODEL),
                   pl.BlockSpec((tm * ROW_TILES, LANES), lambda i: (i, 0)),
                   rowb(LANES), const((1, LANES))],
        out_shape=[jax.ShapeDtypeStruct((t, D_MODEL), F32),
                   jax.ShapeDtypeStruct((t * ROW_TILES, LANES), F32),
                   jax.ShapeDtypeStruct((t, LANES), F32), jax.ShapeDtypeStruct((1, LANES), F32)],
        scratch_shapes=[pltpu.VMEM((1, LANES), F32)],
        compiler_params=_params("arbitrary"),
        name="merge",
    )(x, o, yf, yb, xbc, z, u, gl, wa, ws, wc, wo, dsk, ng, cg, cb, bc, l1g, l1b, rw, rb)


def _row_slice(ref, row):
    return ref.at[pl.ds(pl.multiple_of(row * ROW_TILES, ROW_TILES), ROW_TILES)]


def _from_row_tiles(ref, first, rows, pitch):
    return jnp.concatenate([ref[pl.ds(first + t, rows, stride=pitch), :] for t in range(ROW_TILES)], -1)


def _fill_padding(fill_start_ref, fill_n_ref, zero_ref, xs_hbm, sem):
    zero_ref[...] = jnp.zeros_like(zero_ref)
    bits = [1 << b for b in reversed(range((MOE_ROWS - 1).bit_length()))]

    def copies(e):
        n_e = fill_n_ref[e]
        for bit in bits:
            done = n_e & ~(2 * bit - 1)
            yield (n_e & bit) != 0, pltpu.make_async_copy(
                zero_ref.at[pl.ds(0, bit * ROW_TILES)],
                xs_hbm.at[pl.ds(pl.multiple_of((fill_start_ref[e] + done) * ROW_TILES, ROW_TILES),
                                bit * ROW_TILES)], sem)

    def start(e, carry):
        for cond, cp in copies(e):
            pl.when(cond)(cp.start)
        return carry

    def wait(e, carry):
        for cond, cp in copies(e):
            pl.when(cond)(cp.wait)
        return carry

    def tail(j):
        first = (fill_start_ref[N_EXPERTS] * ROW_TILES + j * zero_ref.shape[0])
        return pltpu.make_async_copy(zero_ref, xs_hbm.at[pl.ds(pl.multiple_of(first, ROW_TILES),
                                                               zero_ref.shape[0])], sem)

    def tail_start(j, carry):
        tail(j).start()
        return carry

    def tail_wait(j, carry):
        tail(j).wait()
        return carry

    lax.fori_loop(0, N_EXPERTS, start, 0)
    lax.fori_loop(0, fill_n_ref[N_EXPERTS], tail_start, 0)
    lax.fori_loop(0, N_EXPERTS, wait, 0)
    lax.fori_loop(0, fill_n_ref[N_EXPERTS], tail_wait, 0)


def _dispatch_kernel(fill_start_ref, fill_n_ref, dest_ref, x_ref, xs_hbm, sem, fill_sem, zero_ref, *, tm):
    @pl.when(pl.program_id(0) == pl.num_programs(0) - 1)
    def _():
        _fill_padding(fill_start_ref, fill_n_ref, zero_ref, xs_hbm, fill_sem)

    def body(c, carry):
        for u in range(ISSUE_UNROLL):
            t = c * ISSUE_UNROLL + u
            src = _row_slice(x_ref, t)
            for kk in range(TOP_K):
                pltpu.make_async_copy(src, _row_slice(xs_hbm, dest_ref[0, 0, t * TOP_K + kk]),
                                      sem).start(priority=kk % 2)
        return carry

    lax.fori_loop(0, tm // ISSUE_UNROLL, body, 0)
    for _ in range(TOP_K):
        pltpu.make_async_copy(x_ref, xs_hbm.at[pl.ds(0, tm * ROW_TILES)], sem).wait()


def _dispatch(dest, fill_start, fill_n, x1g, n_rows, tm=512):
    t = x1g.shape[0] // ROW_TILES
    tm = min(tm, t)
    nt = t // tm
    grid_spec = pltpu.PrefetchScalarGridSpec(
        num_scalar_prefetch=2,
        grid=(nt,),
        in_specs=[pl.BlockSpec((1, 1, tm * TOP_K), lambda i, fs, fn: (i, 0, 0), memory_space=pltpu.SMEM),
                  pl.BlockSpec((tm * ROW_TILES, LANES), lambda i, fs, fn: (i, 0))],
        out_specs=pl.BlockSpec(memory_space=pl.ANY),
        scratch_shapes=[pltpu.SemaphoreType.DMA(()), pltpu.SemaphoreType.DMA(()),
                        pltpu.VMEM((MOE_ROWS // 2 * ROW_TILES, LANES), F32)],
    )
    return pl.pallas_call(
        functools.partial(_dispatch_kernel, tm=tm),
        grid_spec=grid_spec,
        out_shape=jax.ShapeDtypeStruct((n_rows * ROW_TILES, LANES), F32),
        compiler_params=_params("arbitrary"),
        name="dispatch",
    )(fill_start, fill_n, dest.reshape(nt, 1, tm * TOP_K), x1g)


def _moe_kernel(blk_e_ref, nused_ref, x_ref, wgu_ref, bgu_ref, wdn_ref, bdn_ref, o_ref, wgu_b, wdn_b):
    i = pl.program_id(0)

    @pl.when((i == 0) | (blk_e_ref[i] != blk_e_ref[jnp.maximum(i - 1, 0)]))
    def _():
        wgu_b[...] = wgu_ref[0, 0].astype(BF16)
        wdn_b[...] = wdn_ref[0, 0].astype(BF16)

    @pl.when(i < nused_ref[0])
    def _():
        xb = _from_row_tiles(x_ref, 0, MOE_ROWS, ROW_TILES).astype(BF16)
        h = jnp.dot(xb, wgu_b[...], preferred_element_type=F32) + bgu_ref[0]
        gate = jnp.minimum(h[:, :D_FF], SWIGLU_LIMIT)
        up = jnp.clip(h[:, D_FF:], -SWIGLU_LIMIT, SWIGLU_LIMIT)
        act = (up + 1.0) * (gate * jax.nn.sigmoid(SWIGLU_ALPHA * gate))
        y = jnp.dot(act.astype(BF16), wdn_b[...], preferred_element_type=F32) + bdn_ref[0]
        for t in range(ROW_TILES):
            o_ref[pl.ds(t, MOE_ROWS, stride=ROW_TILES), :] = y[:, t * LANES:(t + 1) * LANES]

    @pl.when(i >= nused_ref[0])
    def _():
        o_ref[...] = jnp.zeros_like(o_ref)


def _moe_experts(blk_e, nused, xs, wgu, bgu, wdn, bdn, layer):
    n_blocks = blk_e.shape[0]
    grid_spec = pltpu.PrefetchScalarGridSpec(
        num_scalar_prefetch=2,
        grid=(n_blocks,),
        in_specs=[pl.BlockSpec((MOE_ROWS * ROW_TILES, LANES), lambda i, be, nu: (jnp.minimum(i, nu[0] - 1), 0)),
                  pl.BlockSpec((1, 1, D_MODEL, 2 * D_FF), lambda i, be, nu: (layer, be[i], 0, 0)),
                  pl.BlockSpec((1, 1, 2 * D_FF), lambda i, be, nu: (be[i], 0, 0)),
                  pl.BlockSpec((1, 1, D_FF, D_MODEL), lambda i, be, nu: (layer, be[i], 0, 0)),
                  pl.BlockSpec((1, 1, D_MODEL), lambda i, be, nu: (be[i], 0, 0))],
        out_specs=pl.BlockSpec((MOE_ROWS * ROW_TILES, LANES), lambda i, be, nu: (i, 0)),
        scratch_shapes=[pltpu.VMEM((D_MODEL, 2 * D_FF), BF16), pltpu.VMEM((D_FF, D_MODEL), BF16)],
    )
    return pl.pallas_call(
        _moe_kernel,
        grid_spec=grid_spec,
        out_shape=jax.ShapeDtypeStruct(xs.shape, F32),
        compiler_params=_params("arbitrary"),
        name="moe_experts",
    )(blk_e, nused, xs, wgu, bgu, wdn, bdn)


def _gather_topk_rows(dest_ref, yb_hbm, dst, sem, tm):
    def body(c, carry):
        for u in range(ISSUE_UNROLL):
            t = c * ISSUE_UNROLL + u
            for kk in range(TOP_K):
                pltpu.make_async_copy(_row_slice(yb_hbm, dest_ref[0, 0, t * TOP_K + kk]),
                                      _row_slice(dst, kk * tm + t), sem).start(priority=kk % 2)
        return carry
    lax.fori_loop(0, tm // ISSUE_UNROLL, body, 0)


def _combine_kernel(dest_first_ref, dest_next_ref, yb_hbm, x1_ref, route_ref, g_ref, b_ref, x2_ref, x2b_ref,
                    buf, sem, *, tm):
    i = pl.program_id(0)
    n = pl.num_programs(0)
    slot = i % 2

    @pl.when(i == 0)
    def _():
        _gather_topk_rows(dest_first_ref, yb_hbm, buf.at[0], sem.at[0], tm)

    @pl.when(i + 1 < n)
    def _():
        _gather_topk_rows(dest_next_ref, yb_hbm, buf.at[1 - slot], sem.at[1 - slot], tm)

    pltpu.make_async_copy(yb_hbm.at[pl.ds(0, tm * TOP_K * ROW_TILES)], buf.at[slot], sem.at[slot]).wait()
    route = route_ref[...]
    ffn = jnp.zeros((tm, D_MODEL), F32)
    for kk in range(TOP_K):
        rows = _from_row_tiles(buf.at[slot], kk * tm * ROW_TILES, tm, ROW_TILES)
        ffn = ffn + rows * route[:, 2 * TOP_K + kk:2 * TOP_K + kk + 1]
    x2 = _layernorm(DN_ALPHA * x1_ref[...] + ffn, g_ref[...], b_ref[...])
    x2_ref[...] = x2
    x2b_ref[...] = x2.astype(BF16)


def _combine(dest, yb, x1, route, g, b, tm=512):
    t = x1.shape[0]
    tm = min(tm, t)
    nt = t // tm
    dest3 = dest.reshape(nt, 1, tm * TOP_K)
    return pl.pallas_call(
        functools.partial(_combine_kernel, tm=tm),
        grid=(nt,),
        in_specs=[pl.BlockSpec((1, 1, tm * TOP_K), lambda i: (0, 0, 0), memory_space=pltpu.SMEM),
                  pl.BlockSpec((1, 1, tm * TOP_K), lambda i: (jnp.minimum(i + 1, nt - 1), 0, 0),
                               memory_space=pltpu.SMEM),
                  pl.BlockSpec(memory_space=pl.ANY),
                  pl.BlockSpec((tm, D_MODEL), lambda i: (i, 0)),
                  pl.BlockSpec((tm, LANES), lambda i: (i, 0)),
                  pl.BlockSpec((1, D_MODEL), lambda i: (0, 0)),
                  pl.BlockSpec((1, D_MODEL), lambda i: (0, 0))],
        out_specs=[pl.BlockSpec((tm, D_MODEL), lambda i: (i, 0)),
                   pl.BlockSpec((tm, D_MODEL), lambda i: (i, 0))],
        out_shape=[jax.ShapeDtypeStruct((t, D_MODEL), F32), jax.ShapeDtypeStruct((t, D_MODEL), BF16)],
        scratch_shapes=[pltpu.VMEM((2, tm * TOP_K * ROW_TILES, LANES), F32),
                        pltpu.SemaphoreType.DMA((2,))],
        compiler_params=_params("arbitrary"),
        name="combine",
    )(dest3, dest3, yb, x1, route, g, b)


def _pad_cols(w, width):
    return jnp.pad(w, ((0, 0), (0, width - w.shape[1])))


def _pack_in_proj(w_in, b_in):
    offs = np.concatenate([[0], np.cumsum(IN_SIZES)])
    wb = jnp.concatenate([w_in, b_in[None, :]], 0)

    def piece(i):
        return wb[:, offs[i]:offs[i + 1]]

    kr = piece(2)
    kr_swapped = jnp.concatenate([kr[:, MLA_DR // 2:], kr[:, :MLA_DR // 2]], 1)
    dt = piece(5)
    grp_a = jnp.concatenate([piece(0), piece(1), _pad_cols(jnp.concatenate([kr, kr_swapped], 1), LANES),
                             _pad_cols(dt[:, :SSM_HEADS], LANES), _pad_cols(dt[:, SSM_HEADS:], LANES)], 1)
    groups = {"a": (grp_a, F32), "z": (piece(3), BF16), "xbc": (piece(4), F32),
              "cnv": (jnp.concatenate([piece(6), piece(7)], 1), F32), "gate": (piece(8), BF16)}
    return {k: (v[:-1].astype(BF16), v[-1:], dt_) for k, (v, dt_) in groups.items()}


def _pack_mla(w_uq, w_ukv, w_br_attn, seq):
    hq = MLA_DN + MLA_DR
    half = MLA_DR // 2
    wq = w_uq.reshape(MLA_Q_LORA, MLA_HEADS, hq)
    zq = jnp.zeros((MLA_Q_LORA, MLA_HEADS, HEAD_PAD - hq), F32)
    wq_main = jnp.concatenate([wq, zq], -1).reshape(MLA_Q_LORA, MLA_W)
    wq_swap = jnp.concatenate([jnp.zeros((MLA_Q_LORA, MLA_HEADS, MLA_DN), F32),
                               wq[..., MLA_DN + half:], wq[..., MLA_DN:MLA_DN + half], zq],
                              -1).reshape(MLA_Q_LORA, MLA_W)
    wkv = w_ukv.reshape(MLA_KV_LORA, MLA_HEADS, MLA_DN + MLA_DV)
    zk = jnp.zeros((MLA_KV_LORA, MLA_HEADS, HEAD_PAD - MLA_DN), F32)
    wk = jnp.concatenate([wkv[..., :MLA_DN], zk], -1).reshape(MLA_KV_LORA, MLA_W)
    wv = jnp.concatenate([wkv[..., MLA_DN:], zk], -1).reshape(MLA_KV_LORA, MLA_W)
    e2 = np.zeros((LANES, MLA_HEADS, HEAD_PAD), np.float32)
    for j in range(MLA_DR):
        e2[j, :, MLA_DN + j] = 1.0
        e2[MLA_DR + j, :, MLA_DN + j] = 1.0
    vone = np.zeros((MLA_HEADS, HEAD_PAD), np.float32)
    vone[:, MLA_DV] = 1.0
    pos = jnp.arange(seq, dtype=F32)
    inv = ROPE_THETA ** (-jnp.arange(0, MLA_DR, 2, dtype=F32) / MLA_DR)
    ang = pos[:, None] * inv[None, :]
    cos, sin = jnp.cos(ang), jnp.sin(ang)
    scale = (MLA_DN + MLA_DR) ** -0.5 * math.log2(math.e)
    ones = jnp.ones((seq, MLA_DN), F32)
    zpad = jnp.zeros((seq, HEAD_PAD - hq), F32)
    cq = jnp.tile(jnp.concatenate([ones, cos, cos, zpad], 1) * scale, (1, MLA_HEADS))
    sq = jnp.tile(jnp.concatenate([0 * ones, -sin, sin, zpad], 1) * scale, (1, MLA_HEADS))
    tk = jnp.concatenate([cos, cos, -sin, sin, jnp.zeros((seq, LANES - 2 * MLA_DR), F32)], 1)
    return dict(wq=wq_main.astype(BF16), wqs=wq_swap.astype(BF16), wk=wk.astype(BF16), wv=wv.astype(BF16),
                e2=jnp.asarray(e2.reshape(LANES, MLA_W), BF16), vone=jnp.asarray(vone.reshape(1, MLA_W)),
                wbr=w_br_attn.astype(BF16), cq=cq, sq=sq, tk=tk)


def _route_tables(route, cnt, n_tok):
    idx = route[:, :TOP_K].astype(jnp.int32)
    rank = route[:, TOP_K:2 * TOP_K].astype(jnp.int32)
    counts = cnt[0, :N_EXPERTS].astype(jnp.int32)
    n_blocks = -(-(n_tok * TOP_K + N_EXPERTS * (MOE_ROWS - 1)) // MOE_ROWS)
    padded = (counts + MOE_ROWS - 1) // MOE_ROWS * MOE_ROWS
    pad_end = jnp.cumsum(padded)
    pad_start = pad_end - padded
    onehot = idx[..., None] == jnp.arange(N_EXPERTS, dtype=jnp.int32)
    dest = (jnp.sum(jnp.where(onehot, pad_start, 0), -1) + rank).reshape(-1)
    blk_first = jnp.arange(n_blocks, dtype=jnp.int32) * MOE_ROWS
    blk_e = jnp.minimum(jnp.sum((pad_end[None, :] <= blk_first[:, None]).astype(jnp.int32), -1), N_EXPERTS - 1)
    nused = pad_end[-1:] // MOE_ROWS
    n_rows = n_blocks * MOE_ROWS
    fill_start = jnp.concatenate([pad_start + counts, pad_end[-1:]])
    fill_n = jnp.concatenate([padded - counts, (n_rows - pad_end[-1:]) // (MOE_ROWS // 2)])
    return dest, blk_e, nused, fill_start, fill_n, n_rows


def kernel(x, w_in, b_in, mla_q_norm, mla_kv_norm, mla_w_uq, mla_w_ukv, w_br_attn, ssm_conv_w, ssm_conv_b, ssm_dt_bias, ssm_a_log, ssm_d, ssm_norm, w_br_ssm, cnv_dw_w, cnv_dw_b, cnv_ln_g, cnv_ln_b, w_br_conv, b_br_conv, w_out, ln1_g, ln1_b, router_w, router_b, moe_w_gate_up, moe_b_gate_up, moe_w_down, moe_b_down, ln2_g, ln2_b):
    bsz, seq, d = x.shape
    n_tok = bsz * seq
    xf = x.reshape(n_tok, d)
    xb = xf.astype(BF16)
    for l in range(DEPTH):
        proj = _pack_in_proj(w_in[l], b_in[l])
        mla = _pack_mla(mla_w_uq[l], mla_w_ukv[l], w_br_attn[l], seq)
        a_grp = _linear(xb, *proj["a"])
        z = _linear(xb, *proj["z"])
        xbc_raw = _linear(xb, *proj["xbc"], tn=768)
        cnv_raw = _linear(xb, *proj["cnv"])
        gate_logits = _linear(xb, *proj["gate"], tn=1024)

        q, k, v = _mla_prep(a_grp, seq, mla_q_norm[l][None], mla_kv_norm[l][None], mla["wq"], mla["wqs"],
                            mla["wk"], mla["wv"], mla["e2"], mla["cq"], mla["sq"], mla["tk"], mla["vone"])
        attn = _attention(q.reshape(bsz, seq, MLA_W), k.reshape(bsz, seq, MLA_W), v.reshape(bsz, seq, MLA_W))

        xbc = _dwconv(xbc_raw.reshape(bsz, seq, SSM_CONV_DIM), ssm_conv_w[l], ssm_conv_b[l],
                      glu=False, silu_out=True, rows=256)
        a3 = a_grp.reshape(bsz, seq, GROUP_A)
        dtb = jnp.pad(ssm_dt_bias[l], ((0, 0), (0, LANES - SSM_HEADS)))
        nega = jnp.pad(-jnp.exp(ssm_a_log[l]), ((0, 0), (0, LANES - SSM_HEADS)))
        y_fwd, y_bwd = _ssd(xbc, a3, dtb, nega)

        u = _dwconv(cnv_raw.reshape(bsz, seq, 2 * CNV_CH), cnv_dw_w[l], cnv_dw_b[l], glu=True, silu_out=False,
                    rows=128)

        rw = jnp.pad(router_w[l], ((0, 0), (0, LANES - N_EXPERTS)))
        rw_hi = rw.astype(BF16)
        rw_lo = (rw - rw_hi.astype(F32)).astype(BF16)
        rw = jnp.concatenate([rw_hi, rw_hi, rw_lo], 0)
        rb = jnp.pad(router_b[l], (0, LANES - N_EXPERTS), constant_values=NEG_BIG)[None]
        x1, x1g, route, cnt = _merge(
            xf, attn.reshape(n_tok, MLA_HEADS * MLA_DV), y_fwd.reshape(n_tok, SSM_D_INNER), y_bwd.reshape(n_tok, SSM_D_INNER),
            xbc.reshape(n_tok, SSM_CONV_DIM), z, u.reshape(n_tok, CNV_CH), gate_logits,
            mla["wbr"], w_br_ssm[l].astype(BF16), w_br_conv[l].astype(BF16), w_out[l].astype(BF16),
            jnp.repeat(ssm_d[l], SSM_HEAD_DIM)[None], ssm_norm[l][None], cnv_ln_g[l][None], cnv_ln_b[l][None],
            b_br_conv[l][None], ln1_g[l][None], ln1_b[l][None], rw, rb)

        dest, blk_e, nused, fill_start, fill_n, n_rows = _route_tables(route, cnt, n_tok)
        xs = _dispatch(dest, fill_start, fill_n, x1g, n_rows)
        yb = _moe_experts(blk_e, nused, xs, moe_w_gate_up, moe_b_gate_up[l][:, None, :], moe_w_down,
                          moe_b_down[l][:, None, :], l)
        xf, xb = _combine(dest, yb, x1, route, ln2_g[l][None], ln2_b[l][None])
    return xf.reshape(bsz, seq, d)
```

```python
import functools
import math

import numpy as np
import jax
import jax.numpy as jnp
from jax import lax
from jax.experimental import pallas as pl
from jax.experimental.pallas import tpu as pltpu

F32 = jnp.float32
BF16 = jnp.bfloat16
HIGHEST = lax.Precision.HIGHEST

LANES = 128
SUBLANES = 8
VMEM_LIMIT_BYTES = 56 * 1024 * 1024

D_MODEL = 1024
DEPTH = 2
MLA_HEADS = 8
MLA_Q_LORA = 384
MLA_KV_LORA = 256
MLA_DN = 64
MLA_DR = 32
MLA_DV = 64
ROPE_THETA = 10000.0
SSM_HEADS = 16
SSM_HEAD_DIM = 64
SSM_D_INNER = SSM_HEADS * SSM_HEAD_DIM
SSM_GROUPS = 4
SSM_STATE = 64
SSM_CONV = 5
SSM_CHUNK = 128
SSM_BC = SSM_GROUPS * SSM_STATE
SSM_CONV_DIM = SSM_D_INNER + 2 * SSM_BC
CNV_CH = 512
CNV_WIDTH = 31
N_EXPERTS = 32
TOP_K = 4
D_FF = 1024
SWIGLU_LIMIT = 7.0
SWIGLU_ALPHA = 1.702
DN_ALPHA = (2 * DEPTH) ** 0.25
IN_SIZES = (MLA_Q_LORA, MLA_KV_LORA, MLA_DR, SSM_D_INNER, SSM_CONV_DIM, 2 * SSM_HEADS,
            CNV_CH, CNV_CH, 3 * D_MODEL)
HEAD_PAD = LANES
MLA_W = MLA_HEADS * HEAD_PAD
GROUP_A = 1024
CONV_HALO = 16
MOE_ROWS = 512
ISSUE_UNROLL = 8
ROW_TILES = D_MODEL // LANES
NEG_BIG = -1e30


def _params(*sem):
    return pltpu.CompilerParams(dimension_semantics=sem, vmem_limit_bytes=VMEM_LIMIT_BYTES)


def _sigmoid(x):
    return 0.5 * jnp.tanh(0.5 * x) + 0.5


def _linear_kernel(x_ref, w_ref, b_ref, o_ref):
    acc = jnp.dot(x_ref[...], w_ref[...], preferred_element_type=F32)
    o_ref[...] = (acc + b_ref[...]).astype(o_ref.dtype)


def _linear(x, w, b, out_dtype, tm=1024, tn=None):
    m, k = x.shape
    n = w.shape[1]
    tm = min(tm, m)
    tn = n if tn is None else tn
    return pl.pallas_call(
        _linear_kernel,
        grid=(n // tn, m // tm),
        in_specs=[pl.BlockSpec((tm, k), lambda j, i: (i, 0)),
                  pl.BlockSpec((k, tn), lambda j, i: (0, j)),
                  pl.BlockSpec((1, tn), lambda j, i: (0, j))],
        out_specs=pl.BlockSpec((tm, tn), lambda j, i: (i, j)),
        out_shape=jax.ShapeDtypeStruct((m, n), out_dtype),
        compiler_params=_params("parallel", "parallel"),
        name="linear",
    )(x, w, b)


def _rms(x, g, eps=1e-6):
    return x * lax.rsqrt(jnp.mean(x * x, -1, keepdims=True) + eps) * g


def _mla_prep_kernel(a_ref, gq_ref, gkv_ref, wq_ref, wqs_ref, wk_ref, wv_ref, e2_ref,
                     cq_ref, sq_ref, tk_ref, vone_ref, q_ref, k_ref, v_ref):
    a = a_ref[...]
    c_q = a[:, :MLA_Q_LORA]
    c_kv = a[:, MLA_Q_LORA:MLA_Q_LORA + MLA_KV_LORA]
    kr = a[:, MLA_Q_LORA + MLA_KV_LORA:]
    qn = _rms(c_q, gq_ref[...]).astype(BF16)
    kvn = _rms(c_kv, gkv_ref[...]).astype(BF16)
    q = (jnp.dot(qn, wq_ref[...], preferred_element_type=F32) * cq_ref[...]
         + jnp.dot(qn, wqs_ref[...], preferred_element_type=F32) * sq_ref[...])
    q_ref[...] = q.astype(BF16)
    krp = (kr * tk_ref[...]).astype(BF16)
    k = (jnp.dot(kvn, wk_ref[...], preferred_element_type=F32)
         + jnp.dot(krp, e2_ref[...], preferred_element_type=F32))
    k_ref[...] = k.astype(BF16)
    v = jnp.dot(kvn, wv_ref[...], preferred_element_type=F32) + vone_ref[...]
    v_ref[...] = v.astype(BF16)


def _mla_prep(a, seq, gq, gkv, wq, wqs, wk, wv, e2, cq, sq, tk, vone, tm=512):
    t = a.shape[0]
    tm = min(tm, seq)
    nper = seq // tm
    wa = MLA_Q_LORA + MLA_KV_LORA + LANES

    def const(shape):
        return pl.BlockSpec(shape, lambda i: (0, 0))

    def tab(width):
        return pl.BlockSpec((tm, width), lambda i: (i % nper, 0))

    out = jax.ShapeDtypeStruct((t, MLA_W), BF16)
    return pl.pallas_call(
        _mla_prep_kernel,
        grid=(t // tm,),
        in_specs=[pl.BlockSpec((tm, wa), lambda i: (i, 0)),
                  const((1, MLA_Q_LORA)), const((1, MLA_KV_LORA)),
                  const((MLA_Q_LORA, MLA_W)), const((MLA_Q_LORA, MLA_W)),
                  const((MLA_KV_LORA, MLA_W)), const((MLA_KV_LORA, MLA_W)),
                  const((LANES, MLA_W)),
                  tab(MLA_W), tab(MLA_W), tab(LANES), const((1, MLA_W))],
        out_specs=[pl.BlockSpec((tm, MLA_W), lambda i: (i, 0))] * 3,
        out_shape=[out, out, out],
        compiler_params=_params("parallel"),
        name="mla_prep",
    )(a, gq, gkv, wq, wqs, wk, wv, e2, cq, sq, tk, vone)


def _attn_kernel(q_ref, k_ref, v_ref, o_ref, *, sub):
    low = lax.broadcasted_iota(jnp.int32, (sub, HEAD_PAD), 1) < MLA_DV
    for r in range(q_ref.shape[1] // sub):
        rows = slice(r * sub, (r + 1) * sub)
        outs = []
        for hh in range(2):
            lanes = slice(hh * HEAD_PAD, (hh + 1) * HEAD_PAD)
            s = lax.dot_general(q_ref[0, rows, lanes], k_ref[0, :, lanes], (((1,), (1,)), ((), ())),
                                preferred_element_type=F32)
            m = jnp.max(s, -1, keepdims=True)
            p = jnp.exp2(s - m).astype(BF16)
            o = jnp.dot(p, v_ref[0, :, lanes], preferred_element_type=F32)
            outs.append(o / o[:, MLA_DV:MLA_DV + 1])
        o_ref[0, rows] = jnp.where(low, outs[0], pltpu.roll(outs[1], MLA_DV, 1)).astype(o_ref.dtype)


def _attention(q, k, v, tq=1024, sub=256):
    b, s, _ = q.shape
    tq = min(tq, s)
    pair = 2 * HEAD_PAD
    return pl.pallas_call(
        functools.partial(_attn_kernel, sub=min(sub, tq)),
        grid=(b, MLA_HEADS // 2, s // tq),
        in_specs=[pl.BlockSpec((1, tq, pair), lambda bi, h, i: (bi, i, h)),
                  pl.BlockSpec((1, s, pair), lambda bi, h, i: (bi, 0, h)),
                  pl.BlockSpec((1, s, pair), lambda bi, h, i: (bi, 0, h))],
        out_specs=pl.BlockSpec((1, tq, HEAD_PAD), lambda bi, h, i: (bi, i, h)),
        out_shape=jax.ShapeDtypeStruct((b, s, MLA_HEADS * MLA_DV), BF16),
        compiler_params=_params("parallel", "parallel", "parallel"),
        name="attention",
    )(q, k, v)


def _dwconv_kernel(*refs, width, glu, silu_out, seq, rows):
    if glu:
        a_ref, g_ref, w_ref, b_ref, o_ref, pad_ref = refs
        pre = a_ref[0] * _sigmoid(g_ref[0])
    else:
        x_ref, w_ref, b_ref, o_ref, pad_ref = refs
        pre = x_ref[0]
    ch = o_ref.shape[-1]
    halo = jnp.zeros((CONV_HALO, ch), F32)
    pad_ref[0:CONV_HALO, :] = halo
    pad_ref[CONV_HALO + seq:2 * CONV_HALO + seq, :] = halo
    pad_ref[CONV_HALO:CONV_HALO + seq, :] = pre
    half = (width - 1) // 2
    win_rows = rows + 2 * CONV_HALO

    def body(c, carry):
        base = pl.multiple_of(c * rows, rows)
        acc = jnp.zeros((rows, ch), F32) + b_ref[...]
        for t in range(width):
            acc = acc + pad_ref[pl.ds(base + (CONV_HALO - half + t), rows), :] * w_ref[t:t + 1, :]
        if silu_out:
            acc = acc * _sigmoid(acc)
        o_ref[0, pl.ds(base, rows), :] = acc
        return carry

    lax.fori_loop(0, seq // rows, body, 0)


def _dwconv(x, w, b, *, glu, silu_out, rows=64):
    bsz, seq, cin = x.shape
    width, ch = w.shape
    nct = ch // LANES
    kern = functools.partial(_dwconv_kernel, width=width, glu=glu, silu_out=silu_out, seq=seq,
                             rows=min(rows, seq))
    xspec = pl.BlockSpec((1, seq, LANES), lambda bi, j: (bi, 0, j))
    in_specs = [xspec]
    args = [x]
    if glu:
        in_specs.append(pl.BlockSpec((1, seq, LANES), lambda bi, j: (bi, 0, j + nct)))
        args.append(x)
    in_specs += [pl.BlockSpec((width, LANES), lambda bi, j: (0, j)),
                 pl.BlockSpec((1, LANES), lambda bi, j: (0, j))]
    return pl.pallas_call(
        kern,
        grid=(bsz, nct),
        in_specs=in_specs,
        out_specs=pl.BlockSpec((1, seq, LANES), lambda bi, j: (bi, 0, j)),
        out_shape=jax.ShapeDtypeStruct((bsz, seq, ch), F32),
        scratch_shapes=[pltpu.VMEM((seq + 2 * CONV_HALO, LANES), F32)],
        compiler_params=_params("parallel", "parallel"),
        name="dwconv",
    )(*args, w, b.reshape(1, ch))


def _softplus(x):
    return jnp.maximum(x, 0.0) + jnp.log1p(jnp.exp(-jnp.abs(x)))


def _ssd_kernel(*refs, nb):
    n_in = 8
    ins = (refs[0:n_in], refs[n_in:2 * n_in])
    ys = refs[2 * n_in:2 * n_in + 2]
    scratch = refs[2 * n_in + 2:]
    n_s = len(scratch) // (2 * nb)
    chains = [(d, bi, scratch[(d * nb + bi) * n_s:(d * nb + bi + 1) * n_s]) for bi in range(nb) for d in range(2)]

    @pl.when(pl.program_id(1) == 0)
    def _():
        for _, _, sc in chains:
            sc[0][...] = jnp.zeros_like(sc[0])

    etots = [_ssd_prep(bi, *ins[d], *sc, reverse=bool(d)) for d, bi, sc in chains]
    for g in range(SSM_GROUPS):
        gens = [_ssd_group(g, etot, bi, *ins[d], ys[d], *sc) for etot, (d, bi, sc) in zip(etots, chains)]
        for _ in zip(*gens):
            pass


def _ssd_prep(bi, xs_ref, bm_ref, cm_ref, dt_ref, dtb_ref, nega_ref, cum_ref, mask_ref,
              h_ref, acs_s, acst_s, dtt_s, wstt_s, bmt_s, bmtb_s, *, reverse):
    ln = SSM_CHUNK
    dt = _softplus(dt_ref[bi] + dtb_ref[...])
    a = dt * nega_ref[...]
    a_hi = a.astype(BF16)
    rem = a - a_hi.astype(F32)
    a_mid = rem.astype(BF16)
    a_lo = (rem - a_mid.astype(F32)).astype(BF16)
    acs = jnp.dot(cum_ref[...], jnp.concatenate([a_hi, a_mid, a_lo], 0), preferred_element_type=F32)
    tot = acs[0:1] if reverse else acs[ln - 1:ln]
    etot = jnp.exp(tot)
    acs_s[...] = acs
    acst_s[...] = acs.T
    dtt_s[...] = dt.T
    wstt_s[...] = (dt * jnp.exp(tot - acs)).T
    bm_t = bm_ref[bi].T
    bmt_s[...] = bm_t
    bmtb_s[...] = bm_t.astype(BF16)
    return etot


def _ssd_group(g, etot, bi, xs_ref, bm_ref, cm_ref, dt_ref, dtb_ref, nega_ref, cum_ref, mask_ref, y_ref,
               h_ref, acs_s, acst_s, dtt_s, wstt_s, bmt_s, bmtb_s):
    ln = SSM_CHUNK
    glane = lax.broadcasted_iota(jnp.int32, (ln, SSM_BC), 1) // SSM_STATE
    lo = lax.broadcasted_iota(jnp.int32, (ln, LANES), 1) < SSM_HEAD_DIM
    lo_1 = lax.broadcasted_iota(jnp.int32, (1, LANES), 1) < SSM_HEAD_DIM
    heads_per_group = SSM_HEADS // SSM_GROUPS
    group_w = heads_per_group * SSM_HEAD_DIM
    cm_b = cm_ref[bi].astype(BF16)
    cb = jnp.dot(jnp.where(glane == g, cm_b, jnp.zeros_like(cm_b)), bmtb_s[...],
                 preferred_element_type=F32)
    glanes = slice(g * group_w, (g + 1) * group_w)
    y_off = jnp.dot(cm_b, h_ref[:, glanes].astype(BF16), preferred_element_type=F32)
    rows = slice(g * SSM_STATE, (g + 1) * SSM_STATE)
    for j in range(heads_per_group // 2):
        pair = g * (heads_per_group // 2) + j
        lanes = slice(pair * LANES, (pair + 1) * LANES)
        x = xs_ref[bi, :, lanes]
        x2 = jnp.concatenate([jnp.where(lo, x, 0.0).astype(BF16), jnp.where(lo, 0.0, x).astype(BF16)], 0)
        mats, ecol, wsts, et = [], [], [], []
        for hh in range(2):
            h = 2 * pair + hh
            colb = jnp.broadcast_to(acs_s[:, h:h + 1], (ln, LANES))
            decay = jnp.exp(colb - acst_s[h:h + 1, :] + mask_ref[...])
            mats.append((cb * decay * dtt_s[h:h + 1, :]).astype(BF16))
            ecol.append(jnp.exp(colb))
            wsts.append((bmt_s[rows, :] * wstt_s[h:h + 1, :]).astype(BF16))
            et.append(jnp.broadcast_to(etot[:, h:h + 1], (1, LANES)))
        diag = jnp.dot(jnp.concatenate(mats, 1), x2, preferred_element_type=F32)
        st = jnp.dot(jnp.concatenate(wsts, 1), x2, preferred_element_type=F32)
        y_ref[bi, :, lanes] = diag + y_off[:, j * LANES:(j + 1) * LANES] * jnp.where(lo, ecol[0], ecol[1])
        h_ref[rows, lanes] = h_ref[rows, lanes] * jnp.where(lo_1, et[0], et[1]) + st
        yield


def _ssd(xbc, a_grp, dt_bias, neg_a, nb=2):
    bsz, seq, _ = xbc.shape
    nc = seq // SSM_CHUNK
    nb = nb if bsz % nb == 0 else 1

    def direction(reverse):
        dt_tile = 7 if reverse else 6
        row = 1 if reverse else 0

        def cidx(c):
            return (nc - 1 - c) if reverse else c

        ins = [pl.BlockSpec((nb, SSM_CHUNK, SSM_D_INNER), lambda b, c: (b, cidx(c), 0)),
               pl.BlockSpec((nb, SSM_CHUNK, SSM_BC), lambda b, c: (b, cidx(c), SSM_D_INNER // SSM_BC)),
               pl.BlockSpec((nb, SSM_CHUNK, SSM_BC), lambda b, c: (b, cidx(c), SSM_D_INNER // SSM_BC + 1)),
               pl.BlockSpec((nb, SSM_CHUNK, LANES), lambda b, c: (b, cidx(c), dt_tile)),
               pl.BlockSpec((1, LANES), lambda b, c: (0, 0)),
               pl.BlockSpec((1, LANES), lambda b, c: (0, 0)),
               pl.BlockSpec((SSM_CHUNK, 3 * SSM_CHUNK), lambda b, c: (0, 0)),
               pl.BlockSpec((SSM_CHUNK, SSM_CHUNK), lambda b, c: (0, 0))]
        out = pl.BlockSpec((nb, SSM_CHUNK, SSM_D_INNER), lambda b, c: (b, cidx(c), 0))
        li = np.arange(SSM_CHUNK)
        keep = (li[None, :] >= li[:, None]) if reverse else (li[None, :] <= li[:, None])
        cum = jnp.asarray(np.tile(keep.astype(np.float32), (1, 3)), BF16)
        mask = jnp.asarray(np.where(keep, 0.0, -np.inf).astype(np.float32))
        return ins, out, (xbc, xbc, xbc, a_grp, dt_bias[row:row + 1], neg_a[row:row + 1], cum, mask)

    f_ins, f_out, f_args = direction(False)
    b_ins, b_out, b_args = direction(True)
    y = jax.ShapeDtypeStruct((bsz, seq, SSM_D_INNER), F32)
    per_chain = [pltpu.VMEM((SSM_BC, SSM_D_INNER), F32),
                 pltpu.VMEM((SSM_CHUNK, LANES), F32),
                 pltpu.VMEM((LANES, SSM_CHUNK), F32),
                 pltpu.VMEM((LANES, SSM_CHUNK), F32),
                 pltpu.VMEM((LANES, SSM_CHUNK), F32),
                 pltpu.VMEM((SSM_BC, SSM_CHUNK), F32),
                 pltpu.VMEM((SSM_BC, SSM_CHUNK), BF16)]
    return pl.pallas_call(
        functools.partial(_ssd_kernel, nb=nb),
        grid=(bsz // nb, nc),
        in_specs=f_ins + b_ins,
        out_specs=[f_out, b_out],
        out_shape=[y, y],
        scratch_shapes=per_chain * (2 * nb),
        compiler_params=_params("parallel", "arbitrary"),
        name="ssd",
    )(*f_args, *b_args)


def _layernorm(x, g, b, eps=1e-5):
    mu = jnp.mean(x, -1, keepdims=True)
    xc = x - mu
    var = jnp.mean(xc * xc, -1, keepdims=True)
    return xc * lax.rsqrt(var + eps) * g + b


def _merge_kernel(x_ref, o_ref, yf_ref, yb_ref, xs_ref, z_ref, u_ref, gl_ref,
                  wa_ref, ws_ref, wc_ref, wo_ref, dsk_ref, ng_ref, cg_ref, cb_ref, bc_ref,
                  l1g_ref, l1b_ref, rw_ref, rb_ref,
                  x1_ref, x1g_ref, route_ref, cnt_ref, carry_ref, *, sub):
    tm = x_ref.shape[0]

    @pl.when(pl.program_id(0) == 0)
    def _():
        carry_ref[...] = jnp.zeros_like(carry_ref)

    def rows_of(first):
        rows = pl.ds(first, sub)
        y_attn = jnp.dot(o_ref[rows, :], wa_ref[...], preferred_element_type=F32)
        z = z_ref[rows, :].astype(F32)
        ys = (yf_ref[rows, :] + yb_ref[rows, :] + xs_ref[rows, :] * dsk_ref[...]) * (z * _sigmoid(z))
        gw = SSM_D_INNER // SSM_GROUPS
        ys = jnp.concatenate(
            [_rms(ys[:, g * gw:(g + 1) * gw], ng_ref[:, g * gw:(g + 1) * gw]) for g in range(SSM_GROUPS)], -1)
        yield
        y_ssm = jnp.dot(ys.astype(BF16), ws_ref[...], preferred_element_type=F32)
        uc = _layernorm(u_ref[rows, :], cg_ref[...], cb_ref[...])
        uc = uc * _sigmoid(uc)
        yield
        y_conv = jnp.dot(uc.astype(BF16), wc_ref[...], preferred_element_type=F32) + bc_ref[...]
        gl = gl_ref[rows, :].astype(F32)
        mixed = (_sigmoid(gl[:, :D_MODEL]) * y_attn
                 + _sigmoid(gl[:, D_MODEL:2 * D_MODEL]) * y_ssm
                 + _sigmoid(gl[:, 2 * D_MODEL:]) * y_conv)
        yield
        mixed = jnp.dot(mixed.astype(BF16), wo_ref[...], preferred_element_type=F32)
        x1 = _layernorm(DN_ALPHA * x_ref[rows, :] + mixed, l1g_ref[...], l1b_ref[...])
        x1_ref[rows, :] = x1
        for t in range(ROW_TILES):
            x1g_ref[pl.ds(first * ROW_TILES + t, sub, stride=ROW_TILES), :] = x1[:, t * LANES:(t + 1) * LANES]
        yield
        x_hi = x1.astype(BF16)
        x_lo = (x1 - x_hi.astype(F32)).astype(BF16)
        lg = (jnp.dot(jnp.concatenate([x_hi, x_lo, x_hi], -1), rw_ref[...], preferred_element_type=F32)
              + rb_ref[...])
        lane = lax.broadcasted_iota(jnp.int32, (sub, LANES), 1).astype(F32)
        sels, vals, idxs = [], [], []
        for _ in range(TOP_K):
            m = jnp.max(lg, -1, keepdims=True)
            idx = jnp.min(jnp.where(lg == m, lane, float(LANES)), -1, keepdims=True)
            sel = lane == idx
            lg = jnp.where(sel, NEG_BIG * 2, lg)
            sels.append(sel)
            vals.append(m)
            idxs.append(idx)
        yield
        es = [jnp.exp(v - vals[0]) for v in vals]
        den = es[0] + es[1] + es[2] + es[3]
        hot = jnp.zeros((sub, LANES), F32)
        for sel in sels:
            hot = hot + sel.astype(F32)
        r = lax.broadcasted_iota(jnp.int32, (sub, sub), 0)
        c = lax.broadcasted_iota(jnp.int32, (sub, sub), 1)
        excl = jnp.dot((c < r).astype(BF16), hot.astype(BF16), preferred_element_type=F32) + carry_ref[...]
        carry_ref[...] = carry_ref[...] + jnp.sum(hot, 0, keepdims=True)
        route = jnp.zeros((sub, LANES), F32)
        for kk in range(TOP_K):
            rank = jnp.sum(jnp.where(sels[kk], excl, 0.0), -1, keepdims=True)
            route = jnp.where(lane == float(kk), idxs[kk], route)
            route = jnp.where(lane == float(TOP_K + kk), rank, route)
            route = jnp.where(lane == float(2 * TOP_K + kk), es[kk] / den, route)
        route_ref[rows, :] = route
        yield

    for _ in zip(*[rows_of(first) for first in range(0, tm, sub)]):
        pass
    cnt_ref[...] = carry_ref[...]


def _merge(x, o, yf, yb, xbc, z, u, gl, wa, ws, wc, wo, dsk, ng, cg, cb, bc, l1g, l1b, rw, rb, tm=512, sub=256):
    t = x.shape[0]
    tm = min(tm, t)

    def rowb(width, col=0):
        return pl.BlockSpec((tm, width), lambda i: (i, col))

    def const(shape):
        return pl.BlockSpec(shape, lambda i: (0, 0))

    return pl.pallas_call(
        functools.partial(_merge_kernel, sub=min(sub, tm)),
        grid=(t // tm,),
        in_specs=[rowb(D_MODEL), rowb(MLA_HEADS * MLA_DV), rowb(SSM_D_INNER), rowb(SSM_D_INNER), rowb(SSM_D_INNER),
                  rowb(SSM_D_INNER), rowb(CNV_CH), rowb(3 * D_MODEL),
                  const((MLA_HEADS * MLA_DV, D_MODEL)), const((SSM_D_INNER, D_MODEL)), const((CNV_CH, D_MODEL)),
                  const((D_MODEL, D_MODEL)), const((1, SSM_D_INNER)), const((1, SSM_D_INNER)),
                  const((1, CNV_CH)), const((1, CNV_CH)), const((1, D_MODEL)),
                  const((1, D_MODEL)), const((1, D_MODEL)), const((3 * D_MODEL, LANES)), const((1, LANES))],
        out_specs=[rowb(D_MODEL),
                   pl.BlockSpec((tm * ROW_TILES, LANES), lambda i: (i, 0)),
                   rowb(LANES), const((1, LANES))],
        out_shape=[jax.ShapeDtypeStruct((t, D_MODEL), F32),
                   jax.ShapeDtypeStruct((t * ROW_TILES, LANES), F32),
                   jax.ShapeDtypeStruct((t, LANES), F32), jax.ShapeDtypeStruct((1, LANES), F32)],
        scratch_shapes=[pltpu.VMEM((1, LANES), F32)],
        compiler_params=_params("arbitrary"),
        name="merge",
    )(x, o, yf, yb, xbc, z, u, gl, wa, ws, wc, wo, dsk, ng, cg, cb, bc, l1g, l1b, rw, rb)


def _row_slice(ref, row):
    return ref.at[pl.ds(pl.multiple_of(row * ROW_TILES, ROW_TILES), ROW_TILES)]


def _from_row_tiles(ref, first, rows, pitch):
    return jnp.concatenate([ref[pl.ds(first + t, rows, stride=pitch), :] for t in range(ROW_TILES)], -1)


def _fill_padding(fill_start_ref, fill_n_ref, zero_ref, xs_hbm, sem):
    zero_ref[...] = jnp.zeros_like(zero_ref)
    bits = [1 << b for b in reversed(range((MOE_ROWS - 1).bit_length()))]

    def copies(e):
        n_e = fill_n_ref[e]
        for bit in bits:
            done = n_e & ~(2 * bit - 1)
            yield (n_e & bit) != 0, pltpu.make_async_copy(
                zero_ref.at[pl.ds(0, bit * ROW_TILES)],
                xs_hbm.at[pl.ds(pl.multiple_of((fill_start_ref[e] + done) * ROW_TILES, ROW_TILES),
                                bit * ROW_TILES)], sem)

    def start(e, carry):
        for cond, cp in copies(e):
            pl.when(cond)(cp.start)
        return carry

    def wait(e, carry):
        for cond, cp in copies(e):
            pl.when(cond)(cp.wait)
        return carry

    def tail(j):
        first = (fill_start_ref[N_EXPERTS] * ROW_TILES + j * zero_ref.shape[0])
        return pltpu.make_async_copy(zero_ref, xs_hbm.at[pl.ds(pl.multiple_of(first, ROW_TILES),
                                                               zero_ref.shape[0])], sem)

    def tail_start(j, carry):
        tail(j).start()
        return carry

    def tail_wait(j, carry):
        tail(j).wait()
        return carry

    lax.fori_loop(0, N_EXPERTS, start, 0)
    lax.fori_loop(0, fill_n_ref[N_EXPERTS], tail_start, 0)
    lax.fori_loop(0, N_EXPERTS, wait, 0)
    lax.fori_loop(0, fill_n_ref[N_EXPERTS], tail_wait, 0)


def _dispatch_kernel(fill_start_ref, fill_n_ref, dest_ref, x_ref, xs_hbm, sem, fill_sem, zero_ref, *, tm):
    @pl.when(pl.program_id(0) == pl.num_programs(0) - 1)
    def _():
        _fill_padding(fill_start_ref, fill_n_ref, zero_ref, xs_hbm, fill_sem)

    def body(c, carry):
        for u in range(ISSUE_UNROLL):
            t = c * ISSUE_UNROLL + u
            src = _row_slice(x_ref, t)
            for kk in range(TOP_K):
                pltpu.make_async_copy(src, _row_slice(xs_hbm, dest_ref[0, 0, t * TOP_K + kk]),
                                      sem).start(priority=kk % 2)
        return carry

    lax.fori_loop(0, tm // ISSUE_UNROLL, body, 0)
    for _ in range(TOP_K):
        pltpu.make_async_copy(x_ref, xs_hbm.at[pl.ds(0, tm * ROW_TILES)], sem).wait()


def _dispatch(dest, fill_start, fill_n, x1g, n_rows, tm=512):
    t = x1g.shape[0] // ROW_TILES
    tm = min(tm, t)
    nt = t // tm
    grid_spec = pltpu.PrefetchScalarGridSpec(
        num_scalar_prefetch=2,
        grid=(nt,),
        in_specs=[pl.BlockSpec((1, 1, tm * TOP_K), lambda i, fs, fn: (i, 0, 0), memory_space=pltpu.SMEM),
                  pl.BlockSpec((tm * ROW_TILES, LANES), lambda i, fs, fn: (i, 0))],
        out_specs=pl.BlockSpec(memory_space=pl.ANY),
        scratch_shapes=[pltpu.SemaphoreType.DMA(()), pltpu.SemaphoreType.DMA(()),
                        pltpu.VMEM((MOE_ROWS // 2 * ROW_TILES, LANES), F32)],
    )
    return pl.pallas_call(
        functools.partial(_dispatch_kernel, tm=tm),
        grid_spec=grid_spec,
        out_shape=jax.ShapeDtypeStruct((n_rows * ROW_TILES, LANES), F32),
        compiler_params=_params("arbitrary"),
        name="dispatch",
    )(fill_start, fill_n, dest.reshape(nt, 1, tm * TOP_K), x1g)


def _moe_kernel(blk_e_ref, nused_ref, x_ref, wgu_ref, bgu_ref, wdn_ref, bdn_ref, o_ref, wgu_b, wdn_b):
    i = pl.program_id(0)

    @pl.when((i == 0) | (blk_e_ref[i] != blk_e_ref[jnp.maximum(i - 1, 0)]))
    def _():
        wgu_b[...] = wgu_ref[0, 0].astype(BF16)
        wdn_b[...] = wdn_ref[0, 0].astype(BF16)

    @pl.when(i < nused_ref[0])
    def _():
        xb = _from_row_tiles(x_ref, 0, MOE_ROWS, ROW_TILES).astype(BF16)
        h = jnp.dot(xb, wgu_b[...], preferred_element_type=F32) + bgu_ref[0]
        gate = jnp.minimum(h[:, :D_FF], SWIGLU_LIMIT)
        up = jnp.clip(h[:, D_FF:], -SWIGLU_LIMIT, SWIGLU_LIMIT)
        act = (up + 1.0) * (gate * _sigmoid(SWIGLU_ALPHA * gate))
        y = jnp.dot(act.astype(BF16), wdn_b[...], preferred_element_type=F32) + bdn_ref[0]
        for t in range(ROW_TILES):
            o_ref[pl.ds(t, MOE_ROWS, stride=ROW_TILES), :] = y[:, t * LANES:(t + 1) * LANES]

    @pl.when(i >= nused_ref[0])
    def _():
        o_ref[...] = jnp.zeros_like(o_ref)


def _moe_experts(blk_e, nused, xs, wgu, bgu, wdn, bdn, layer):
    n_blocks = blk_e.shape[0]
    grid_spec = pltpu.PrefetchScalarGridSpec(
        num_scalar_prefetch=2,
        grid=(n_blocks,),
        in_specs=[pl.BlockSpec((MOE_ROWS * ROW_TILES, LANES), lambda i, be, nu: (jnp.minimum(i, nu[0] - 1), 0)),
                  pl.BlockSpec((1, 1, D_MODEL, 2 * D_FF), lambda i, be, nu: (layer, be[i], 0, 0)),
                  pl.BlockSpec((1, 1, 2 * D_FF), lambda i, be, nu: (be[i], 0, 0)),
                  pl.BlockSpec((1, 1, D_FF, D_MODEL), lambda i, be, nu: (layer, be[i], 0, 0)),
                  pl.BlockSpec((1, 1, D_MODEL), lambda i, be, nu: (be[i], 0, 0))],
        out_specs=pl.BlockSpec((MOE_ROWS * ROW_TILES, LANES), lambda i, be, nu: (i, 0)),
        scratch_shapes=[pltpu.VMEM((D_MODEL, 2 * D_FF), BF16), pltpu.VMEM((D_FF, D_MODEL), BF16)],
    )
    return pl.pallas_call(
        _moe_kernel,
        grid_spec=grid_spec,
        out_shape=jax.ShapeDtypeStruct(xs.shape, F32),
        compiler_params=_params("arbitrary"),
        name="moe_experts",
    )(blk_e, nused, xs, wgu, bgu, wdn, bdn)


def _gather_topk_rows(dest_ref, yb_hbm, dst, sem, tm):
    def body(c, carry):
        for u in range(ISSUE_UNROLL):
            t = c * ISSUE_UNROLL + u
            for kk in range(TOP_K):
                pltpu.make_async_copy(_row_slice(yb_hbm, dest_ref[0, 0, t * TOP_K + kk]),
                                      _row_slice(dst, kk * tm + t), sem).start(priority=kk % 2)
        return carry
    lax.fori_loop(0, tm // ISSUE_UNROLL, body, 0)


def _combine_kernel(dest_first_ref, dest_next_ref, yb_hbm, x1_ref, route_ref, g_ref, b_ref, x2_ref, x2b_ref,
                    buf, sem, *, tm):
    i = pl.program_id(0)
    n = pl.num_programs(0)
    slot = i % 2

    @pl.when(i == 0)
    def _():
        _gather_topk_rows(dest_first_ref, yb_hbm, buf.at[0], sem.at[0], tm)

    @pl.when(i + 1 < n)
    def _():
        _gather_topk_rows(dest_next_ref, yb_hbm, buf.at[1 - slot], sem.at[1 - slot], tm)

    pltpu.make_async_copy(yb_hbm.at[pl.ds(0, tm * TOP_K * ROW_TILES)], buf.at[slot], sem.at[slot]).wait()
    route = route_ref[...]
    ffn = jnp.zeros((tm, D_MODEL), F32)
    for kk in range(TOP_K):
        rows = _from_row_tiles(buf.at[slot], kk * tm * ROW_TILES, tm, ROW_TILES)
        ffn = ffn + rows * route[:, 2 * TOP_K + kk:2 * TOP_K + kk + 1]
    x2 = _layernorm(DN_ALPHA * x1_ref[...] + ffn, g_ref[...], b_ref[...])
    x2_ref[...] = x2
    x2b_ref[...] = x2.astype(BF16)


def _combine(dest, yb, x1, route, g, b, tm=512):
    t = x1.shape[0]
    tm = min(tm, t)
    nt = t // tm
    dest3 = dest.reshape(nt, 1, tm * TOP_K)
    return pl.pallas_call(
        functools.partial(_combine_kernel, tm=tm),
        grid=(nt,),
        in_specs=[pl.BlockSpec((1, 1, tm * TOP_K), lambda i: (0, 0, 0), memory_space=pltpu.SMEM),
                  pl.BlockSpec((1, 1, tm * TOP_K), lambda i: (jnp.minimum(i + 1, nt - 1), 0, 0),
                               memory_space=pltpu.SMEM),
                  pl.BlockSpec(memory_space=pl.ANY),
                  pl.BlockSpec((tm, D_MODEL), lambda i: (i, 0)),
                  pl.BlockSpec((tm, LANES), lambda i: (i, 0)),
                  pl.BlockSpec((1, D_MODEL), lambda i: (0, 0)),
                  pl.BlockSpec((1, D_MODEL), lambda i: (0, 0))],
        out_specs=[pl.BlockSpec((tm, D_MODEL), lambda i: (i, 0)),
                   pl.BlockSpec((tm, D_MODEL), lambda i: (i, 0))],
        out_shape=[jax.ShapeDtypeStruct((t, D_MODEL), F32), jax.ShapeDtypeStruct((t, D_MODEL), BF16)],
        scratch_shapes=[pltpu.VMEM((2, tm * TOP_K * ROW_TILES, LANES), F32),
                        pltpu.SemaphoreType.DMA((2,))],
        compiler_params=_params("arbitrary"),
        name="combine",
    )(dest3, dest3, yb, x1, route, g, b)


def _pad_cols(w, width):
    return jnp.pad(w, ((0, 0), (0, width - w.shape[1])))


def _pack_in_proj(w_in, b_in):
    offs = np.concatenate([[0], np.cumsum(IN_SIZES)])
    wb = jnp.concatenate([w_in, b_in[None, :]], 0)

    def piece(i):
        return wb[:, offs[i]:offs[i + 1]]

    kr = piece(2)
    kr_swapped = jnp.concatenate([kr[:, MLA_DR // 2:], kr[:, :MLA_DR // 2]], 1)
    dt = piece(5)
    grp_a = jnp.concatenate([piece(0), piece(1), _pad_cols(jnp.concatenate([kr, kr_swapped], 1), LANES),
                             _pad_cols(dt[:, :SSM_HEADS], LANES), _pad_cols(dt[:, SSM_HEADS:], LANES)], 1)
    groups = {"a": (grp_a, F32), "z": (piece(3), BF16), "xbc": (piece(4), F32),
              "cnv": (jnp.concatenate([piece(6), piece(7)], 1), F32), "gate": (piece(8), BF16)}
    return {k: (v[:-1].astype(BF16), v[-1:], dt_) for k, (v, dt_) in groups.items()}


def _pack_mla(w_uq, w_ukv, w_br_attn, seq):
    hq = MLA_DN + MLA_DR
    half = MLA_DR // 2
    wq = w_uq.reshape(MLA_Q_LORA, MLA_HEADS, hq)
    zq = jnp.zeros((MLA_Q_LORA, MLA_HEADS, HEAD_PAD - hq), F32)
    wq_main = jnp.concatenate([wq, zq], -1).reshape(MLA_Q_LORA, MLA_W)
    wq_swap = jnp.concatenate([jnp.zeros((MLA_Q_LORA, MLA_HEADS, MLA_DN), F32),
                               wq[..., MLA_DN + half:], wq[..., MLA_DN:MLA_DN + half], zq],
                              -1).reshape(MLA_Q_LORA, MLA_W)
    wkv = w_ukv.reshape(MLA_KV_LORA, MLA_HEADS, MLA_DN + MLA_DV)
    zk = jnp.zeros((MLA_KV_LORA, MLA_HEADS, HEAD_PAD - MLA_DN), F32)
    wk = jnp.concatenate([wkv[..., :MLA_DN], zk], -1).reshape(MLA_KV_LORA, MLA_W)
    wv = jnp.concatenate([wkv[..., MLA_DN:], zk], -1).reshape(MLA_KV_LORA, MLA_W)
    e2 = np.zeros((LANES, MLA_HEADS, HEAD_PAD), np.float32)
    for j in range(MLA_DR):
        e2[j, :, MLA_DN + j] = 1.0
        e2[MLA_DR + j, :, MLA_DN + j] = 1.0
    vone = np.zeros((MLA_HEADS, HEAD_PAD), np.float32)
    vone[:, MLA_DV] = 1.0
    pos = jnp.arange(seq, dtype=F32)
    inv = ROPE_THETA ** (-jnp.arange(0, MLA_DR, 2, dtype=F32) / MLA_DR)
    ang = pos[:, None] * inv[None, :]
    cos, sin = jnp.cos(ang), jnp.sin(ang)
    scale = (MLA_DN + MLA_DR) ** -0.5 * math.log2(math.e)
    ones = jnp.ones((seq, MLA_DN), F32)
    zpad = jnp.zeros((seq, HEAD_PAD - hq), F32)
    cq = jnp.tile(jnp.concatenate([ones, cos, cos, zpad], 1) * scale, (1, MLA_HEADS))
    sq = jnp.tile(jnp.concatenate([0 * ones, -sin, sin, zpad], 1) * scale, (1, MLA_HEADS))
    tk = jnp.concatenate([cos, cos, -sin, sin, jnp.zeros((seq, LANES - 2 * MLA_DR), F32)], 1)
    return dict(wq=wq_main.astype(BF16), wqs=wq_swap.astype(BF16), wk=wk.astype(BF16), wv=wv.astype(BF16),
                e2=jnp.asarray(e2.reshape(LANES, MLA_W), BF16), vone=jnp.asarray(vone.reshape(1, MLA_W)),
                wbr=w_br_attn.astype(BF16), cq=cq, sq=sq, tk=tk)


def _route_tables(route, cnt, n_tok):
    idx = route[:, :TOP_K].astype(jnp.int32)
    rank = route[:, TOP_K:2 * TOP_K].astype(jnp.int32)
    counts = cnt[0, :N_EXPERTS].astype(jnp.int32)
    n_blocks = -(-(n_tok * TOP_K + N_EXPERTS * (MOE_ROWS - 1)) // MOE_ROWS)
    padded = (counts + MOE_ROWS - 1) // MOE_ROWS * MOE_ROWS
    pad_end = jnp.cumsum(padded)
    pad_start = pad_end - padded
    onehot = idx[..., None] == jnp.arange(N_EXPERTS, dtype=jnp.int32)
    dest = (jnp.sum(jnp.where(onehot, pad_start, 0), -1) + rank).reshape(-1)
    blk_first = jnp.arange(n_blocks, dtype=jnp.int32) * MOE_ROWS
    blk_e = jnp.minimum(jnp.sum((pad_end[None, :] <= blk_first[:, None]).astype(jnp.int32), -1), N_EXPERTS - 1)
    nused = pad_end[-1:] // MOE_ROWS
    n_rows = n_blocks * MOE_ROWS
    fill_start = jnp.concatenate([pad_start + counts, pad_end[-1:]])
    fill_n = jnp.concatenate([padded - counts, (n_rows - pad_end[-1:]) // (MOE_ROWS // 2)])
    return dest, blk_e, nused, fill_start, fill_n, n_rows


def kernel(x, w_in, b_in, mla_q_norm, mla_kv_norm, mla_w_uq, mla_w_ukv, w_br_attn, ssm_conv_w, ssm_conv_b, ssm_dt_bias, ssm_a_log, ssm_d, ssm_norm, w_br_ssm, cnv_dw_w, cnv_dw_b, cnv_ln_g, cnv_ln_b, w_br_conv, b_br_conv, w_out, ln1_g, ln1_b, router_w, router_b, moe_w_gate_up, moe_b_gate_up, moe_w_down, moe_b_down, ln2_g, ln2_b):
    bsz, seq, d = x.shape
    n_tok = bsz * seq
    xf = x.reshape(n_tok, d)
    xb = xf.astype(BF16)
    for l in range(DEPTH):
        proj = _pack_in_proj(w_in[l], b_in[l])
        mla = _pack_mla(mla_w_uq[l], mla_w_ukv[l], w_br_attn[l], seq)
        a_grp = _linear(xb, *proj["a"])
        z = _linear(xb, *proj["z"])
        xbc_raw = _linear(xb, *proj["xbc"], tn=768)
        cnv_raw = _linear(xb, *proj["cnv"])
        gate_logits = _linear(xb, *proj["gate"], tn=1024)

        q, k, v = _mla_prep(a_grp, seq, mla_q_norm[l][None], mla_kv_norm[l][None], mla["wq"], mla["wqs"],
                            mla["wk"], mla["wv"], mla["e2"], mla["cq"], mla["sq"], mla["tk"], mla["vone"])
        attn = _attention(q.reshape(bsz, seq, MLA_W), k.reshape(bsz, seq, MLA_W), v.reshape(bsz, seq, MLA_W))

        xbc = _dwconv(xbc_raw.reshape(bsz, seq, SSM_CONV_DIM), ssm_conv_w[l], ssm_conv_b[l],
                      glu=False, silu_out=True, rows=256)
        a3 = a_grp.reshape(bsz, seq, GROUP_A)
        dtb = jnp.pad(ssm_dt_bias[l], ((0, 0), (0, LANES - SSM_HEADS)))
        nega = jnp.pad(-jnp.exp(ssm_a_log[l]), ((0, 0), (0, LANES - SSM_HEADS)))
        y_fwd, y_bwd = _ssd(xbc, a3, dtb, nega)

        u = _dwconv(cnv_raw.reshape(bsz, seq, 2 * CNV_CH), cnv_dw_w[l], cnv_dw_b[l], glu=True, silu_out=False,
                    rows=128)

        rw = jnp.pad(router_w[l], ((0, 0), (0, LANES - N_EXPERTS)))
        rw_hi = rw.astype(BF16)
        rw_lo = (rw - rw_hi.astype(F32)).astype(BF16)
        rw = jnp.concatenate([rw_hi, rw_hi, rw_lo], 0)
        rb = jnp.pad(router_b[l], (0, LANES - N_EXPERTS), constant_values=NEG_BIG)[None]
        x1, x1g, route, cnt = _merge(
            xf, attn.reshape(n_tok, MLA_HEADS * MLA_DV), y_fwd.reshape(n_tok, SSM_D_INNER), y_bwd.reshape(n_tok, SSM_D_INNER),
            xbc.reshape(n_tok, SSM_CONV_DIM), z, u.reshape(n_tok, CNV_CH), gate_logits,
            mla["wbr"], w_br_ssm[l].astype(BF16), w_br_conv[l].astype(BF16), w_out[l].astype(BF16),
            jnp.repeat(ssm_d[l], SSM_HEAD_DIM)[None], ssm_norm[l][None], cnv_ln_g[l][None], cnv_ln_b[l][None],
            b_br_conv[l][None], ln1_g[l][None], ln1_b[l][None], rw, rb)

        dest, blk_e, nused, fill_start, fill_n, n_rows = _route_tables(route, cnt, n_tok)
        xs = _dispatch(dest, fill_start, fill_n, x1g, n_rows)
        yb = _moe_experts(blk_e, nused, xs, moe_w_gate_up, moe_b_gate_up[l][:, None, :], moe_w_down,
                          moe_b_down[l][:, None, :], l)
        xf, xb = _combine(dest, yb, x1, route, ln2_g[l][None], ln2_b[l][None])
    return xf.reshape(bsz, seq, d)
```

```python
import functools
import math

import numpy as np
import jax
import jax.numpy as jnp
from jax import lax
from jax.experimental import pallas as pl
from jax.experimental.pallas import tpu as pltpu

F32 = jnp.float32
BF16 = jnp.bfloat16
HIGHEST = lax.Precision.HIGHEST

LANES = 128
SUBLANES = 8
VMEM_LIMIT_BYTES = 56 * 1024 * 1024

D_MODEL = 1024
DEPTH = 2
MLA_HEADS = 8
MLA_Q_LORA = 384
MLA_KV_LORA = 256
MLA_DN = 64
MLA_DR = 32
MLA_DV = 64
ROPE_THETA = 10000.0
SSM_HEADS = 16
SSM_HEAD_DIM = 64
SSM_D_INNER = SSM_HEADS * SSM_HEAD_DIM
SSM_GROUPS = 4
SSM_STATE = 64
SSM_CONV = 5
SSM_CHUNK = 128
SSM_BC = SSM_GROUPS * SSM_STATE
SSM_CONV_DIM = SSM_D_INNER + 2 * SSM_BC
CNV_CH = 512
CNV_WIDTH = 31
N_EXPERTS = 32
TOP_K = 4
D_FF = 1024
SWIGLU_LIMIT = 7.0
SWIGLU_ALPHA = 1.702
DN_ALPHA = (2 * DEPTH) ** 0.25
IN_SIZES = (MLA_Q_LORA, MLA_KV_LORA, MLA_DR, SSM_D_INNER, SSM_CONV_DIM, 2 * SSM_HEADS,
            CNV_CH, CNV_CH, 3 * D_MODEL)
HEAD_PAD = LANES
MLA_W = MLA_HEADS * HEAD_PAD
GROUP_A = 1024
CONV_HALO = 16
MOE_ROWS = 512
ISSUE_UNROLL = 8
ROW_TILES = D_MODEL // LANES
NEG_BIG = -1e30


def _params(*sem):
    return pltpu.CompilerParams(dimension_semantics=sem, vmem_limit_bytes=VMEM_LIMIT_BYTES)


def _sigmoid(x):
    return 0.5 * jnp.tanh(0.5 * x) + 0.5


def _linear_kernel(x_ref, w_ref, b_ref, o_ref):
    acc = jnp.dot(x_ref[...], w_ref[...], preferred_element_type=F32)
    o_ref[...] = (acc + b_ref[...]).astype(o_ref.dtype)


def _linear(x, w, b, out_dtype, tm=1024, tn=None):
    m, k = x.shape
    n = w.shape[1]
    tm = min(tm, m)
    tn = n if tn is None else tn
    return pl.pallas_call(
        _linear_kernel,
        grid=(n // tn, m // tm),
        in_specs=[pl.BlockSpec((tm, k), lambda j, i: (i, 0)),
                  pl.BlockSpec((k, tn), lambda j, i: (0, j)),
                  pl.BlockSpec((1, tn), lambda j, i: (0, j))],
        out_specs=pl.BlockSpec((tm, tn), lambda j, i: (i, j)),
        out_shape=jax.ShapeDtypeStruct((m, n), out_dtype),
        compiler_params=_params("parallel", "parallel"),
        name="linear",
    )(x, w, b)


def _rms(x, g, eps=1e-6):
    return x * lax.rsqrt(jnp.mean(x * x, -1, keepdims=True) + eps) * g


def _mla_prep_kernel(a_ref, gq_ref, gkv_ref, wq_ref, wk_ref, wv_ref, e2_ref,
                     cq_ref, sq_ref, tk_ref, vone_ref, q_ref, k_ref, v_ref, *, sub):
    def rows_of(first):
        rows = pl.ds(first, sub)
        a = a_ref[rows, :]
        c_q = a[:, :MLA_Q_LORA]
        c_kv = a[:, MLA_Q_LORA:MLA_Q_LORA + MLA_KV_LORA]
        kr = a[:, MLA_Q_LORA + MLA_KV_LORA:]
        qn = _rms(c_q, gq_ref[...]).astype(BF16)
        kvn = _rms(c_kv, gkv_ref[...]).astype(BF16)
        yield
        q = jnp.dot(qn, wq_ref[...], preferred_element_type=F32)
        half = MLA_DR // 2
        first_half = lax.broadcasted_iota(jnp.int32, q.shape, 1) % HEAD_PAD < MLA_DN + half
        swapped = jnp.where(first_half, pltpu.roll(q, MLA_W - half, 1), pltpu.roll(q, half, 1))
        q_ref[rows, :] = (q * cq_ref[rows, :] + swapped * sq_ref[rows, :]).astype(BF16)
        yield
        krp = (kr * tk_ref[rows, :]).astype(BF16)
        k = (jnp.dot(kvn, wk_ref[...], preferred_element_type=F32)
             + jnp.dot(krp, e2_ref[...], preferred_element_type=F32))
        k_ref[rows, :] = k.astype(BF16)
        yield
        v = jnp.dot(kvn, wv_ref[...], preferred_element_type=F32) + vone_ref[...]
        v_ref[rows, :] = v.astype(BF16)
        yield

    for _ in zip(*[rows_of(first) for first in range(0, a_ref.shape[0], sub)]):
        pass


def _mla_prep(a, seq, gq, gkv, wq, wk, wv, e2, cq, sq, tk, vone, tm=1024, sub=512):
    t = a.shape[0]
    tm = min(tm, seq)
    nper = seq // tm
    wa = MLA_Q_LORA + MLA_KV_LORA + LANES

    def const(shape):
        return pl.BlockSpec(shape, lambda i: (0, 0))

    def tab(width):
        return pl.BlockSpec((tm, width), lambda i: (i % nper, 0))

    out = jax.ShapeDtypeStruct((t, MLA_W), BF16)
    return pl.pallas_call(
        functools.partial(_mla_prep_kernel, sub=min(sub, tm)),
        grid=(t // tm,),
        in_specs=[pl.BlockSpec((tm, wa), lambda i: (i, 0)),
                  const((1, MLA_Q_LORA)), const((1, MLA_KV_LORA)),
                  const((MLA_Q_LORA, MLA_W)),
                  const((MLA_KV_LORA, MLA_W)), const((MLA_KV_LORA, MLA_W)),
                  const((LANES, MLA_W)),
                  tab(MLA_W), tab(MLA_W), tab(LANES), const((1, MLA_W))],
        out_specs=[pl.BlockSpec((tm, MLA_W), lambda i: (i, 0))] * 3,
        out_shape=[out, out, out],
        compiler_params=_params("parallel"),
        name="mla_prep",
    )(a, gq, gkv, wq, wk, wv, e2, cq, sq, tk, vone)


def _attn_kernel(q_ref, k_ref, v_ref, o_ref, *, sub):
    low = lax.broadcasted_iota(jnp.int32, (sub, HEAD_PAD), 1) < MLA_DV
    for r in range(q_ref.shape[1] // sub):
        rows = slice(r * sub, (r + 1) * sub)
        outs = []
        for hh in range(2):
            lanes = slice(hh * HEAD_PAD, (hh + 1) * HEAD_PAD)
            s = lax.dot_general(q_ref[0, rows, lanes], k_ref[0, :, lanes], (((1,), (1,)), ((), ())),
                                preferred_element_type=F32)
            m = jnp.max(s, -1, keepdims=True)
            p = jnp.exp2(s - m).astype(BF16)
            o = jnp.dot(p, v_ref[0, :, lanes], preferred_element_type=F32)
            outs.append(o / o[:, MLA_DV:MLA_DV + 1])
        o_ref[0, rows] = jnp.where(low, outs[0], pltpu.roll(outs[1], MLA_DV, 1)).astype(o_ref.dtype)


def _attention(q, k, v, tq=1024, sub=256):
    b, s, _ = q.shape
    tq = min(tq, s)
    pair = 2 * HEAD_PAD
    return pl.pallas_call(
        functools.partial(_attn_kernel, sub=min(sub, tq)),
        grid=(b, MLA_HEADS // 2, s // tq),
        in_specs=[pl.BlockSpec((1, tq, pair), lambda bi, h, i: (bi, i, h)),
                  pl.BlockSpec((1, s, pair), lambda bi, h, i: (bi, 0, h)),
                  pl.BlockSpec((1, s, pair), lambda bi, h, i: (bi, 0, h))],
        out_specs=pl.BlockSpec((1, tq, HEAD_PAD), lambda bi, h, i: (bi, i, h)),
        out_shape=jax.ShapeDtypeStruct((b, s, MLA_HEADS * MLA_DV), BF16),
        compiler_params=_params("parallel", "parallel", "parallel"),
        name="attention",
    )(q, k, v)


def _dwconv_kernel(*refs, width, glu, silu_out, seq, rows):
    if glu:
        a_ref, g_ref, w_ref, b_ref, o_ref, pad_ref = refs
        pre = a_ref[0] * _sigmoid(g_ref[0])
    else:
        x_ref, w_ref, b_ref, o_ref, pad_ref = refs
        pre = x_ref[0]
    ch = o_ref.shape[-1]
    halo = jnp.zeros((CONV_HALO, ch), F32)
    pad_ref[0:CONV_HALO, :] = halo
    pad_ref[CONV_HALO + seq:2 * CONV_HALO + seq, :] = halo
    pad_ref[CONV_HALO:CONV_HALO + seq, :] = pre
    half = (width - 1) // 2
    win_rows = rows + 2 * CONV_HALO

    def body(c, carry):
        base = pl.multiple_of(c * rows, rows)
        acc = jnp.zeros((rows, ch), F32) + b_ref[...]
        for t in range(width):
            acc = acc + pad_ref[pl.ds(base + (CONV_HALO - half + t), rows), :] * w_ref[t:t + 1, :]
        if silu_out:
            acc = acc * _sigmoid(acc)
        o_ref[0, pl.ds(base, rows), :] = acc
        return carry

    lax.fori_loop(0, seq // rows, body, 0)


def _dwconv(x, w, b, *, glu, silu_out, rows=64):
    bsz, seq, cin = x.shape
    width, ch = w.shape
    nct = ch // LANES
    kern = functools.partial(_dwconv_kernel, width=width, glu=glu, silu_out=silu_out, seq=seq,
                             rows=min(rows, seq))
    xspec = pl.BlockSpec((1, seq, LANES), lambda bi, j: (bi, 0, j))
    in_specs = [xspec]
    args = [x]
    if glu:
        in_specs.append(pl.BlockSpec((1, seq, LANES), lambda bi, j: (bi, 0, j + nct)))
        args.append(x)
    in_specs += [pl.BlockSpec((width, LANES), lambda bi, j: (0, j)),
                 pl.BlockSpec((1, LANES), lambda bi, j: (0, j))]
    return pl.pallas_call(
        kern,
        grid=(bsz, nct),
        in_specs=in_specs,
        out_specs=pl.BlockSpec((1, seq, LANES), lambda bi, j: (bi, 0, j)),
        out_shape=jax.ShapeDtypeStruct((bsz, seq, ch), F32),
        scratch_shapes=[pltpu.VMEM((seq + 2 * CONV_HALO, LANES), F32)],
        compiler_params=_params("parallel", "parallel"),
        name="dwconv",
    )(*args, w, b.reshape(1, ch))


def _softplus(x):
    return jnp.maximum(x, 0.0) + jnp.log1p(jnp.exp(-jnp.abs(x)))


def _ssd_kernel(*refs, nb):
    n_in = 8
    ins = (refs[0:n_in], refs[n_in:2 * n_in])
    ys = refs[2 * n_in:2 * n_in + 2]
    scratch = refs[2 * n_in + 2:]
    n_s = len(scratch) // (2 * nb)
    chains = [(d, bi, scratch[(d * nb + bi) * n_s:(d * nb + bi + 1) * n_s]) for bi in range(nb) for d in range(2)]

    @pl.when(pl.program_id(1) == 0)
    def _():
        for _, _, sc in chains:
            sc[0][...] = jnp.zeros_like(sc[0])

    etots = [_ssd_prep(bi, *ins[d], *sc, reverse=bool(d)) for d, bi, sc in chains]
    for g in range(SSM_GROUPS):
        gens = [_ssd_group(g, etot, bi, *ins[d], ys[d], *sc) for etot, (d, bi, sc) in zip(etots, chains)]
        for _ in zip(*gens):
            pass


def _ssd_prep(bi, xs_ref, bm_ref, cm_ref, dt_ref, dtb_ref, nega_ref, cum_ref, mask_ref,
              h_ref, acs_s, acst_s, dtt_s, wstt_s, bmt_s, bmtb_s, *, reverse):
    ln = SSM_CHUNK
    dt = _softplus(dt_ref[bi] + dtb_ref[...])
    a = dt * nega_ref[...]
    a_hi = a.astype(BF16)
    rem = a - a_hi.astype(F32)
    a_mid = rem.astype(BF16)
    a_lo = (rem - a_mid.astype(F32)).astype(BF16)
    acs = jnp.dot(cum_ref[...], jnp.concatenate([a_hi, a_mid, a_lo], 0), preferred_element_type=F32)
    tot = acs[0:1] if reverse else acs[ln - 1:ln]
    etot = jnp.exp(tot)
    acs_s[...] = acs
    acst_s[...] = acs.T
    dtt_s[...] = dt.T
    wstt_s[...] = (dt * jnp.exp(tot - acs)).T
    bm_t = bm_ref[bi].T
    bmt_s[...] = bm_t
    bmtb_s[...] = bm_t.astype(BF16)
    return etot


def _ssd_group(g, etot, bi, xs_ref, bm_ref, cm_ref, dt_ref, dtb_ref, nega_ref, cum_ref, mask_ref, y_ref,
               h_ref, acs_s, acst_s, dtt_s, wstt_s, bmt_s, bmtb_s):
    ln = SSM_CHUNK
    glane = lax.broadcasted_iota(jnp.int32, (ln, SSM_BC), 1) // SSM_STATE
    lo = lax.broadcasted_iota(jnp.int32, (ln, LANES), 1) < SSM_HEAD_DIM
    lo_1 = lax.broadcasted_iota(jnp.int32, (1, LANES), 1) < SSM_HEAD_DIM
    heads_per_group = SSM_HEADS // SSM_GROUPS
    group_w = heads_per_group * SSM_HEAD_DIM
    cm_b = cm_ref[bi].astype(BF16)
    cb = jnp.dot(jnp.where(glane == g, cm_b, jnp.zeros_like(cm_b)), bmtb_s[...],
                 preferred_element_type=F32)
    glanes = slice(g * group_w, (g + 1) * group_w)
    y_off = jnp.dot(cm_b, h_ref[:, glanes].astype(BF16), preferred_element_type=F32)
    rows = slice(g * SSM_STATE, (g + 1) * SSM_STATE)
    for j in range(heads_per_group // 2):
        pair = g * (heads_per_group // 2) + j
        lanes = slice(pair * LANES, (pair + 1) * LANES)
        x = xs_ref[bi, :, lanes]
        x2 = jnp.concatenate([jnp.where(lo, x, 0.0).astype(BF16), jnp.where(lo, 0.0, x).astype(BF16)], 0)
        mats, ecol, wsts, et = [], [], [], []
        for hh in range(2):
            h = 2 * pair + hh
            colb = jnp.broadcast_to(acs_s[:, h:h + 1], (ln, LANES))
            decay = jnp.exp(colb - acst_s[h:h + 1, :] + mask_ref[...])
            mats.append((cb * decay * dtt_s[h:h + 1, :]).astype(BF16))
            ecol.append(jnp.exp(colb))
            wsts.append((bmt_s[rows, :] * wstt_s[h:h + 1, :]).astype(BF16))
            et.append(jnp.broadcast_to(etot[:, h:h + 1], (1, LANES)))
        diag = jnp.dot(jnp.concatenate(mats, 1), x2, preferred_element_type=F32)
        st = jnp.dot(jnp.concatenate(wsts, 1), x2, preferred_element_type=F32)
        y_ref[bi, :, lanes] = diag + y_off[:, j * LANES:(j + 1) * LANES] * jnp.where(lo, ecol[0], ecol[1])
        h_ref[rows, lanes] = h_ref[rows, lanes] * jnp.where(lo_1, et[0], et[1]) + st
        yield


def _ssd(xbc, a_grp, dt_bias, neg_a, nb=2):
    bsz, seq, _ = xbc.shape
    nc = seq // SSM_CHUNK
    nb = nb if bsz % nb == 0 else 1

    def direction(reverse):
        dt_tile = 7 if reverse else 6
        row = 1 if reverse else 0

        def cidx(c):
            return (nc - 1 - c) if reverse else c

        ins = [pl.BlockSpec((nb, SSM_CHUNK, SSM_D_INNER), lambda b, c: (b, cidx(c), 0)),
               pl.BlockSpec((nb, SSM_CHUNK, SSM_BC), lambda b, c: (b, cidx(c), SSM_D_INNER // SSM_BC)),
               pl.BlockSpec((nb, SSM_CHUNK, SSM_BC), lambda b, c: (b, cidx(c), SSM_D_INNER // SSM_BC + 1)),
               pl.BlockSpec((nb, SSM_CHUNK, LANES), lambda b, c: (b, cidx(c), dt_tile)),
               pl.BlockSpec((1, LANES), lambda b, c: (0, 0)),
               pl.BlockSpec((1, LANES), lambda b, c: (0, 0)),
               pl.BlockSpec((SSM_CHUNK, 3 * SSM_CHUNK), lambda b, c: (0, 0)),
               pl.BlockSpec((SSM_CHUNK, SSM_CHUNK), lambda b, c: (0, 0))]
        out = pl.BlockSpec((nb, SSM_CHUNK, SSM_D_INNER), lambda b, c: (b, cidx(c), 0))
        li = np.arange(SSM_CHUNK)
        keep = (li[None, :] >= li[:, None]) if reverse else (li[None, :] <= li[:, None])
        cum = jnp.asarray(np.tile(keep.astype(np.float32), (1, 3)), BF16)
        mask = jnp.asarray(np.where(keep, 0.0, -np.inf).astype(np.float32))
        return ins, out, (xbc, xbc, xbc, a_grp, dt_bias[row:row + 1], neg_a[row:row + 1], cum, mask)

    f_ins, f_out, f_args = direction(False)
    b_ins, b_out, b_args = direction(True)
    y = jax.ShapeDtypeStruct((bsz, seq, SSM_D_INNER), F32)
    per_chain = [pltpu.VMEM((SSM_BC, SSM_D_INNER), F32),
                 pltpu.VMEM((SSM_CHUNK, LANES), F32),
                 pltpu.VMEM((LANES, SSM_CHUNK), F32),
                 pltpu.VMEM((LANES, SSM_CHUNK), F32),
                 pltpu.VMEM((LANES, SSM_CHUNK), F32),
                 pltpu.VMEM((SSM_BC, SSM_CHUNK), F32),
                 pltpu.VMEM((SSM_BC, SSM_CHUNK), BF16)]
    return pl.pallas_call(
        functools.partial(_ssd_kernel, nb=nb),
        grid=(bsz // nb, nc),
        in_specs=f_ins + b_ins,
        out_specs=[f_out, b_out],
        out_shape=[y, y],
        scratch_shapes=per_chain * (2 * nb),
        compiler_params=_params("parallel", "arbitrary"),
        name="ssd",
    )(*f_args, *b_args)


def _layernorm(x, g, b, eps=1e-5):
    mu = jnp.mean(x, -1, keepdims=True)
    xc = x - mu
    var = jnp.mean(xc * xc, -1, keepdims=True)
    return xc * lax.rsqrt(var + eps) * g + b


def _merge_kernel(x_ref, o_ref, yf_ref, yb_ref, xs_ref, z_ref, u_ref, gl_ref,
                  wa_ref, ws_ref, wc_ref, wo_ref, dsk_ref, ng_ref, cg_ref, cb_ref, bc_ref,
                  l1g_ref, l1b_ref, rw_ref, rb_ref,
                  x1_ref, x1g_ref, route_ref, cnt_ref, carry_ref, *, sub):
    tm = x_ref.shape[0]

    @pl.when(pl.program_id(0) == 0)
    def _():
        carry_ref[...] = jnp.zeros_like(carry_ref)

    def rows_of(first):
        rows = pl.ds(first, sub)
        y_attn = jnp.dot(o_ref[rows, :], wa_ref[...], preferred_element_type=F32)
        z = z_ref[rows, :].astype(F32)
        ys = (yf_ref[rows, :] + yb_ref[rows, :] + xs_ref[rows, :] * dsk_ref[...]) * (z * _sigmoid(z))
        gw = SSM_D_INNER // SSM_GROUPS
        ys = jnp.concatenate(
            [_rms(ys[:, g * gw:(g + 1) * gw], ng_ref[:, g * gw:(g + 1) * gw]) for g in range(SSM_GROUPS)], -1)
        yield
        y_ssm = jnp.dot(ys.astype(BF16), ws_ref[...], preferred_element_type=F32)
        uc = _layernorm(u_ref[rows, :], cg_ref[...], cb_ref[...])
        uc = uc * _sigmoid(uc)
        yield
        y_conv = jnp.dot(uc.astype(BF16), wc_ref[...], preferred_element_type=F32) + bc_ref[...]
        gl = gl_ref[rows, :].astype(F32)
        mixed = (_sigmoid(gl[:, :D_MODEL]) * y_attn
                 + _sigmoid(gl[:, D_MODEL:2 * D_MODEL]) * y_ssm
                 + _sigmoid(gl[:, 2 * D_MODEL:]) * y_conv)
        yield
        mixed = jnp.dot(mixed.astype(BF16), wo_ref[...], preferred_element_type=F32)
        x1 = _layernorm(DN_ALPHA * x_ref[rows, :] + mixed, l1g_ref[...], l1b_ref[...])
        x1_ref[rows, :] = x1
        for t in range(ROW_TILES):
            x1g_ref[pl.ds(first * ROW_TILES + t, sub, stride=ROW_TILES), :] = x1[:, t * LANES:(t + 1) * LANES]
        yield
        x_hi = x1.astype(BF16)
        x_lo = (x1 - x_hi.astype(F32)).astype(BF16)
        lg = (jnp.dot(jnp.concatenate([x_hi, x_lo, x_hi], -1), rw_ref[...], preferred_element_type=F32)
              + rb_ref[...])
        lane = lax.broadcasted_iota(jnp.int32, (sub, LANES), 1).astype(F32)
        sels, vals, idxs = [], [], []
        for _ in range(TOP_K):
            m = jnp.max(lg, -1, keepdims=True)
            idx = jnp.min(jnp.where(lg == m, lane, float(LANES)), -1, keepdims=True)
            sel = lane == idx
            lg = jnp.where(sel, NEG_BIG * 2, lg)
            sels.append(sel)
            vals.append(m)
            idxs.append(idx)
        yield
        es = [jnp.exp(v - vals[0]) for v in vals]
        den = es[0] + es[1] + es[2] + es[3]
        hot = jnp.zeros((sub, LANES), F32)
        for sel in sels:
            hot = hot + sel.astype(F32)
        r = lax.broadcasted_iota(jnp.int32, (sub, sub), 0)
        c = lax.broadcasted_iota(jnp.int32, (sub, sub), 1)
        excl = jnp.dot((c < r).astype(BF16), hot.astype(BF16), preferred_element_type=F32) + carry_ref[...]
        carry_ref[...] = carry_ref[...] + jnp.sum(hot, 0, keepdims=True)
        route = jnp.zeros((sub, LANES), F32)
        for kk in range(TOP_K):
            rank = jnp.sum(jnp.where(sels[kk], excl, 0.0), -1, keepdims=True)
            route = jnp.where(lane == float(kk), idxs[kk], route)
            route = jnp.where(lane == float(TOP_K + kk), rank, route)
            route = jnp.where(lane == float(2 * TOP_K + kk), es[kk] / den, route)
        route_ref[rows, :] = route
        yield

    for _ in zip(*[rows_of(first) for first in range(0, tm, sub)]):
        pass
    cnt_ref[...] = carry_ref[...]


def _merge(x, o, yf, yb, xbc, z, u, gl, wa, ws, wc, wo, dsk, ng, cg, cb, bc, l1g, l1b, rw, rb, tm=512, sub=256):
    t = x.shape[0]
    tm = min(tm, t)

    def rowb(width, col=0):
        return pl.BlockSpec((tm, width), lambda i: (i, col))

    def const(shape):
        return pl.BlockSpec(shape, lambda i: (0, 0))

    return pl.pallas_call(
        functools.partial(_merge_kernel, sub=min(sub, tm)),
        grid=(t // tm,),
        in_specs=[rowb(D_MODEL), rowb(MLA_HEADS * MLA_DV), rowb(SSM_D_INNER), rowb(SSM_D_INNER), rowb(SSM_D_INNER),
                  rowb(SSM_D_INNER), rowb(CNV_CH), rowb(3 * D_MODEL),
                  const((MLA_HEADS * MLA_DV, D_MODEL)), const((SSM_D_INNER, D_MODEL)), const((CNV_CH, D_MODEL)),
                  const((D_MODEL, D_MODEL)), const((1, SSM_D_INNER)), const((1, SSM_D_INNER)),
                  const((1, CNV_CH)), const((1, CNV_CH)), const((1, D_MODEL)),
                  const((1, D_MODEL)), const((1, D_MODEL)), const((3 * D_MODEL, LANES)), const((1, LANES))],
        out_specs=[rowb(D_MODEL),
                   pl.BlockSpec((tm * ROW_TILES, LANES), lambda i: (i, 0)),
                   rowb(LANES), const((1, LANES))],
        out_shape=[jax.ShapeDtypeStruct((t, D_MODEL), F32),
                   jax.ShapeDtypeStruct((t * ROW_TILES, LANES), F32),
                   jax.ShapeDtypeStruct((t, LANES), F32), jax.ShapeDtypeStruct((1, LANES), F32)],
        scratch_shapes=[pltpu.VMEM((1, LANES), F32)],
        compiler_params=_params("arbitrary"),
        name="merge",
    )(x, o, yf, yb, xbc, z, u, gl, wa, ws, wc, wo, dsk, ng, cg, cb, bc, l1g, l1b, rw, rb)


def _row_slice(ref, row):
    return ref.at[pl.ds(pl.multiple_of(row * ROW_TILES, ROW_TILES), ROW_TILES)]


def _from_row_tiles(ref, first, rows, pitch):
    return jnp.concatenate([ref[pl.ds(first + t, rows, stride=pitch), :] for t in range(ROW_TILES)], -1)


def _fill_padding(fill_start_ref, fill_n_ref, zero_ref, xs_hbm, sem):
    zero_ref[...] = jnp.zeros_like(zero_ref)
    bits = [1 << b for b in reversed(range((MOE_ROWS - 1).bit_length()))]

    def copies(e):
        n_e = fill_n_ref[e]
        for bit in bits:
            done = n_e & ~(2 * bit - 1)
            yield (n_e & bit) != 0, pltpu.make_async_copy(
                zero_ref.at[pl.ds(0, bit * ROW_TILES)],
                xs_hbm.at[pl.ds(pl.multiple_of((fill_start_ref[e] + done) * ROW_TILES, ROW_TILES),
                                bit * ROW_TILES)], sem)

    def start(e, carry):
        for cond, cp in copies(e):
            pl.when(cond)(cp.start)
        return carry

    def wait(e, carry):
        for cond, cp in copies(e):
            pl.when(cond)(cp.wait)
        return carry

    def tail(j):
        first = (fill_start_ref[N_EXPERTS] * ROW_TILES + j * zero_ref.shape[0])
        return pltpu.make_async_copy(zero_ref, xs_hbm.at[pl.ds(pl.multiple_of(first, ROW_TILES),
                                                               zero_ref.shape[0])], sem)

    def tail_start(j, carry):
        tail(j).start()
        return carry

    def tail_wait(j, carry):
        tail(j).wait()
        return carry

    lax.fori_loop(0, N_EXPERTS, start, 0)
    lax.fori_loop(0, fill_n_ref[N_EXPERTS], tail_start, 0)
    lax.fori_loop(0, N_EXPERTS, wait, 0)
    lax.fori_loop(0, fill_n_ref[N_EXPERTS], tail_wait, 0)


def _dispatch_kernel(fill_start_ref, fill_n_ref, dest_ref, x_ref, xs_hbm, sem, fill_sem, zero_ref, *, tm):
    @pl.when(pl.program_id(0) == pl.num_programs(0) - 1)
    def _():
        _fill_padding(fill_start_ref, fill_n_ref, zero_ref, xs_hbm, fill_sem)

    def body(c, carry):
        for u in range(ISSUE_UNROLL):
            t = c * ISSUE_UNROLL + u
            src = _row_slice(x_ref, t)
            for kk in range(TOP_K):
                pltpu.make_async_copy(src, _row_slice(xs_hbm, dest_ref[0, 0, t * TOP_K + kk]),
                                      sem).start(priority=kk % 2)
        return carry

    lax.fori_loop(0, tm // ISSUE_UNROLL, body, 0)
    for _ in range(TOP_K):
        pltpu.make_async_copy(x_ref, xs_hbm.at[pl.ds(0, tm * ROW_TILES)], sem).wait()


def _dispatch(dest, fill_start, fill_n, x1g, n_rows, tm=1024):
    t = x1g.shape[0] // ROW_TILES
    tm = min(tm, t)
    nt = t // tm
    grid_spec = pltpu.PrefetchScalarGridSpec(
        num_scalar_prefetch=2,
        grid=(nt,),
        in_specs=[pl.BlockSpec((1, 1, tm * TOP_K), lambda i, fs, fn: (i, 0, 0), memory_space=pltpu.SMEM),
                  pl.BlockSpec((tm * ROW_TILES, LANES), lambda i, fs, fn: (i, 0))],
        out_specs=pl.BlockSpec(memory_space=pl.ANY),
        scratch_shapes=[pltpu.SemaphoreType.DMA(()), pltpu.SemaphoreType.DMA(()),
                        pltpu.VMEM((MOE_ROWS // 2 * ROW_TILES, LANES), F32)],
    )
    return pl.pallas_call(
        functools.partial(_dispatch_kernel, tm=tm),
        grid_spec=grid_spec,
        out_shape=jax.ShapeDtypeStruct((n_rows * ROW_TILES, LANES), F32),
        compiler_params=_params("arbitrary"),
        name="dispatch",
    )(fill_start, fill_n, dest.reshape(nt, 1, tm * TOP_K), x1g)


def _moe_kernel(blk_e_ref, nused_ref, x_ref, wgu_ref, bgu_ref, wdn_ref, bdn_ref, o_ref, wgu_b, wdn_b):
    i = pl.program_id(0)

    @pl.when((i == 0) | (blk_e_ref[i] != blk_e_ref[jnp.maximum(i - 1, 0)]))
    def _():
        wgu_b[...] = wgu_ref[0, 0].astype(BF16)
        wdn_b[...] = wdn_ref[0, 0].astype(BF16)

    @pl.when(i < nused_ref[0])
    def _():
        xb = _from_row_tiles(x_ref, 0, MOE_ROWS, ROW_TILES).astype(BF16)
        h = jnp.dot(xb, wgu_b[...], preferred_element_type=F32) + bgu_ref[0]
        gate = jnp.minimum(h[:, :D_FF], SWIGLU_LIMIT)
        up = jnp.clip(h[:, D_FF:], -SWIGLU_LIMIT, SWIGLU_LIMIT)
        act = (up + 1.0) * (gate * _sigmoid(SWIGLU_ALPHA * gate))
        y = jnp.dot(act.astype(BF16), wdn_b[...], preferred_element_type=F32) + bdn_ref[0]
        for t in range(ROW_TILES):
            o_ref[pl.ds(t, MOE_ROWS, stride=ROW_TILES), :] = y[:, t * LANES:(t + 1) * LANES]

    @pl.when(i >= nused_ref[0])
    def _():
        o_ref[...] = jnp.zeros_like(o_ref)


def _moe_experts(blk_e, nused, xs, wgu, bgu, wdn, bdn, layer):
    n_blocks = blk_e.shape[0]
    grid_spec = pltpu.PrefetchScalarGridSpec(
        num_scalar_prefetch=2,
        grid=(n_blocks,),
        in_specs=[pl.BlockSpec((MOE_ROWS * ROW_TILES, LANES), lambda i, be, nu: (jnp.minimum(i, nu[0] - 1), 0)),
                  pl.BlockSpec((1, 1, D_MODEL, 2 * D_FF), lambda i, be, nu: (layer, be[i], 0, 0)),
                  pl.BlockSpec((1, 1, 2 * D_FF), lambda i, be, nu: (be[i], 0, 0)),
                  pl.BlockSpec((1, 1, D_FF, D_MODEL), lambda i, be, nu: (layer, be[i], 0, 0)),
                  pl.BlockSpec((1, 1, D_MODEL), lambda i, be, nu: (be[i], 0, 0))],
        out_specs=pl.BlockSpec((MOE_ROWS * ROW_TILES, LANES), lambda i, be, nu: (i, 0)),
        scratch_shapes=[pltpu.VMEM((D_MODEL, 2 * D_FF), BF16), pltpu.VMEM((D_FF, D_MODEL), BF16)],
    )
    return pl.pallas_call(
        _moe_kernel,
        grid_spec=grid_spec,
        out_shape=jax.ShapeDtypeStruct(xs.shape, F32),
        compiler_params=_params("arbitrary"),
        name="moe_experts",
    )(blk_e, nused, xs, wgu, bgu, wdn, bdn)


def _gather_topk_rows(dest_ref, yb_hbm, dst, sem, tm):
    def body(c, carry):
        for u in range(ISSUE_UNROLL):
            t = c * ISSUE_UNROLL + u
            for kk in range(TOP_K):
                pltpu.make_async_copy(_row_slice(yb_hbm, dest_ref[0, 0, t * TOP_K + kk]),
                                      _row_slice(dst, kk * tm + t), sem).start(priority=kk % 2)
        return carry
    lax.fori_loop(0, tm // ISSUE_UNROLL, body, 0)


def _combine_kernel(dest_first_ref, dest_next_ref, yb_hbm, x1_ref, route_ref, g_ref, b_ref, x2_ref, x2b_ref,
                    buf, sem, *, tm):
    i = pl.program_id(0)
    n = pl.num_programs(0)
    slot = i % 2

    @pl.when(i == 0)
    def _():
        _gather_topk_rows(dest_first_ref, yb_hbm, buf.at[0], sem.at[0], tm)

    @pl.when(i + 1 < n)
    def _():
        _gather_topk_rows(dest_next_ref, yb_hbm, buf.at[1 - slot], sem.at[1 - slot], tm)

    pltpu.make_async_copy(yb_hbm.at[pl.ds(0, tm * TOP_K * ROW_TILES)], buf.at[slot], sem.at[slot]).wait()
    route = route_ref[...]
    ffn = jnp.zeros((tm, D_MODEL), F32)
    for kk in range(TOP_K):
        rows = _from_row_tiles(buf.at[slot], kk * tm * ROW_TILES, tm, ROW_TILES)
        ffn = ffn + rows * route[:, 2 * TOP_K + kk:2 * TOP_K + kk + 1]
    x2 = _layernorm(DN_ALPHA * x1_ref[...] + ffn, g_ref[...], b_ref[...])
    x2_ref[...] = x2
    x2b_ref[...] = x2.astype(BF16)


def _combine(dest, yb, x1, route, g, b, tm=512):
    t = x1.shape[0]
    tm = min(tm, t)
    nt = t // tm
    dest3 = dest.reshape(nt, 1, tm * TOP_K)
    return pl.pallas_call(
        functools.partial(_combine_kernel, tm=tm),
        grid=(nt,),
        in_specs=[pl.BlockSpec((1, 1, tm * TOP_K), lambda i: (0, 0, 0), memory_space=pltpu.SMEM),
                  pl.BlockSpec((1, 1, tm * TOP_K), lambda i: (jnp.minimum(i + 1, nt - 1), 0, 0),
                               memory_space=pltpu.SMEM),
                  pl.BlockSpec(memory_space=pl.ANY),
                  pl.BlockSpec((tm, D_MODEL), lambda i: (i, 0)),
                  pl.BlockSpec((tm, LANES), lambda i: (i, 0)),
                  pl.BlockSpec((1, D_MODEL), lambda i: (0, 0)),
                  pl.BlockSpec((1, D_MODEL), lambda i: (0, 0))],
        out_specs=[pl.BlockSpec((tm, D_MODEL), lambda i: (i, 0)),
                   pl.BlockSpec((tm, D_MODEL), lambda i: (i, 0))],
        out_shape=[jax.ShapeDtypeStruct((t, D_MODEL), F32), jax.ShapeDtypeStruct((t, D_MODEL), BF16)],
        scratch_shapes=[pltpu.VMEM((2, tm * TOP_K * ROW_TILES, LANES), F32),
                        pltpu.SemaphoreType.DMA((2,))],
        compiler_params=_params("arbitrary"),
        name="combine",
    )(dest3, dest3, yb, x1, route, g, b)


def _pad_cols(w, width):
    return jnp.pad(w, ((0, 0), (0, width - w.shape[1])))


def _pack_in_proj(w_in, b_in):
    offs = np.concatenate([[0], np.cumsum(IN_SIZES)])
    wb = jnp.concatenate([w_in, b_in[None, :]], 0)

    def piece(i):
        return wb[:, offs[i]:offs[i + 1]]

    kr = piece(2)
    kr_swapped = jnp.concatenate([kr[:, MLA_DR // 2:], kr[:, :MLA_DR // 2]], 1)
    dt = piece(5)
    grp_a = jnp.concatenate([piece(0), piece(1), _pad_cols(jnp.concatenate([kr, kr_swapped], 1), LANES),
                             _pad_cols(dt[:, :SSM_HEADS], LANES), _pad_cols(dt[:, SSM_HEADS:], LANES)], 1)
    groups = {"a": (grp_a, F32), "z": (piece(3), BF16), "xbc": (piece(4), F32),
              "cnv": (jnp.concatenate([piece(6), piece(7)], 1), F32), "gate": (piece(8), BF16)}
    return {k: (v[:-1].astype(BF16), v[-1:], dt_) for k, (v, dt_) in groups.items()}


def _pack_mla(w_uq, w_ukv, w_br_attn, seq):
    hq = MLA_DN + MLA_DR
    wq = w_uq.reshape(MLA_Q_LORA, MLA_HEADS, hq)
    zq = jnp.zeros((MLA_Q_LORA, MLA_HEADS, HEAD_PAD - hq), F32)
    wq_main = jnp.concatenate([wq, zq], -1).reshape(MLA_Q_LORA, MLA_W)
    wkv = w_ukv.reshape(MLA_KV_LORA, MLA_HEADS, MLA_DN + MLA_DV)
    zk = jnp.zeros((MLA_KV_LORA, MLA_HEADS, HEAD_PAD - MLA_DN), F32)
    wk = jnp.concatenate([wkv[..., :MLA_DN], zk], -1).reshape(MLA_KV_LORA, MLA_W)
    wv = jnp.concatenate([wkv[..., MLA_DN:], zk], -1).reshape(MLA_KV_LORA, MLA_W)
    e2 = np.zeros((LANES, MLA_HEADS, HEAD_PAD), np.float32)
    for j in range(MLA_DR):
        e2[j, :, MLA_DN + j] = 1.0
        e2[MLA_DR + j, :, MLA_DN + j] = 1.0
    vone = np.zeros((MLA_HEADS, HEAD_PAD), np.float32)
    vone[:, MLA_DV] = 1.0
    pos = jnp.arange(seq, dtype=F32)
    inv = ROPE_THETA ** (-jnp.arange(0, MLA_DR, 2, dtype=F32) / MLA_DR)
    ang = pos[:, None] * inv[None, :]
    cos, sin = jnp.cos(ang), jnp.sin(ang)
    scale = (MLA_DN + MLA_DR) ** -0.5 * math.log2(math.e)
    ones = jnp.ones((seq, MLA_DN), F32)
    zpad = jnp.zeros((seq, HEAD_PAD - hq), F32)
    cq = jnp.tile(jnp.concatenate([ones, cos, cos, zpad], 1) * scale, (1, MLA_HEADS))
    sq = jnp.tile(jnp.concatenate([0 * ones, -sin, sin, zpad], 1) * scale, (1, MLA_HEADS))
    tk = jnp.concatenate([cos, cos, -sin, sin, jnp.zeros((seq, LANES - 2 * MLA_DR), F32)], 1)
    return dict(wq=wq_main.astype(BF16), wk=wk.astype(BF16), wv=wv.astype(BF16),
                e2=jnp.asarray(e2.reshape(LANES, MLA_W), BF16), vone=jnp.asarray(vone.reshape(1, MLA_W)),
                wbr=w_br_attn.astype(BF16), cq=cq, sq=sq, tk=tk)


def _route_tables(route, cnt, n_tok):
    idx = route[:, :TOP_K].astype(jnp.int32)
    rank = route[:, TOP_K:2 * TOP_K].astype(jnp.int32)
    counts = cnt[0, :N_EXPERTS].astype(jnp.int32)
    n_blocks = -(-(n_tok * TOP_K + N_EXPERTS * (MOE_ROWS - 1)) // MOE_ROWS)
    padded = (counts + MOE_ROWS - 1) // MOE_ROWS * MOE_ROWS
    pad_end = jnp.cumsum(padded)
    pad_start = pad_end - padded
    onehot = idx[..., None] == jnp.arange(N_EXPERTS, dtype=jnp.int32)
    dest = (jnp.sum(jnp.where(onehot, pad_start, 0), -1) + rank).reshape(-1)
    blk_first = jnp.arange(n_blocks, dtype=jnp.int32) * MOE_ROWS
    blk_e = jnp.minimum(jnp.sum((pad_end[None, :] <= blk_first[:, None]).astype(jnp.int32), -1), N_EXPERTS - 1)
    nused = pad_end[-1:] // MOE_ROWS
    n_rows = n_blocks * MOE_ROWS
    fill_start = jnp.concatenate([pad_start + counts, pad_end[-1:]])
    fill_n = jnp.concatenate([padded - counts, (n_rows - pad_end[-1:]) // (MOE_ROWS // 2)])
    return dest, blk_e, nused, fill_start, fill_n, n_rows


def kernel(x, w_in, b_in, mla_q_norm, mla_kv_norm, mla_w_uq, mla_w_ukv, w_br_attn, ssm_conv_w, ssm_conv_b, ssm_dt_bias, ssm_a_log, ssm_d, ssm_norm, w_br_ssm, cnv_dw_w, cnv_dw_b, cnv_ln_g, cnv_ln_b, w_br_conv, b_br_conv, w_out, ln1_g, ln1_b, router_w, router_b, moe_w_gate_up, moe_b_gate_up, moe_w_down, moe_b_down, ln2_g, ln2_b):
    bsz, seq, d = x.shape
    n_tok = bsz * seq
    xf = x.reshape(n_tok, d)
    xb = xf.astype(BF16)
    for l in range(DEPTH):
        proj = _pack_in_proj(w_in[l], b_in[l])
        mla = _pack_mla(mla_w_uq[l], mla_w_ukv[l], w_br_attn[l], seq)
        a_grp = _linear(xb, *proj["a"])
        z = _linear(xb, *proj["z"])
        xbc_raw = _linear(xb, *proj["xbc"], tn=768)
        cnv_raw = _linear(xb, *proj["cnv"])
        gate_logits = _linear(xb, *proj["gate"], tn=1024)

        q, k, v = _mla_prep(a_grp, seq, mla_q_norm[l][None], mla_kv_norm[l][None], mla["wq"],
                            mla["wk"], mla["wv"], mla["e2"], mla["cq"], mla["sq"], mla["tk"], mla["vone"])
        attn = _attention(q.reshape(bsz, seq, MLA_W), k.reshape(bsz, seq, MLA_W), v.reshape(bsz, seq, MLA_W))

        xbc = _dwconv(xbc_raw.reshape(bsz, seq, SSM_CONV_DIM), ssm_conv_w[l], ssm_conv_b[l],
                      glu=False, silu_out=True, rows=256)
        a3 = a_grp.reshape(bsz, seq, GROUP_A)
        dtb = jnp.pad(ssm_dt_bias[l], ((0, 0), (0, LANES - SSM_HEADS)))
        nega = jnp.pad(-jnp.exp(ssm_a_log[l]), ((0, 0), (0, LANES - SSM_HEADS)))
        y_fwd, y_bwd = _ssd(xbc, a3, dtb, nega)

        u = _dwconv(cnv_raw.reshape(bsz, seq, 2 * CNV_CH), cnv_dw_w[l], cnv_dw_b[l], glu=True, silu_out=False,
                    rows=128)

        rw = jnp.pad(router_w[l], ((0, 0), (0, LANES - N_EXPERTS)))
        rw_hi = rw.astype(BF16)
        rw_lo = (rw - rw_hi.astype(F32)).astype(BF16)
        rw = jnp.concatenate([rw_hi, rw_hi, rw_lo], 0)
        rb = jnp.pad(router_b[l], (0, LANES - N_EXPERTS), constant_values=NEG_BIG)[None]
        x1, x1g, route, cnt = _merge(
            xf, attn.reshape(n_tok, MLA_HEADS * MLA_DV), y_fwd.reshape(n_tok, SSM_D_INNER), y_bwd.reshape(n_tok, SSM_D_INNER),
            xbc.reshape(n_tok, SSM_CONV_DIM), z, u.reshape(n_tok, CNV_CH), gate_logits,
            mla["wbr"], w_br_ssm[l].astype(BF16), w_br_conv[l].astype(BF16), w_out[l].astype(BF16),
            jnp.repeat(ssm_d[l], SSM_HEAD_DIM)[None], ssm_norm[l][None], cnv_ln_g[l][None], cnv_ln_b[l][None],
            b_br_conv[l][None], ln1_g[l][None], ln1_b[l][None], rw, rb)

        dest, blk_e, nused, fill_start, fill_n, n_rows = _route_tables(route, cnt, n_tok)
        xs = _dispatch(dest, fill_start, fill_n, x1g, n_rows)
        yb = _moe_experts(blk_e, nused, xs, moe_w_gate_up, moe_b_gate_up[l][:, None, :], moe_w_down,
                          moe_b_down[l][:, None, :], l)
        xf, xb = _combine(dest, yb, x1, route, ln2_g[l][None], ln2_b[l][None])
    return xf.reshape(bsz, seq, d)
```

```python
import functools
import math

import numpy as np
import jax
import jax.numpy as jnp
from jax import lax
from jax.experimental import pallas as pl
from jax.experimental.pallas import tpu as pltpu

F32 = jnp.float32
BF16 = jnp.bfloat16
HIGHEST = lax.Precision.HIGHEST

LANES = 128
SUBLANES = 8
VMEM_LIMIT_BYTES = 56 * 1024 * 1024

D_MODEL = 1024
DEPTH = 2
MLA_HEADS = 8
MLA_Q_LORA = 384
MLA_KV_LORA = 256
MLA_DN = 64
MLA_DR = 32
MLA_DV = 64
ROPE_THETA = 10000.0
SSM_HEADS = 16
SSM_HEAD_DIM = 64
SSM_D_INNER = SSM_HEADS * SSM_HEAD_DIM
SSM_GROUPS = 4
SSM_STATE = 64
SSM_CONV = 5
SSM_CHUNK = 128
SSM_BC = SSM_GROUPS * SSM_STATE
SSM_CONV_DIM = SSM_D_INNER + 2 * SSM_BC
CNV_CH = 512
CNV_WIDTH = 31
N_EXPERTS = 32
TOP_K = 4
D_FF = 1024
SWIGLU_LIMIT = 7.0
SWIGLU_ALPHA = 1.702
DN_ALPHA = (2 * DEPTH) ** 0.25
IN_SIZES = (MLA_Q_LORA, MLA_KV_LORA, MLA_DR, SSM_D_INNER, SSM_CONV_DIM, 2 * SSM_HEADS,
            CNV_CH, CNV_CH, 3 * D_MODEL)
HEAD_PAD = LANES
MLA_W = MLA_HEADS * HEAD_PAD
GROUP_A = 1024
CONV_HALO = 16
MOE_ROWS = 512
ATTN_AHEAD = 2
ISSUE_UNROLL = 8
ROW_TILES = D_MODEL // LANES
NEG_BIG = -1e30


def _params(*sem):
    return pltpu.CompilerParams(dimension_semantics=sem, vmem_limit_bytes=VMEM_LIMIT_BYTES)


def _sigmoid(x):
    return 0.5 * jnp.tanh(0.5 * x) + 0.5


def _linear_kernel(x_ref, w_ref, b_ref, o_ref):
    acc = jnp.dot(x_ref[...], w_ref[...], preferred_element_type=F32)
    o_ref[...] = (acc + b_ref[...]).astype(o_ref.dtype)


def _linear(x, w, b, out_dtype, tm=1024, tn=None):
    m, k = x.shape
    n = w.shape[1]
    tm = min(tm, m)
    tn = n if tn is None else tn
    return pl.pallas_call(
        _linear_kernel,
        grid=(n // tn, m // tm),
        in_specs=[pl.BlockSpec((tm, k), lambda j, i: (i, 0)),
                  pl.BlockSpec((k, tn), lambda j, i: (0, j)),
                  pl.BlockSpec((1, tn), lambda j, i: (0, j))],
        out_specs=pl.BlockSpec((tm, tn), lambda j, i: (i, j)),
        out_shape=jax.ShapeDtypeStruct((m, n), out_dtype),
        compiler_params=_params("parallel", "parallel"),
        name="linear",
    )(x, w, b)


def _rms(x, g, eps=1e-6):
    return x * lax.rsqrt(jnp.mean(x * x, -1, keepdims=True) + eps) * g


def _mla_prep_kernel(a_ref, gq_ref, gkv_ref, wq_ref, wk_ref, wv_ref, e2_ref,
                     cq_ref, sq_ref, tk_ref, vone_ref, q_ref, k_ref, v_ref, *, sub):
    def rows_of(first):
        rows = pl.ds(first, sub)
        a = a_ref[rows, :]
        c_q = a[:, :MLA_Q_LORA]
        c_kv = a[:, MLA_Q_LORA:MLA_Q_LORA + MLA_KV_LORA]
        kr = a[:, MLA_Q_LORA + MLA_KV_LORA:]
        qn = _rms(c_q, gq_ref[...]).astype(BF16)
        kvn = _rms(c_kv, gkv_ref[...]).astype(BF16)
        yield
        q = jnp.dot(qn, wq_ref[...], preferred_element_type=F32)
        half = MLA_DR // 2
        first_half = lax.broadcasted_iota(jnp.int32, q.shape, 1) % HEAD_PAD < MLA_DN + half
        swapped = jnp.where(first_half, pltpu.roll(q, MLA_W - half, 1), pltpu.roll(q, half, 1))
        q_ref[rows, :] = (q * cq_ref[rows, :] + swapped * sq_ref[rows, :]).astype(BF16)
        yield
        krp = (kr * tk_ref[rows, :]).astype(BF16)
        k = (jnp.dot(kvn, wk_ref[...], preferred_element_type=F32)
             + jnp.dot(krp, e2_ref[...], preferred_element_type=F32))
        k_ref[rows, :] = k.astype(BF16)
        yield
        v = jnp.dot(kvn, wv_ref[...], preferred_element_type=F32) + vone_ref[...]
        v_ref[rows, :] = v.astype(BF16)
        yield

    for _ in zip(*[rows_of(first) for first in range(0, a_ref.shape[0], sub)]):
        pass


def _mla_prep(a, seq, gq, gkv, wq, wk, wv, e2, cq, sq, tk, vone, tm=1024, sub=512):
    t = a.shape[0]
    tm = min(tm, seq)
    nper = seq // tm
    wa = MLA_Q_LORA + MLA_KV_LORA + LANES

    def const(shape):
        return pl.BlockSpec(shape, lambda i: (0, 0))

    def tab(width):
        return pl.BlockSpec((tm, width), lambda i: (i % nper, 0))

    out = jax.ShapeDtypeStruct((t, MLA_W), BF16)
    return pl.pallas_call(
        functools.partial(_mla_prep_kernel, sub=min(sub, tm)),
        grid=(t // tm,),
        in_specs=[pl.BlockSpec((tm, wa), lambda i: (i, 0)),
                  const((1, MLA_Q_LORA)), const((1, MLA_KV_LORA)),
                  const((MLA_Q_LORA, MLA_W)),
                  const((MLA_KV_LORA, MLA_W)), const((MLA_KV_LORA, MLA_W)),
                  const((LANES, MLA_W)),
                  tab(MLA_W), tab(MLA_W), tab(LANES), const((1, MLA_W))],
        out_specs=[pl.BlockSpec((tm, MLA_W), lambda i: (i, 0))] * 3,
        out_shape=[out, out, out],
        compiler_params=_params("parallel"),
        name="mla_prep",
    )(a, gq, gkv, wq, wk, wv, e2, cq, sq, tk, vone)


def _attn_kernel(q_ref, k_ref, v_ref, o_ref, *, sub):
    low = lax.broadcasted_iota(jnp.int32, (sub, HEAD_PAD), 1) < MLA_DV
    units = [(r, hh) for r in range(q_ref.shape[1] // sub) for hh in range(2)]

    def scores(unit):
        r, hh = unit
        lanes = slice(hh * HEAD_PAD, (hh + 1) * HEAD_PAD)
        return lax.dot_general(q_ref[0, r * sub:(r + 1) * sub, lanes], k_ref[0, :, lanes],
                               (((1,), (1,)), ((), ())), preferred_element_type=F32)

    ahead = [scores(u) for u in units[:ATTN_AHEAD]]
    outs = []
    for i, (r, hh) in enumerate(units):
        s = ahead.pop(0)
        if i + ATTN_AHEAD < len(units):
            ahead.append(scores(units[i + ATTN_AHEAD]))
        lanes = slice(hh * HEAD_PAD, (hh + 1) * HEAD_PAD)
        m = jnp.max(s, -1, keepdims=True)
        p = jnp.exp2(s - m).astype(BF16)
        o = jnp.dot(p, v_ref[0, :, lanes], preferred_element_type=F32)
        outs.append(o / o[:, MLA_DV:MLA_DV + 1])
        if hh == 1:
            o_ref[0, r * sub:(r + 1) * sub] = jnp.where(
                low, outs[0], pltpu.roll(outs[1], MLA_DV, 1)).astype(o_ref.dtype)
            outs = []


def _attention(q, k, v, tq=1024, sub=256):
    b, s, _ = q.shape
    tq = min(tq, s)
    pair = 2 * HEAD_PAD
    return pl.pallas_call(
        functools.partial(_attn_kernel, sub=min(sub, tq)),
        grid=(b, MLA_HEADS // 2, s // tq),
        in_specs=[pl.BlockSpec((1, tq, pair), lambda bi, h, i: (bi, i, h)),
                  pl.BlockSpec((1, s, pair), lambda bi, h, i: (bi, 0, h)),
                  pl.BlockSpec((1, s, pair), lambda bi, h, i: (bi, 0, h))],
        out_specs=pl.BlockSpec((1, tq, HEAD_PAD), lambda bi, h, i: (bi, i, h)),
        out_shape=jax.ShapeDtypeStruct((b, s, MLA_HEADS * MLA_DV), BF16),
        compiler_params=_params("parallel", "parallel", "parallel"),
        name="attention",
    )(q, k, v)


def _dwconv_kernel(*refs, width, glu, silu_out, seq, rows):
    if glu:
        a_ref, g_ref, w_ref, b_ref, o_ref, pad_ref = refs
        pre = a_ref[0] * _sigmoid(g_ref[0])
    else:
        x_ref, w_ref, b_ref, o_ref, pad_ref = refs
        pre = x_ref[0]
    ch = o_ref.shape[-1]
    halo = jnp.zeros((CONV_HALO, ch), F32)
    pad_ref[0:CONV_HALO, :] = halo
    pad_ref[CONV_HALO + seq:2 * CONV_HALO + seq, :] = halo
    pad_ref[CONV_HALO:CONV_HALO + seq, :] = pre
    half = (width - 1) // 2
    win_rows = rows + 2 * CONV_HALO

    def body(c, carry):
        base = pl.multiple_of(c * rows, rows)
        acc = jnp.zeros((rows, ch), F32) + b_ref[...]
        for t in range(width):
            acc = acc + pad_ref[pl.ds(base + (CONV_HALO - half + t), rows), :] * w_ref[t:t + 1, :]
        if silu_out:
            acc = acc * _sigmoid(acc)
        o_ref[0, pl.ds(base, rows), :] = acc
        return carry

    lax.fori_loop(0, seq // rows, body, 0)


def _dwconv(x, w, b, *, glu, silu_out, rows=64):
    bsz, seq, cin = x.shape
    width, ch = w.shape
    nct = ch // LANES
    kern = functools.partial(_dwconv_kernel, width=width, glu=glu, silu_out=silu_out, seq=seq,
                             rows=min(rows, seq))
    xspec = pl.BlockSpec((1, seq, LANES), lambda bi, j: (bi, 0, j))
    in_specs = [xspec]
    args = [x]
    if glu:
        in_specs.append(pl.BlockSpec((1, seq, LANES), lambda bi, j: (bi, 0, j + nct)))
        args.append(x)
    in_specs += [pl.BlockSpec((width, LANES), lambda bi, j: (0, j)),
                 pl.BlockSpec((1, LANES), lambda bi, j: (0, j))]
    return pl.pallas_call(
        kern,
        grid=(bsz, nct),
        in_specs=in_specs,
        out_specs=pl.BlockSpec((1, seq, LANES), lambda bi, j: (bi, 0, j)),
        out_shape=jax.ShapeDtypeStruct((bsz, seq, ch), F32),
        scratch_shapes=[pltpu.VMEM((seq + 2 * CONV_HALO, LANES), F32)],
        compiler_params=_params("parallel", "parallel"),
        name="dwconv",
    )(*args, w, b.reshape(1, ch))


def _softplus(x):
    return jnp.maximum(x, 0.0) + jnp.log1p(jnp.exp(-jnp.abs(x)))


def _ssd_kernel(*refs, nb):
    n_in = 8
    ins = (refs[0:n_in], refs[n_in:2 * n_in])
    ys = refs[2 * n_in:2 * n_in + 2]
    scratch = refs[2 * n_in + 2:]
    n_s = len(scratch) // (2 * nb)
    chains = [(d, bi, scratch[(d * nb + bi) * n_s:(d * nb + bi + 1) * n_s]) for bi in range(nb) for d in range(2)]

    @pl.when(pl.program_id(1) == 0)
    def _():
        for _, _, sc in chains:
            sc[0][...] = jnp.zeros_like(sc[0])

    etots = [_ssd_prep(bi, *ins[d], *sc, reverse=bool(d)) for d, bi, sc in chains]
    for g in range(SSM_GROUPS):
        gens = [_ssd_group(g, etot, bi, *ins[d], ys[d], *sc) for etot, (d, bi, sc) in zip(etots, chains)]
        for _ in zip(*gens):
            pass


def _ssd_prep(bi, xs_ref, bm_ref, cm_ref, dt_ref, dtb_ref, nega_ref, cum_ref, mask_ref,
              h_ref, acs_s, acst_s, dtt_s, wstt_s, bmt_s, bmtb_s, *, reverse):
    ln = SSM_CHUNK
    dt = _softplus(dt_ref[bi] + dtb_ref[...])
    a = dt * nega_ref[...]
    a_hi = a.astype(BF16)
    rem = a - a_hi.astype(F32)
    a_mid = rem.astype(BF16)
    a_lo = (rem - a_mid.astype(F32)).astype(BF16)
    acs = jnp.dot(cum_ref[...], jnp.concatenate([a_hi, a_mid, a_lo], 0), preferred_element_type=F32)
    tot = acs[0:1] if reverse else acs[ln - 1:ln]
    etot = jnp.exp(tot)
    acs_s[...] = acs
    acst_s[...] = acs.T
    dtt_s[...] = dt.T
    wstt_s[...] = (dt * jnp.exp(tot - acs)).T
    bm_t = bm_ref[bi].T
    bmt_s[...] = bm_t
    bmtb_s[...] = bm_t.astype(BF16)
    return etot


def _ssd_group(g, etot, bi, xs_ref, bm_ref, cm_ref, dt_ref, dtb_ref, nega_ref, cum_ref, mask_ref, y_ref,
               h_ref, acs_s, acst_s, dtt_s, wstt_s, bmt_s, bmtb_s):
    ln = SSM_CHUNK
    glane = lax.broadcasted_iota(jnp.int32, (ln, SSM_BC), 1) // SSM_STATE
    lo = lax.broadcasted_iota(jnp.int32, (ln, LANES), 1) < SSM_HEAD_DIM
    lo_1 = lax.broadcasted_iota(jnp.int32, (1, LANES), 1) < SSM_HEAD_DIM
    heads_per_group = SSM_HEADS // SSM_GROUPS
    group_w = heads_per_group * SSM_HEAD_DIM
    cm_b = cm_ref[bi].astype(BF16)
    cb = jnp.dot(jnp.where(glane == g, cm_b, jnp.zeros_like(cm_b)), bmtb_s[...],
                 preferred_element_type=F32)
    glanes = slice(g * group_w, (g + 1) * group_w)
    y_off = jnp.dot(cm_b, h_ref[:, glanes].astype(BF16), preferred_element_type=F32)
    rows = slice(g * SSM_STATE, (g + 1) * SSM_STATE)
    for j in range(heads_per_group // 2):
        pair = g * (heads_per_group // 2) + j
        lanes = slice(pair * LANES, (pair + 1) * LANES)
        x = xs_ref[bi, :, lanes]
        x2 = jnp.concatenate([jnp.where(lo, x, 0.0).astype(BF16), jnp.where(lo, 0.0, x).astype(BF16)], 0)
        mats, ecol, wsts, et = [], [], [], []
        for hh in range(2):
            h = 2 * pair + hh
            colb = jnp.broadcast_to(acs_s[:, h:h + 1], (ln, LANES))
            decay = jnp.exp(colb - acst_s[h:h + 1, :] + mask_ref[...])
            mats.append((cb * decay * dtt_s[h:h + 1, :]).astype(BF16))
            ecol.append(jnp.exp(colb))
            wsts.append((bmt_s[rows, :] * wstt_s[h:h + 1, :]).astype(BF16))
            et.append(jnp.broadcast_to(etot[:, h:h + 1], (1, LANES)))
        diag = jnp.dot(jnp.concatenate(mats, 1), x2, preferred_element_type=F32)
        st = jnp.dot(jnp.concatenate(wsts, 1), x2, preferred_element_type=F32)
        y_ref[bi, :, lanes] = diag + y_off[:, j * LANES:(j + 1) * LANES] * jnp.where(lo, ecol[0], ecol[1])
        h_ref[rows, lanes] = h_ref[rows, lanes] * jnp.where(lo_1, et[0], et[1]) + st
        yield


def _ssd(xbc, a_grp, dt_bias, neg_a, nb=2):
    bsz, seq, _ = xbc.shape
    nc = seq // SSM_CHUNK
    nb = nb if bsz % nb == 0 else 1

    def direction(reverse):
        dt_tile = 7 if reverse else 6
        row = 1 if reverse else 0

        def cidx(c):
            return (nc - 1 - c) if reverse else c

        ins = [pl.BlockSpec((nb, SSM_CHUNK, SSM_D_INNER), lambda b, c: (b, cidx(c), 0)),
               pl.BlockSpec((nb, SSM_CHUNK, SSM_BC), lambda b, c: (b, cidx(c), SSM_D_INNER // SSM_BC)),
               pl.BlockSpec((nb, SSM_CHUNK, SSM_BC), lambda b, c: (b, cidx(c), SSM_D_INNER // SSM_BC + 1)),
               pl.BlockSpec((nb, SSM_CHUNK, LANES), lambda b, c: (b, cidx(c), dt_tile)),
               pl.BlockSpec((1, LANES), lambda b, c: (0, 0)),
               pl.BlockSpec((1, LANES), lambda b, c: (0, 0)),
               pl.BlockSpec((SSM_CHUNK, 3 * SSM_CHUNK), lambda b, c: (0, 0)),
               pl.BlockSpec((SSM_CHUNK, SSM_CHUNK), lambda b, c: (0, 0))]
        out = pl.BlockSpec((nb, SSM_CHUNK, SSM_D_INNER), lambda b, c: (b, cidx(c), 0))
        li = np.arange(SSM_CHUNK)
        keep = (li[None, :] >= li[:, None]) if reverse else (li[None, :] <= li[:, None])
        cum = jnp.asarray(np.tile(keep.astype(np.float32), (1, 3)), BF16)
        mask = jnp.asarray(np.where(keep, 0.0, -np.inf).astype(np.float32))
        return ins, out, (xbc, xbc, xbc, a_grp, dt_bias[row:row + 1], neg_a[row:row + 1], cum, mask)

    f_ins, f_out, f_args = direction(False)
    b_ins, b_out, b_args = direction(True)
    y = jax.ShapeDtypeStruct((bsz, seq, SSM_D_INNER), F32)
    per_chain = [pltpu.VMEM((SSM_BC, SSM_D_INNER), F32),
                 pltpu.VMEM((SSM_CHUNK, LANES), F32),
                 pltpu.VMEM((LANES, SSM_CHUNK), F32),
                 pltpu.VMEM((LANES, SSM_CHUNK), F32),
                 pltpu.VMEM((LANES, SSM_CHUNK), F32),
                 pltpu.VMEM((SSM_BC, SSM_CHUNK), F32),
                 pltpu.VMEM((SSM_BC, SSM_CHUNK), BF16)]
    return pl.pallas_call(
        functools.partial(_ssd_kernel, nb=nb),
        grid=(bsz // nb, nc),
        in_specs=f_ins + b_ins,
        out_specs=[f_out, b_out],
        out_shape=[y, y],
        scratch_shapes=per_chain * (2 * nb),
        compiler_params=_params("parallel", "arbitrary"),
        name="ssd",
    )(*f_args, *b_args)


def _layernorm(x, g, b, eps=1e-5):
    mu = jnp.mean(x, -1, keepdims=True)
    xc = x - mu
    var = jnp.mean(xc * xc, -1, keepdims=True)
    return xc * lax.rsqrt(var + eps) * g + b


def _merge_kernel(x_ref, o_ref, yf_ref, yb_ref, xs_ref, z_ref, u_ref, gl_ref,
                  wa_ref, ws_ref, wc_ref, wo_ref, dsk_ref, ng_ref, cg_ref, cb_ref, bc_ref,
                  l1g_ref, l1b_ref, rw_ref, rb_ref,
                  x1_ref, x1g_ref, route_ref, cnt_ref, carry_ref, *, sub):
    tm = x_ref.shape[0]

    @pl.when(pl.program_id(0) == 0)
    def _():
        carry_ref[...] = jnp.zeros_like(carry_ref)

    def rows_of(first):
        rows = pl.ds(first, sub)
        y_attn = jnp.dot(o_ref[rows, :], wa_ref[...], preferred_element_type=F32)
        z = z_ref[rows, :].astype(F32)
        ys = (yf_ref[rows, :] + yb_ref[rows, :] + xs_ref[rows, :] * dsk_ref[...]) * (z * _sigmoid(z))
        gw = SSM_D_INNER // SSM_GROUPS
        ys = jnp.concatenate(
            [_rms(ys[:, g * gw:(g + 1) * gw], ng_ref[:, g * gw:(g + 1) * gw]) for g in range(SSM_GROUPS)], -1)
        yield
        y_ssm = jnp.dot(ys.astype(BF16), ws_ref[...], preferred_element_type=F32)
        uc = _layernorm(u_ref[rows, :], cg_ref[...], cb_ref[...])
        uc = uc * _sigmoid(uc)
        yield
        y_conv = jnp.dot(uc.astype(BF16), wc_ref[...], preferred_element_type=F32) + bc_ref[...]
        gl = gl_ref[rows, :].astype(F32)
        mixed = (_sigmoid(gl[:, :D_MODEL]) * y_attn
                 + _sigmoid(gl[:, D_MODEL:2 * D_MODEL]) * y_ssm
                 + _sigmoid(gl[:, 2 * D_MODEL:]) * y_conv)
        yield
        mixed = jnp.dot(mixed.astype(BF16), wo_ref[...], preferred_element_type=F32)
        x1 = _layernorm(DN_ALPHA * x_ref[rows, :] + mixed, l1g_ref[...], l1b_ref[...])
        x1_ref[rows, :] = x1
        for t in range(ROW_TILES):
            x1g_ref[pl.ds(first * ROW_TILES + t, sub, stride=ROW_TILES), :] = x1[:, t * LANES:(t + 1) * LANES]
        yield
        x_hi = x1.astype(BF16)
        x_lo = (x1 - x_hi.astype(F32)).astype(BF16)
        lg = (jnp.dot(jnp.concatenate([x_hi, x_lo, x_hi], -1), rw_ref[...], preferred_element_type=F32)
              + rb_ref[...])
        lane = lax.broadcasted_iota(jnp.int32, (sub, LANES), 1).astype(F32)
        sels, vals, idxs = [], [], []
        for _ in range(TOP_K):
            m = jnp.max(lg, -1, keepdims=True)
            idx = jnp.min(jnp.where(lg == m, lane, float(LANES)), -1, keepdims=True)
            sel = lane == idx
            lg = jnp.where(sel, NEG_BIG * 2, lg)
            sels.append(sel)
            vals.append(m)
            idxs.append(idx)
        yield
        es = [jnp.exp(v - vals[0]) for v in vals]
        den = es[0] + es[1] + es[2] + es[3]
        hot = jnp.zeros((sub, LANES), F32)
        for sel in sels:
            hot = hot + sel.astype(F32)
        r = lax.broadcasted_iota(jnp.int32, (sub, sub), 0)
        c = lax.broadcasted_iota(jnp.int32, (sub, sub), 1)
        excl = jnp.dot((c < r).astype(BF16), hot.astype(BF16), preferred_element_type=F32) + carry_ref[...]
        carry_ref[...] = carry_ref[...] + jnp.sum(hot, 0, keepdims=True)
        route = jnp.zeros((sub, LANES), F32)
        for kk in range(TOP_K):
            rank = jnp.sum(jnp.where(sels[kk], excl, 0.0), -1, keepdims=True)
            route = jnp.where(lane == float(kk), idxs[kk], route)
            route = jnp.where(lane == float(TOP_K + kk), rank, route)
            route = jnp.where(lane == float(2 * TOP_K + kk), es[kk] / den, route)
        route_ref[rows, :] = route
        yield

    for _ in zip(*[rows_of(first) for first in range(0, tm, sub)]):
        pass
    cnt_ref[...] = carry_ref[...]


def _merge(x, o, yf, yb, xbc, z, u, gl, wa, ws, wc, wo, dsk, ng, cg, cb, bc, l1g, l1b, rw, rb, tm=512, sub=256):
    t = x.shape[0]
    tm = min(tm, t)

    def rowb(width, col=0):
        return pl.BlockSpec((tm, width), lambda i: (i, col))

    def const(shape):
        return pl.BlockSpec(shape, lambda i: (0, 0))

    return pl.pallas_call(
        functools.partial(_merge_kernel, sub=min(sub, tm)),
        grid=(t // tm,),
        in_specs=[rowb(D_MODEL), rowb(MLA_HEADS * MLA_DV), rowb(SSM_D_INNER), rowb(SSM_D_INNER), rowb(SSM_D_INNER),
                  rowb(SSM_D_INNER), rowb(CNV_CH), rowb(3 * D_MODEL),
                  const((MLA_HEADS * MLA_DV, D_MODEL)), const((SSM_D_INNER, D_MODEL)), const((CNV_CH, D_MODEL)),
                  const((D_MODEL, D_MODEL)), const((1, SSM_D_INNER)), const((1, SSM_D_INNER)),
                  const((1, CNV_CH)), const((1, CNV_CH)), const((1, D_MODEL)),
                  const((1, D_MODEL)), const((1, D_MODEL)), const((3 * D_MODEL, LANES)), const((1, LANES))],
        out_specs=[rowb(D_MODEL),
                   pl.BlockSpec((tm * ROW_TILES, LANES), lambda i: (i, 0)),
                   rowb(LANES), const((1, LANES))],
        out_shape=[jax.ShapeDtypeStruct((t, D_MODEL), F32),
                   jax.ShapeDtypeStruct((t * ROW_TILES, LANES), F32),
                   jax.ShapeDtypeStruct((t, LANES), F32), jax.ShapeDtypeStruct((1, LANES), F32)],
        scratch_shapes=[pltpu.VMEM((1, LANES), F32)],
        compiler_params=_params("arbitrary"),
        name="merge",
    )(x, o, yf, yb, xbc, z, u, gl, wa, ws, wc, wo, dsk, ng, cg, cb, bc, l1g, l1b, rw, rb)


def _row_slice(ref, row):
    return ref.at[pl.ds(pl.multiple_of(row * ROW_TILES, ROW_TILES), ROW_TILES)]


def _from_row_tiles(ref, first, rows, pitch):
    return jnp.concatenate([ref[pl.ds(first + t, rows, stride=pitch), :] for t in range(ROW_TILES)], -1)


def _fill_padding(fill_start_ref, fill_n_ref, zero_ref, xs_hbm, sem):
    zero_ref[...] = jnp.zeros_like(zero_ref)
    bits = [1 << b for b in reversed(range((MOE_ROWS - 1).bit_length()))]

    def copies(e):
        n_e = fill_n_ref[e]
        for bit in bits:
            done = n_e & ~(2 * bit - 1)
            yield (n_e & bit) != 0, pltpu.make_async_copy(
                zero_ref.at[pl.ds(0, bit * ROW_TILES)],
                xs_hbm.at[pl.ds(pl.multiple_of((fill_start_ref[e] + done) * ROW_TILES, ROW_TILES),
                                bit * ROW_TILES)], sem)

    def start(e, carry):
        for cond, cp in copies(e):
            pl.when(cond)(cp.start)
        return carry

    def wait(e, carry):
        for cond, cp in copies(e):
            pl.when(cond)(cp.wait)
        return carry

    def tail(j):
        first = (fill_start_ref[N_EXPERTS] * ROW_TILES + j * zero_ref.shape[0])
        return pltpu.make_async_copy(zero_ref, xs_hbm.at[pl.ds(pl.multiple_of(first, ROW_TILES),
                                                               zero_ref.shape[0])], sem)

    def tail_start(j, carry):
        tail(j).start()
        return carry

    def tail_wait(j, carry):
        tail(j).wait()
        return carry

    lax.fori_loop(0, N_EXPERTS, start, 0)
    lax.fori_loop(0, fill_n_ref[N_EXPERTS], tail_start, 0)
    lax.fori_loop(0, N_EXPERTS, wait, 0)
    lax.fori_loop(0, fill_n_ref[N_EXPERTS], tail_wait, 0)


def _dispatch_kernel(fill_start_ref, fill_n_ref, dest_ref, x_ref, xs_hbm, sem, fill_sem, zero_ref, *, tm):
    @pl.when(pl.program_id(0) == pl.num_programs(0) - 1)
    def _():
        _fill_padding(fill_start_ref, fill_n_ref, zero_ref, xs_hbm, fill_sem)

    def body(c, carry):
        for u in range(ISSUE_UNROLL):
            t = c * ISSUE_UNROLL + u
            src = _row_slice(x_ref, t)
            for kk in range(TOP_K):
                pltpu.make_async_copy(src, _row_slice(xs_hbm, dest_ref[0, 0, t * TOP_K + kk]),
                                      sem).start(priority=kk % 2)
        return carry

    lax.fori_loop(0, tm // ISSUE_UNROLL, body, 0)
    for _ in range(TOP_K):
        pltpu.make_async_copy(x_ref, xs_hbm.at[pl.ds(0, tm * ROW_TILES)], sem).wait()


def _dispatch(dest, fill_start, fill_n, x1g, n_rows, tm=1024):
    t = x1g.shape[0] // ROW_TILES
    tm = min(tm, t)
    nt = t // tm
    grid_spec = pltpu.PrefetchScalarGridSpec(
        num_scalar_prefetch=2,
        grid=(nt,),
        in_specs=[pl.BlockSpec((1, 1, tm * TOP_K), lambda i, fs, fn: (i, 0, 0), memory_space=pltpu.SMEM),
                  pl.BlockSpec((tm * ROW_TILES, LANES), lambda i, fs, fn: (i, 0))],
        out_specs=pl.BlockSpec(memory_space=pl.ANY),
        scratch_shapes=[pltpu.SemaphoreType.DMA(()), pltpu.SemaphoreType.DMA(()),
                        pltpu.VMEM((MOE_ROWS // 2 * ROW_TILES, LANES), F32)],
    )
    return pl.pallas_call(
        functools.partial(_dispatch_kernel, tm=tm),
        grid_spec=grid_spec,
        out_shape=jax.ShapeDtypeStruct((n_rows * ROW_TILES, LANES), F32),
        compiler_params=_params("arbitrary"),
        name="dispatch",
    )(fill_start, fill_n, dest.reshape(nt, 1, tm * TOP_K), x1g)


def _moe_kernel(blk_e_ref, nused_ref, x_ref, wgu_ref, bgu_ref, wdn_ref, bdn_ref, o_ref, wgu_b, wdn_b):
    i = pl.program_id(0)

    @pl.when((i == 0) | (blk_e_ref[i] != blk_e_ref[jnp.maximum(i - 1, 0)]))
    def _():
        wgu_b[...] = wgu_ref[0, 0].astype(BF16)
        wdn_b[...] = wdn_ref[0, 0].astype(BF16)

    @pl.when(i < nused_ref[0])
    def _():
        xb = _from_row_tiles(x_ref, 0, MOE_ROWS, ROW_TILES).astype(BF16)
        h = jnp.dot(xb, wgu_b[...], preferred_element_type=F32) + bgu_ref[0]
        gate = jnp.minimum(h[:, :D_FF], SWIGLU_LIMIT)
        up = jnp.clip(h[:, D_FF:], -SWIGLU_LIMIT, SWIGLU_LIMIT)
        act = (up + 1.0) * (gate * _sigmoid(SWIGLU_ALPHA * gate))
        y = jnp.dot(act.astype(BF16), wdn_b[...], preferred_element_type=F32) + bdn_ref[0]
        for t in range(ROW_TILES):
            o_ref[pl.ds(t, MOE_ROWS, stride=ROW_TILES), :] = y[:, t * LANES:(t + 1) * LANES]

    @pl.when(i >= nused_ref[0])
    def _():
        o_ref[...] = jnp.zeros_like(o_ref)


def _moe_experts(blk_e, nused, xs, wgu, bgu, wdn, bdn, layer):
    n_blocks = blk_e.shape[0]
    grid_spec = pltpu.PrefetchScalarGridSpec(
        num_scalar_prefetch=2,
        grid=(n_blocks,),
        in_specs=[pl.BlockSpec((MOE_ROWS * ROW_TILES, LANES), lambda i, be, nu: (jnp.minimum(i, nu[0] - 1), 0)),
                  pl.BlockSpec((1, 1, D_MODEL, 2 * D_FF), lambda i, be, nu: (layer, be[i], 0, 0)),
                  pl.BlockSpec((1, 1, 2 * D_FF), lambda i, be, nu: (be[i], 0, 0)),
                  pl.BlockSpec((1, 1, D_FF, D_MODEL), lambda i, be, nu: (layer, be[i], 0, 0)),
                  pl.BlockSpec((1, 1, D_MODEL), lambda i, be, nu: (be[i], 0, 0))],
        out_specs=pl.BlockSpec((MOE_ROWS * ROW_TILES, LANES), lambda i, be, nu: (i, 0)),
        scratch_shapes=[pltpu.VMEM((D_MODEL, 2 * D_FF), BF16), pltpu.VMEM((D_FF, D_MODEL), BF16)],
    )
    return pl.pallas_call(
        _moe_kernel,
        grid_spec=grid_spec,
        out_shape=jax.ShapeDtypeStruct(xs.shape, F32),
        compiler_params=_params("arbitrary"),
        name="moe_experts",
    )(blk_e, nused, xs, wgu, bgu, wdn, bdn)


def _gather_topk_rows(dest_ref, yb_hbm, dst, sem, tm):
    def body(c, carry):
        for u in range(ISSUE_UNROLL):
            t = c * ISSUE_UNROLL + u
            for kk in range(TOP_K):
                pltpu.make_async_copy(_row_slice(yb_hbm, dest_ref[0, 0, t * TOP_K + kk]),
                                      _row_slice(dst, kk * tm + t), sem).start(priority=kk % 2)
        return carry
    lax.fori_loop(0, tm // ISSUE_UNROLL, body, 0)


def _combine_kernel(dest_first_ref, dest_next_ref, yb_hbm, x1_ref, route_ref, g_ref, b_ref, x2_ref, x2b_ref,
                    buf, sem, *, tm):
    i = pl.program_id(0)
    n = pl.num_programs(0)
    slot = i % 2

    @pl.when(i == 0)
    def _():
        _gather_topk_rows(dest_first_ref, yb_hbm, buf.at[0], sem.at[0], tm)

    @pl.when(i + 1 < n)
    def _():
        _gather_topk_rows(dest_next_ref, yb_hbm, buf.at[1 - slot], sem.at[1 - slot], tm)

    pltpu.make_async_copy(yb_hbm.at[pl.ds(0, tm * TOP_K * ROW_TILES)], buf.at[slot], sem.at[slot]).wait()
    route = route_ref[...]
    ffn = jnp.zeros((tm, D_MODEL), F32)
    for kk in range(TOP_K):
        rows = _from_row_tiles(buf.at[slot], kk * tm * ROW_TILES, tm, ROW_TILES)
        ffn = ffn + rows * route[:, 2 * TOP_K + kk:2 * TOP_K + kk + 1]
    x2 = _layernorm(DN_ALPHA * x1_ref[...] + ffn, g_ref[...], b_ref[...])
    x2_ref[...] = x2
    x2b_ref[...] = x2.astype(BF16)


def _combine(dest, yb, x1, route, g, b, tm=512):
    t = x1.shape[0]
    tm = min(tm, t)
    nt = t // tm
    dest3 = dest.reshape(nt, 1, tm * TOP_K)
    return pl.pallas_call(
        functools.partial(_combine_kernel, tm=tm),
        grid=(nt,),
        in_specs=[pl.BlockSpec((1, 1, tm * TOP_K), lambda i: (0, 0, 0), memory_space=pltpu.SMEM),
                  pl.BlockSpec((1, 1, tm * TOP_K), lambda i: (jnp.minimum(i + 1, nt - 1), 0, 0),
                               memory_space=pltpu.SMEM),
                  pl.BlockSpec(memory_space=pl.ANY),
                  pl.BlockSpec((tm, D_MODEL), lambda i: (i, 0)),
                  pl.BlockSpec((tm, LANES), lambda i: (i, 0)),
                  pl.BlockSpec((1, D_MODEL), lambda i: (0, 0)),
                  pl.BlockSpec((1, D_MODEL), lambda i: (0, 0))],
        out_specs=[pl.BlockSpec((tm, D_MODEL), lambda i: (i, 0)),
                   pl.BlockSpec((tm, D_MODEL), lambda i: (i, 0))],
        out_shape=[jax.ShapeDtypeStruct((t, D_MODEL), F32), jax.ShapeDtypeStruct((t, D_MODEL), BF16)],
        scratch_shapes=[pltpu.VMEM((2, tm * TOP_K * ROW_TILES, LANES), F32),
                        pltpu.SemaphoreType.DMA((2,))],
        compiler_params=_params("arbitrary"),
        name="combine",
    )(dest3, dest3, yb, x1, route, g, b)


def _pad_cols(w, width):
    return jnp.pad(w, ((0, 0), (0, width - w.shape[1])))


def _pack_in_proj(w_in, b_in):
    offs = np.concatenate([[0], np.cumsum(IN_SIZES)])
    wb = jnp.concatenate([w_in, b_in[None, :]], 0)

    def piece(i):
        return wb[:, offs[i]:offs[i + 1]]

    kr = piece(2)
    kr_swapped = jnp.concatenate([kr[:, MLA_DR // 2:], kr[:, :MLA_DR // 2]], 1)
    dt = piece(5)
    grp_a = jnp.concatenate([piece(0), piece(1), _pad_cols(jnp.concatenate([kr, kr_swapped], 1), LANES),
                             _pad_cols(dt[:, :SSM_HEADS], LANES), _pad_cols(dt[:, SSM_HEADS:], LANES)], 1)
    groups = {"a": (grp_a, F32), "z": (piece(3), BF16), "xbc": (piece(4), F32),
              "cnv": (jnp.concatenate([piece(6), piece(7)], 1), F32), "gate": (piece(8), BF16)}
    return {k: (v[:-1].astype(BF16), v[-1:], dt_) for k, (v, dt_) in groups.items()}


def _pack_mla(w_uq, w_ukv, w_br_attn, seq):
    hq = MLA_DN + MLA_DR
    wq = w_uq.reshape(MLA_Q_LORA, MLA_HEADS, hq)
    zq = jnp.zeros((MLA_Q_LORA, MLA_HEADS, HEAD_PAD - hq), F32)
    wq_main = jnp.concatenate([wq, zq], -1).reshape(MLA_Q_LORA, MLA_W)
    wkv = w_ukv.reshape(MLA_KV_LORA, MLA_HEADS, MLA_DN + MLA_DV)
    zk = jnp.zeros((MLA_KV_LORA, MLA_HEADS, HEAD_PAD - MLA_DN), F32)
    wk = jnp.concatenate([wkv[..., :MLA_DN], zk], -1).reshape(MLA_KV_LORA, MLA_W)
    wv = jnp.concatenate([wkv[..., MLA_DN:], zk], -1).reshape(MLA_KV_LORA, MLA_W)
    e2 = np.zeros((LANES, MLA_HEADS, HEAD_PAD), np.float32)
    for j in range(MLA_DR):
        e2[j, :, MLA_DN + j] = 1.0
        e2[MLA_DR + j, :, MLA_DN + j] = 1.0
    vone = np.zeros((MLA_HEADS, HEAD_PAD), np.float32)
    vone[:, MLA_DV] = 1.0
    pos = jnp.arange(seq, dtype=F32)
    inv = ROPE_THETA ** (-jnp.arange(0, MLA_DR, 2, dtype=F32) / MLA_DR)
    ang = pos[:, None] * inv[None, :]
    cos, sin = jnp.cos(ang), jnp.sin(ang)
    scale = (MLA_DN + MLA_DR) ** -0.5 * math.log2(math.e)
    ones = jnp.ones((seq, MLA_DN), F32)
    zpad = jnp.zeros((seq, HEAD_PAD - hq), F32)
    cq = jnp.tile(jnp.concatenate([ones, cos, cos, zpad], 1) * scale, (1, MLA_HEADS))
    sq = jnp.tile(jnp.concatenate([0 * ones, -sin, sin, zpad], 1) * scale, (1, MLA_HEADS))
    tk = jnp.concatenate([cos, cos, -sin, sin, jnp.zeros((seq, LANES - 2 * MLA_DR), F32)], 1)
    return dict(wq=wq_main.astype(BF16), wk=wk.astype(BF16), wv=wv.astype(BF16),
                e2=jnp.asarray(e2.reshape(LANES, MLA_W), BF16), vone=jnp.asarray(vone.reshape(1, MLA_W)),
                wbr=w_br_attn.astype(BF16), cq=cq, sq=sq, tk=tk)


def _route_tables(route, cnt, n_tok):
    idx = route[:, :TOP_K].astype(jnp.int32)
    rank = route[:, TOP_K:2 * TOP_K].astype(jnp.int32)
    counts = cnt[0, :N_EXPERTS].astype(jnp.int32)
    n_blocks = -(-(n_tok * TOP_K + N_EXPERTS * (MOE_ROWS - 1)) // MOE_ROWS)
    padded = (counts + MOE_ROWS - 1) // MOE_ROWS * MOE_ROWS
    pad_end = jnp.cumsum(padded)
    pad_start = pad_end - padded
    onehot = idx[..., None] == jnp.arange(N_EXPERTS, dtype=jnp.int32)
    dest = (jnp.sum(jnp.where(onehot, pad_start, 0), -1) + rank).reshape(-1)
    blk_first = jnp.arange(n_blocks, dtype=jnp.int32) * MOE_ROWS
    blk_e = jnp.minimum(jnp.sum((pad_end[None, :] <= blk_first[:, None]).astype(jnp.int32), -1), N_EXPERTS - 1)
    nused = pad_end[-1:] // MOE_ROWS
    n_rows = n_blocks * MOE_ROWS
    fill_start = jnp.concatenate([pad_start + counts, pad_end[-1:]])
    fill_n = jnp.concatenate([padded - counts, (n_rows - pad_end[-1:]) // (MOE_ROWS // 2)])
    return dest, blk_e, nused, fill_start, fill_n, n_rows


def kernel(x, w_in, b_in, mla_q_norm, mla_kv_norm, mla_w_uq, mla_w_ukv, w_br_attn, ssm_conv_w, ssm_conv_b, ssm_dt_bias, ssm_a_log, ssm_d, ssm_norm, w_br_ssm, cnv_dw_w, cnv_dw_b, cnv_ln_g, cnv_ln_b, w_br_conv, b_br_conv, w_out, ln1_g, ln1_b, router_w, router_b, moe_w_gate_up, moe_b_gate_up, moe_w_down, moe_b_down, ln2_g, ln2_b):
    bsz, seq, d = x.shape
    n_tok = bsz * seq
    xf = x.reshape(n_tok, d)
    xb = xf.astype(BF16)
    for l in range(DEPTH):
        proj = _pack_in_proj(w_in[l], b_in[l])
        mla = _pack_mla(mla_w_uq[l], mla_w_ukv[l], w_br_attn[l], seq)
        a_grp = _linear(xb, *proj["a"])
        z = _linear(xb, *proj["z"])
        xbc_raw = _linear(xb, *proj["xbc"], tn=768)
        cnv_raw = _linear(xb, *proj["cnv"])
        gate_logits = _linear(xb, *proj["gate"], tn=1024)

        q, k, v = _mla_prep(a_grp, seq, mla_q_norm[l][None], mla_kv_norm[l][None], mla["wq"],
                            mla["wk"], mla["wv"], mla["e2"], mla["cq"], mla["sq"], mla["tk"], mla["vone"])
        attn = _attention(q.reshape(bsz, seq, MLA_W), k.reshape(bsz, seq, MLA_W), v.reshape(bsz, seq, MLA_W))

        xbc = _dwconv(xbc_raw.reshape(bsz, seq, SSM_CONV_DIM), ssm_conv_w[l], ssm_conv_b[l],
                      glu=False, silu_out=True, rows=256)
        a3 = a_grp.reshape(bsz, seq, GROUP_A)
        dtb = jnp.pad(ssm_dt_bias[l], ((0, 0), (0, LANES - SSM_HEADS)))
        nega = jnp.pad(-jnp.exp(ssm_a_log[l]), ((0, 0), (0, LANES - SSM_HEADS)))
        y_fwd, y_bwd = _ssd(xbc, a3, dtb, nega)

        u = _dwconv(cnv_raw.reshape(bsz, seq, 2 * CNV_CH), cnv_dw_w[l], cnv_dw_b[l], glu=True, silu_out=False,
                    rows=128)

        rw = jnp.pad(router_w[l], ((0, 0), (0, LANES - N_EXPERTS)))
        rw_hi = rw.astype(BF16)
        rw_lo = (rw - rw_hi.astype(F32)).astype(BF16)
        rw = jnp.concatenate([rw_hi, rw_hi, rw_lo], 0)
        rb = jnp.pad(router_b[l], (0, LANES - N_EXPERTS), constant_values=NEG_BIG)[None]
        x1, x1g, route, cnt = _merge(
            xf, attn.reshape(n_tok, MLA_HEADS * MLA_DV), y_fwd.reshape(n_tok, SSM_D_INNER), y_bwd.reshape(n_tok, SSM_D_INNER),
            xbc.reshape(n_tok, SSM_CONV_DIM), z, u.reshape(n_tok, CNV_CH), gate_logits,
            mla["wbr"], w_br_ssm[l].astype(BF16), w_br_conv[l].astype(BF16), w_out[l].astype(BF16),
            jnp.repeat(ssm_d[l], SSM_HEAD_DIM)[None], ssm_norm[l][None], cnv_ln_g[l][None], cnv_ln_b[l][None],
            b_br_conv[l][None], ln1_g[l][None], ln1_b[l][None], rw, rb)

        dest, blk_e, nused, fill_start, fill_n, n_rows = _route_tables(route, cnt, n_tok)
        xs = _dispatch(dest, fill_start, fill_n, x1g, n_rows)
        yb = _moe_experts(blk_e, nused, xs, moe_w_gate_up, moe_b_gate_up[l][:, None, :], moe_w_down,
                          moe_b_down[l][:, None, :], l)
        xf, xb = _combine(dest, yb, x1, route, ln2_g[l][None], ln2_b[l][None])
    return xf.reshape(bsz, seq, d)
```

```python
import functools
import math

import numpy as np
import jax
import jax.numpy as jnp
from jax import lax
from jax.experimental import pallas as pl
from jax.experimental.pallas import tpu as pltpu

F32 = jnp.float32
BF16 = jnp.bfloat16
HIGHEST = lax.Precision.HIGHEST

LANES = 128
SUBLANES = 8
VMEM_LIMIT_BYTES = 56 * 1024 * 1024

D_MODEL = 1024
DEPTH = 2
MLA_HEADS = 8
MLA_Q_LORA = 384
MLA_KV_LORA = 256
MLA_DN = 64
MLA_DR = 32
MLA_DV = 64
ROPE_THETA = 10000.0
SSM_HEADS = 16
SSM_HEAD_DIM = 64
SSM_D_INNER = SSM_HEADS * SSM_HEAD_DIM
SSM_GROUPS = 4
SSM_STATE = 64
SSM_CONV = 5
SSM_CHUNK = 128
SSM_BC = SSM_GROUPS * SSM_STATE
SSM_CONV_DIM = SSM_D_INNER + 2 * SSM_BC
CNV_CH = 512
CNV_WIDTH = 31
N_EXPERTS = 32
TOP_K = 4
D_FF = 1024
SWIGLU_LIMIT = 7.0
SWIGLU_ALPHA = 1.702
DN_ALPHA = (2 * DEPTH) ** 0.25
IN_SIZES = (MLA_Q_LORA, MLA_KV_LORA, MLA_DR, SSM_D_INNER, SSM_CONV_DIM, 2 * SSM_HEADS,
            CNV_CH, CNV_CH, 3 * D_MODEL)
HEAD_PAD = LANES
MLA_W = MLA_HEADS * HEAD_PAD
GROUP_A = 1024
CONV_HALO = 16
MOE_ROWS = 512
ATTN_AHEAD = 2
ISSUE_UNROLL = 8
ROW_TILES = D_MODEL // LANES
NEG_BIG = -1e30


def _params(*sem):
    return pltpu.CompilerParams(dimension_semantics=sem, vmem_limit_bytes=VMEM_LIMIT_BYTES)


def _sigmoid(x):
    return 0.5 * jnp.tanh(0.5 * x) + 0.5


def _linear_kernel(x_ref, w_ref, b_ref, o_ref):
    acc = jnp.dot(x_ref[...], w_ref[...], preferred_element_type=F32)
    o_ref[...] = (acc + b_ref[...]).astype(o_ref.dtype)


def _linear(x, w, b, out_dtype, tm=1024, tn=None):
    m, k = x.shape
    n = w.shape[1]
    tm = min(tm, m)
    tn = n if tn is None else tn
    return pl.pallas_call(
        _linear_kernel,
        grid=(n // tn, m // tm),
        in_specs=[pl.BlockSpec((tm, k), lambda j, i: (i, 0)),
                  pl.BlockSpec((k, tn), lambda j, i: (0, j)),
                  pl.BlockSpec((1, tn), lambda j, i: (0, j))],
        out_specs=pl.BlockSpec((tm, tn), lambda j, i: (i, j)),
        out_shape=jax.ShapeDtypeStruct((m, n), out_dtype),
        compiler_params=_params("parallel", "parallel"),
        name="linear",
    )(x, w, b)


def _rms(x, g, eps=1e-6):
    return x * lax.rsqrt(jnp.mean(x * x, -1, keepdims=True) + eps) * g


def _mla_prep_kernel(a_ref, gq_ref, gkv_ref, wq_ref, wk_ref, wv_ref, e2_ref,
                     cq_ref, sq_ref, tk_ref, vone_ref, q_ref, k_ref, v_ref, *, sub):
    def rows_of(first):
        rows = pl.ds(first, sub)
        a = a_ref[rows, :]
        c_q = a[:, :MLA_Q_LORA]
        c_kv = a[:, MLA_Q_LORA:MLA_Q_LORA + MLA_KV_LORA]
        kr = a[:, MLA_Q_LORA + MLA_KV_LORA:]
        qn = _rms(c_q, gq_ref[...]).astype(BF16)
        kvn = _rms(c_kv, gkv_ref[...]).astype(BF16)
        yield
        q = jnp.dot(qn, wq_ref[...], preferred_element_type=F32)
        half = MLA_DR // 2
        first_half = lax.broadcasted_iota(jnp.int32, q.shape, 1) % HEAD_PAD < MLA_DN + half
        swapped = jnp.where(first_half, pltpu.roll(q, MLA_W - half, 1), pltpu.roll(q, half, 1))
        q_ref[rows, :] = (q * cq_ref[rows, :] + swapped * sq_ref[rows, :]).astype(BF16)
        yield
        krp = (kr * tk_ref[rows, :]).astype(BF16)
        k = (jnp.dot(kvn, wk_ref[...], preferred_element_type=F32)
             + jnp.dot(krp, e2_ref[...], preferred_element_type=F32))
        k_ref[rows, :] = k.astype(BF16)
        yield
        v = jnp.dot(kvn, wv_ref[...], preferred_element_type=F32) + vone_ref[...]
        v_ref[rows, :] = v.astype(BF16)
        yield

    for _ in zip(*[rows_of(first) for first in range(0, a_ref.shape[0], sub)]):
        pass


def _mla_prep(a, seq, gq, gkv, wq, wk, wv, e2, cq, sq, tk, vone, tm=1024, sub=512):
    t = a.shape[0]
    tm = min(tm, seq)
    nper = seq // tm
    wa = MLA_Q_LORA + MLA_KV_LORA + LANES

    def const(shape):
        return pl.BlockSpec(shape, lambda i: (0, 0))

    def tab(width):
        return pl.BlockSpec((tm, width), lambda i: (i % nper, 0))

    out = jax.ShapeDtypeStruct((t, MLA_W), BF16)
    return pl.pallas_call(
        functools.partial(_mla_prep_kernel, sub=min(sub, tm)),
        grid=(t // tm,),
        in_specs=[pl.BlockSpec((tm, wa), lambda i: (i, 0)),
                  const((1, MLA_Q_LORA)), const((1, MLA_KV_LORA)),
                  const((MLA_Q_LORA, MLA_W)),
                  const((MLA_KV_LORA, MLA_W)), const((MLA_KV_LORA, MLA_W)),
                  const((LANES, MLA_W)),
                  tab(MLA_W), tab(MLA_W), tab(LANES), const((1, MLA_W))],
        out_specs=[pl.BlockSpec((tm, MLA_W), lambda i: (i, 0))] * 3,
        out_shape=[out, out, out],
        compiler_params=_params("parallel"),
        name="mla_prep",
    )(a, gq, gkv, wq, wk, wv, e2, cq, sq, tk, vone)


def _attn_kernel(q_ref, k_ref, v_ref, o_ref, *, sub):
    low = lax.broadcasted_iota(jnp.int32, (sub, HEAD_PAD), 1) < MLA_DV
    units = [(r, hh) for r in range(q_ref.shape[1] // sub) for hh in range(2)]

    def scores(unit):
        r, hh = unit
        lanes = slice(hh * HEAD_PAD, (hh + 1) * HEAD_PAD)
        return lax.dot_general(q_ref[0, r * sub:(r + 1) * sub, lanes], k_ref[0, :, lanes],
                               (((1,), (1,)), ((), ())), preferred_element_type=F32)

    ahead = [scores(u) for u in units[:ATTN_AHEAD]]
    outs = []
    for i, (r, hh) in enumerate(units):
        s = ahead.pop(0)
        if i + ATTN_AHEAD < len(units):
            ahead.append(scores(units[i + ATTN_AHEAD]))
        lanes = slice(hh * HEAD_PAD, (hh + 1) * HEAD_PAD)
        m = jnp.max(s, -1, keepdims=True)
        p = jnp.exp2(s - m).astype(BF16)
        o = jnp.dot(p, v_ref[0, :, lanes], preferred_element_type=F32)
        outs.append(o / o[:, MLA_DV:MLA_DV + 1])
        if hh == 1:
            o_ref[0, r * sub:(r + 1) * sub] = jnp.where(
                low, outs[0], pltpu.roll(outs[1], MLA_DV, 1)).astype(o_ref.dtype)
            outs = []


def _attention(q, k, v, tq=2048, sub=256):
    b, s, _ = q.shape
    tq = min(tq, s)
    pair = 2 * HEAD_PAD
    return pl.pallas_call(
        functools.partial(_attn_kernel, sub=min(sub, tq)),
        grid=(b, MLA_HEADS // 2, s // tq),
        in_specs=[pl.BlockSpec((1, tq, pair), lambda bi, h, i: (bi, i, h)),
                  pl.BlockSpec((1, s, pair), lambda bi, h, i: (bi, 0, h)),
                  pl.BlockSpec((1, s, pair), lambda bi, h, i: (bi, 0, h))],
        out_specs=pl.BlockSpec((1, tq, HEAD_PAD), lambda bi, h, i: (bi, i, h)),
        out_shape=jax.ShapeDtypeStruct((b, s, MLA_HEADS * MLA_DV), BF16),
        compiler_params=_params("parallel", "parallel", "parallel"),
        name="attention",
    )(q, k, v)


def _dwconv_kernel(*refs, width, glu, silu_out, seq, rows):
    if glu:
        a_ref, g_ref, w_ref, b_ref, o_ref, pad_ref = refs
        pre = a_ref[0] * _sigmoid(g_ref[0])
    else:
        x_ref, w_ref, b_ref, o_ref, pad_ref = refs
        pre = x_ref[0]
    ch = o_ref.shape[-1]
    halo = jnp.zeros((CONV_HALO, ch), F32)
    pad_ref[0:CONV_HALO, :] = halo
    pad_ref[CONV_HALO + seq:2 * CONV_HALO + seq, :] = halo
    pad_ref[CONV_HALO:CONV_HALO + seq, :] = pre
    half = (width - 1) // 2
    win_rows = rows + 2 * CONV_HALO

    def body(c, carry):
        base = pl.multiple_of(c * rows, rows)
        acc = jnp.zeros((rows, ch), F32) + b_ref[...]
        for t in range(width):
            acc = acc + pad_ref[pl.ds(base + (CONV_HALO - half + t), rows), :] * w_ref[t:t + 1, :]
        if silu_out:
            acc = acc * _sigmoid(acc)
        o_ref[0, pl.ds(base, rows), :] = acc
        return carry

    lax.fori_loop(0, seq // rows, body, 0)


def _dwconv(x, w, b, *, glu, silu_out, rows=64):
    bsz, seq, cin = x.shape
    width, ch = w.shape
    nct = ch // LANES
    kern = functools.partial(_dwconv_kernel, width=width, glu=glu, silu_out=silu_out, seq=seq,
                             rows=min(rows, seq))
    xspec = pl.BlockSpec((1, seq, LANES), lambda bi, j: (bi, 0, j))
    in_specs = [xspec]
    args = [x]
    if glu:
        in_specs.append(pl.BlockSpec((1, seq, LANES), lambda bi, j: (bi, 0, j + nct)))
        args.append(x)
    in_specs += [pl.BlockSpec((width, LANES), lambda bi, j: (0, j)),
                 pl.BlockSpec((1, LANES), lambda bi, j: (0, j))]
    return pl.pallas_call(
        kern,
        grid=(bsz, nct),
        in_specs=in_specs,
        out_specs=pl.BlockSpec((1, seq, LANES), lambda bi, j: (bi, 0, j)),
        out_shape=jax.ShapeDtypeStruct((bsz, seq, ch), F32),
        scratch_shapes=[pltpu.VMEM((seq + 2 * CONV_HALO, LANES), F32)],
        compiler_params=_params("parallel", "parallel"),
        name="dwconv",
    )(*args, w, b.reshape(1, ch))


def _softplus(x):
    return jnp.maximum(x, 0.0) + jnp.log1p(jnp.exp(-jnp.abs(x)))


def _ssd_kernel(*refs, nb):
    n_in = 8
    ins = (refs[0:n_in], refs[n_in:2 * n_in])
    ys = refs[2 * n_in:2 * n_in + 2]
    scratch = refs[2 * n_in + 2:]
    n_s = len(scratch) // (2 * nb)
    chains = [(d, bi, scratch[(d * nb + bi) * n_s:(d * nb + bi + 1) * n_s]) for bi in range(nb) for d in range(2)]

    @pl.when(pl.program_id(1) == 0)
    def _():
        for _, _, sc in chains:
            sc[0][...] = jnp.zeros_like(sc[0])

    etots = [_ssd_prep(bi, *ins[d], *sc, reverse=bool(d)) for d, bi, sc in chains]
    for g in range(SSM_GROUPS):
        gens = [_ssd_group(g, etot, bi, *ins[d], ys[d], *sc) for etot, (d, bi, sc) in zip(etots, chains)]
        for _ in zip(*gens):
            pass


def _ssd_prep(bi, xs_ref, bm_ref, cm_ref, dt_ref, dtb_ref, nega_ref, cum_ref, mask_ref,
              h_ref, acs_s, acst_s, dtt_s, wstt_s, bmt_s, bmtb_s, *, reverse):
    ln = SSM_CHUNK
    dt = _softplus(dt_ref[bi] + dtb_ref[...])
    a = dt * nega_ref[...]
    a_hi = a.astype(BF16)
    rem = a - a_hi.astype(F32)
    a_mid = rem.astype(BF16)
    a_lo = (rem - a_mid.astype(F32)).astype(BF16)
    acs = jnp.dot(cum_ref[...], jnp.concatenate([a_hi, a_mid, a_lo], 0), preferred_element_type=F32)
    tot = acs[0:1] if reverse else acs[ln - 1:ln]
    etot = jnp.exp(tot)
    acs_s[...] = acs
    acst_s[...] = acs.T
    dtt_s[...] = dt.T
    wstt_s[...] = (dt * jnp.exp(tot - acs)).T
    bm_t = bm_ref[bi].T
    bmt_s[...] = bm_t
    bmtb_s[...] = bm_t.astype(BF16)
    return etot


def _ssd_group(g, etot, bi, xs_ref, bm_ref, cm_ref, dt_ref, dtb_ref, nega_ref, cum_ref, mask_ref, y_ref,
               h_ref, acs_s, acst_s, dtt_s, wstt_s, bmt_s, bmtb_s):
    ln = SSM_CHUNK
    glane = lax.broadcasted_iota(jnp.int32, (ln, SSM_BC), 1) // SSM_STATE
    lo = lax.broadcasted_iota(jnp.int32, (ln, LANES), 1) < SSM_HEAD_DIM
    lo_1 = lax.broadcasted_iota(jnp.int32, (1, LANES), 1) < SSM_HEAD_DIM
    heads_per_group = SSM_HEADS // SSM_GROUPS
    group_w = heads_per_group * SSM_HEAD_DIM
    cm_b = cm_ref[bi].astype(BF16)
    cb = jnp.dot(jnp.where(glane == g, cm_b, jnp.zeros_like(cm_b)), bmtb_s[...],
                 preferred_element_type=F32)
    glanes = slice(g * group_w, (g + 1) * group_w)
    y_off = jnp.dot(cm_b, h_ref[:, glanes].astype(BF16), preferred_element_type=F32)
    rows = slice(g * SSM_STATE, (g + 1) * SSM_STATE)
    for j in range(heads_per_group // 2):
        pair = g * (heads_per_group // 2) + j
        lanes = slice(pair * LANES, (pair + 1) * LANES)
        x = xs_ref[bi, :, lanes]
        x2 = jnp.concatenate([jnp.where(lo, x, 0.0).astype(BF16), jnp.where(lo, 0.0, x).astype(BF16)], 0)
        mats, ecol, wsts, et = [], [], [], []
        for hh in range(2):
            h = 2 * pair + hh
            colb = jnp.broadcast_to(acs_s[:, h:h + 1], (ln, LANES))
            decay = jnp.exp(colb - acst_s[h:h + 1, :] + mask_ref[...])
            mats.append((cb * decay * dtt_s[h:h + 1, :]).astype(BF16))
            ecol.append(jnp.exp(colb))
            wsts.append((bmt_s[rows, :] * wstt_s[h:h + 1, :]).astype(BF16))
            et.append(jnp.broadcast_to(etot[:, h:h + 1], (1, LANES)))
        diag = jnp.dot(jnp.concatenate(mats, 1), x2, preferred_element_type=F32)
        st = jnp.dot(jnp.concatenate(wsts, 1), x2, preferred_element_type=F32)
        y_ref[bi, :, lanes] = diag + y_off[:, j * LANES:(j + 1) * LANES] * jnp.where(lo, ecol[0], ecol[1])
        h_ref[rows, lanes] = h_ref[rows, lanes] * jnp.where(lo_1, et[0], et[1]) + st
        yield


def _ssd(xbc, a_grp, dt_bias, neg_a, nb=2):
    bsz, seq, _ = xbc.shape
    nc = seq // SSM_CHUNK
    nb = nb if bsz % nb == 0 else 1

    def direction(reverse):
        dt_tile = 7 if reverse else 6
        row = 1 if reverse else 0

        def cidx(c):
            return (nc - 1 - c) if reverse else c

        ins = [pl.BlockSpec((nb, SSM_CHUNK, SSM_D_INNER), lambda b, c: (b, cidx(c), 0)),
               pl.BlockSpec((nb, SSM_CHUNK, SSM_BC), lambda b, c: (b, cidx(c), SSM_D_INNER // SSM_BC)),
               pl.BlockSpec((nb, SSM_CHUNK, SSM_BC), lambda b, c: (b, cidx(c), SSM_D_INNER // SSM_BC + 1)),
               pl.BlockSpec((nb, SSM_CHUNK, LANES), lambda b, c: (b, cidx(c), dt_tile)),
               pl.BlockSpec((1, LANES), lambda b, c: (0, 0)),
               pl.BlockSpec((1, LANES), lambda b, c: (0, 0)),
               pl.BlockSpec((SSM_CHUNK, 3 * SSM_CHUNK), lambda b, c: (0, 0)),
               pl.BlockSpec((SSM_CHUNK, SSM_CHUNK), lambda b, c: (0, 0))]
        out = pl.BlockSpec((nb, SSM_CHUNK, SSM_D_INNER), lambda b, c: (b, cidx(c), 0))
        li = np.arange(SSM_CHUNK)
        keep = (li[None, :] >= li[:, None]) if reverse else (li[None, :] <= li[:, None])
        cum = jnp.asarray(np.tile(keep.astype(np.float32), (1, 3)), BF16)
        mask = jnp.asarray(np.where(keep, 0.0, -np.inf).astype(np.float32))
        return ins, out, (xbc, xbc, xbc, a_grp, dt_bias[row:row + 1], neg_a[row:row + 1], cum, mask)

    f_ins, f_out, f_args = direction(False)
    b_ins, b_out, b_args = direction(True)
    y = jax.ShapeDtypeStruct((bsz, seq, SSM_D_INNER), F32)
    per_chain = [pltpu.VMEM((SSM_BC, SSM_D_INNER), F32),
                 pltpu.VMEM((SSM_CHUNK, LANES), F32),
                 pltpu.VMEM((LANES, SSM_CHUNK), F32),
                 pltpu.VMEM((LANES, SSM_CHUNK), F32),
                 pltpu.VMEM((LANES, SSM_CHUNK), F32),
                 pltpu.VMEM((SSM_BC, SSM_CHUNK), F32),
                 pltpu.VMEM((SSM_BC, SSM_CHUNK), BF16)]
    return pl.pallas_call(
        functools.partial(_ssd_kernel, nb=nb),
        grid=(bsz // nb, nc),
        in_specs=f_ins + b_ins,
        out_specs=[f_out, b_out],
        out_shape=[y, y],
        scratch_shapes=per_chain * (2 * nb),
        compiler_params=_params("parallel", "arbitrary"),
        name="ssd",
    )(*f_args, *b_args)


def _layernorm(x, g, b, eps=1e-5):
    mu = jnp.mean(x, -1, keepdims=True)
    xc = x - mu
    var = jnp.mean(xc * xc, -1, keepdims=True)
    return xc * lax.rsqrt(var + eps) * g + b


def _merge_kernel(x_ref, o_ref, yf_ref, yb_ref, xs_ref, z_ref, u_ref, gl_ref,
                  wa_ref, ws_ref, wc_ref, wo_ref, dsk_ref, ng_ref, cg_ref, cb_ref, bc_ref,
                  l1g_ref, l1b_ref, rw_ref, rb_ref,
                  x1_ref, x1g_ref, route_ref, cnt_ref, carry_ref, *, sub):
    tm = x_ref.shape[0]

    @pl.when(pl.program_id(0) == 0)
    def _():
        carry_ref[...] = jnp.zeros_like(carry_ref)

    def rows_of(first):
        rows = pl.ds(first, sub)
        y_attn = jnp.dot(o_ref[rows, :], wa_ref[...], preferred_element_type=F32)
        z = z_ref[rows, :].astype(F32)
        ys = (yf_ref[rows, :] + yb_ref[rows, :] + xs_ref[rows, :] * dsk_ref[...]) * (z * _sigmoid(z))
        gw = SSM_D_INNER // SSM_GROUPS
        ys = jnp.concatenate(
            [_rms(ys[:, g * gw:(g + 1) * gw], ng_ref[:, g * gw:(g + 1) * gw]) for g in range(SSM_GROUPS)], -1)
        yield
        y_ssm = jnp.dot(ys.astype(BF16), ws_ref[...], preferred_element_type=F32)
        uc = _layernorm(u_ref[rows, :], cg_ref[...], cb_ref[...])
        uc = uc * _sigmoid(uc)
        yield
        y_conv = jnp.dot(uc.astype(BF16), wc_ref[...], preferred_element_type=F32) + bc_ref[...]
        gl = gl_ref[rows, :].astype(F32)
        mixed = (_sigmoid(gl[:, :D_MODEL]) * y_attn
                 + _sigmoid(gl[:, D_MODEL:2 * D_MODEL]) * y_ssm
                 + _sigmoid(gl[:, 2 * D_MODEL:]) * y_conv)
        yield
        mixed = jnp.dot(mixed.astype(BF16), wo_ref[...], preferred_element_type=F32)
        x1 = _layernorm(DN_ALPHA * x_ref[rows, :] + mixed, l1g_ref[...], l1b_ref[...])
        x1_ref[rows, :] = x1
        for t in range(ROW_TILES):
            x1g_ref[pl.ds(first * ROW_TILES + t, sub, stride=ROW_TILES), :] = x1[:, t * LANES:(t + 1) * LANES]
        yield
        x_hi = x1.astype(BF16)
        x_lo = (x1 - x_hi.astype(F32)).astype(BF16)
        lg = (jnp.dot(jnp.concatenate([x_hi, x_lo, x_hi], -1), rw_ref[...], preferred_element_type=F32)
              + rb_ref[...])
        lane = lax.broadcasted_iota(jnp.int32, (sub, LANES), 1).astype(F32)
        sels, vals, idxs = [], [], []
        for _ in range(TOP_K):
            m = jnp.max(lg, -1, keepdims=True)
            idx = jnp.min(jnp.where(lg == m, lane, float(LANES)), -1, keepdims=True)
            sel = lane == idx
            lg = jnp.where(sel, NEG_BIG * 2, lg)
            sels.append(sel)
            vals.append(m)
            idxs.append(idx)
        yield
        es = [jnp.exp(v - vals[0]) for v in vals]
        den = es[0] + es[1] + es[2] + es[3]
        hot = jnp.zeros((sub, LANES), F32)
        for sel in sels:
            hot = hot + sel.astype(F32)
        r = lax.broadcasted_iota(jnp.int32, (sub, sub), 0)
        c = lax.broadcasted_iota(jnp.int32, (sub, sub), 1)
        excl = jnp.dot((c < r).astype(BF16), hot.astype(BF16), preferred_element_type=F32) + carry_ref[...]
        carry_ref[...] = carry_ref[...] + jnp.sum(hot, 0, keepdims=True)
        route = jnp.zeros((sub, LANES), F32)
        for kk in range(TOP_K):
            rank = jnp.sum(jnp.where(sels[kk], excl, 0.0), -1, keepdims=True)
            route = jnp.where(lane == float(kk), idxs[kk], route)
            route = jnp.where(lane == float(TOP_K + kk), rank, route)
            route = jnp.where(lane == float(2 * TOP_K + kk), es[kk] / den, route)
        route_ref[rows, :] = route
        yield

    for _ in zip(*[rows_of(first) for first in range(0, tm, sub)]):
        pass
    cnt_ref[...] = carry_ref[...]


def _merge(x, o, yf, yb, xbc, z, u, gl, wa, ws, wc, wo, dsk, ng, cg, cb, bc, l1g, l1b, rw, rb, tm=512, sub=256):
    t = x.shape[0]
    tm = min(tm, t)

    def rowb(width, col=0):
        return pl.BlockSpec((tm, width), lambda i: (i, col))

    def const(shape):
        return pl.BlockSpec(shape, lambda i: (0, 0))

    return pl.pallas_call(
        functools.partial(_merge_kernel, sub=min(sub, tm)),
        grid=(t // tm,),
        in_specs=[rowb(D_MODEL), rowb(MLA_HEADS * MLA_DV), rowb(SSM_D_INNER), rowb(SSM_D_INNER), rowb(SSM_D_INNER),
                  rowb(SSM_D_INNER), rowb(CNV_CH), rowb(3 * D_MODEL),
                  const((MLA_HEADS * MLA_DV, D_MODEL)), const((SSM_D_INNER, D_MODEL)), const((CNV_CH, D_MODEL)),
                  const((D_MODEL, D_MODEL)), const((1, SSM_D_INNER)), const((1, SSM_D_INNER)),
                  const((1, CNV_CH)), const((1, CNV_CH)), const((1, D_MODEL)),
                  const((1, D_MODEL)), const((1, D_MODEL)), const((3 * D_MODEL, LANES)), const((1, LANES))],
        out_specs=[rowb(D_MODEL),
                   pl.BlockSpec((tm * ROW_TILES, LANES), lambda i: (i, 0)),
                   rowb(LANES), const((1, LANES))],
        out_shape=[jax.ShapeDtypeStruct((t, D_MODEL), F32),
                   jax.ShapeDtypeStruct((t * ROW_TILES, LANES), F32),
                   jax.ShapeDtypeStruct((t, LANES), F32), jax.ShapeDtypeStruct((1, LANES), F32)],
        scratch_shapes=[pltpu.VMEM((1, LANES), F32)],
        compiler_params=_params("arbitrary"),
        name="merge",
    )(x, o, yf, yb, xbc, z, u, gl, wa, ws, wc, wo, dsk, ng, cg, cb, bc, l1g, l1b, rw, rb)


def _row_slice(ref, row):
    return ref.at[pl.ds(pl.multiple_of(row * ROW_TILES, ROW_TILES), ROW_TILES)]


def _from_row_tiles(ref, first, rows, pitch):
    return jnp.concatenate([ref[pl.ds(first + t, rows, stride=pitch), :] for t in range(ROW_TILES)], -1)


def _fill_padding(fill_start_ref, fill_n_ref, zero_ref, xs_hbm, sem):
    zero_ref[...] = jnp.zeros_like(zero_ref)
    bits = [1 << b for b in reversed(range((MOE_ROWS - 1).bit_length()))]

    def copies(e):
        n_e = fill_n_ref[e]
        for bit in bits:
            done = n_e & ~(2 * bit - 1)
            yield (n_e & bit) != 0, pltpu.make_async_copy(
                zero_ref.at[pl.ds(0, bit * ROW_TILES)],
                xs_hbm.at[pl.ds(pl.multiple_of((fill_start_ref[e] + done) * ROW_TILES, ROW_TILES),
                                bit * ROW_TILES)], sem)

    def start(e, carry):
        for cond, cp in copies(e):
            pl.when(cond)(cp.start)
        return carry

    def wait(e, carry):
        for cond, cp in copies(e):
            pl.when(cond)(cp.wait)
        return carry

    def tail(j):
        first = (fill_start_ref[N_EXPERTS] * ROW_TILES + j * zero_ref.shape[0])
        return pltpu.make_async_copy(zero_ref, xs_hbm.at[pl.ds(pl.multiple_of(first, ROW_TILES),
                                                               zero_ref.shape[0])], sem)

    def tail_start(j, carry):
        tail(j).start()
        return carry

    def tail_wait(j, carry):
        tail(j).wait()
        return carry

    lax.fori_loop(0, N_EXPERTS, start, 0)
    lax.fori_loop(0, fill_n_ref[N_EXPERTS], tail_start, 0)
    lax.fori_loop(0, N_EXPERTS, wait, 0)
    lax.fori_loop(0, fill_n_ref[N_EXPERTS], tail_wait, 0)


def _dispatch_kernel(fill_start_ref, fill_n_ref, dest_ref, x_ref, xs_hbm, sem, fill_sem, zero_ref, *, tm):
    @pl.when(pl.program_id(0) == pl.num_programs(0) - 1)
    def _():
        _fill_padding(fill_start_ref, fill_n_ref, zero_ref, xs_hbm, fill_sem)

    def body(c, carry):
        for u in range(ISSUE_UNROLL):
            t = c * ISSUE_UNROLL + u
            src = _row_slice(x_ref, t)
            for kk in range(TOP_K):
                pltpu.make_async_copy(src, _row_slice(xs_hbm, dest_ref[0, 0, t * TOP_K + kk]),
                                      sem).start(priority=kk % 2)
        return carry

    lax.fori_loop(0, tm // ISSUE_UNROLL, body, 0)
    for _ in range(TOP_K):
        pltpu.make_async_copy(x_ref, xs_hbm.at[pl.ds(0, tm * ROW_TILES)], sem).wait()


def _dispatch(dest, fill_start, fill_n, x1g, n_rows, tm=1024):
    t = x1g.shape[0] // ROW_TILES
    tm = min(tm, t)
    nt = t // tm
    grid_spec = pltpu.PrefetchScalarGridSpec(
        num_scalar_prefetch=2,
        grid=(nt,),
        in_specs=[pl.BlockSpec((1, 1, tm * TOP_K), lambda i, fs, fn: (i, 0, 0), memory_space=pltpu.SMEM),
                  pl.BlockSpec((tm * ROW_TILES, LANES), lambda i, fs, fn: (i, 0))],
        out_specs=pl.BlockSpec(memory_space=pl.ANY),
        scratch_shapes=[pltpu.SemaphoreType.DMA(()), pltpu.SemaphoreType.DMA(()),
                        pltpu.VMEM((MOE_ROWS // 2 * ROW_TILES, LANES), F32)],
    )
    return pl.pallas_call(
        functools.partial(_dispatch_kernel, tm=tm),
        grid_spec=grid_spec,
        out_shape=jax.ShapeDtypeStruct((n_rows * ROW_TILES, LANES), F32),
        compiler_params=_params("arbitrary"),
        name="dispatch",
    )(fill_start, fill_n, dest.reshape(nt, 1, tm * TOP_K), x1g)


def _moe_kernel(blk_e_ref, nused_ref, x_ref, wgu_ref, bgu_ref, wdn_ref, bdn_ref, o_ref, wgu_b, wdn_b):
    i = pl.program_id(0)

    @pl.when((i == 0) | (blk_e_ref[i] != blk_e_ref[jnp.maximum(i - 1, 0)]))
    def _():
        wgu_b[...] = wgu_ref[0, 0].astype(BF16)
        wdn_b[...] = wdn_ref[0, 0].astype(BF16)

    @pl.when(i < nused_ref[0])
    def _():
        xb = _from_row_tiles(x_ref, 0, MOE_ROWS, ROW_TILES).astype(BF16)
        h = jnp.dot(xb, wgu_b[...], preferred_element_type=F32) + bgu_ref[0]
        gate = jnp.minimum(h[:, :D_FF], SWIGLU_LIMIT)
        up = jnp.clip(h[:, D_FF:], -SWIGLU_LIMIT, SWIGLU_LIMIT)
        act = (up + 1.0) * (gate * _sigmoid(SWIGLU_ALPHA * gate))
        y = jnp.dot(act.astype(BF16), wdn_b[...], preferred_element_type=F32) + bdn_ref[0]
        for t in range(ROW_TILES):
            o_ref[pl.ds(t, MOE_ROWS, stride=ROW_TILES), :] = y[:, t * LANES:(t + 1) * LANES]

    @pl.when(i >= nused_ref[0])
    def _():
        o_ref[...] = jnp.zeros_like(o_ref)


def _moe_experts(blk_e, nused, xs, wgu, bgu, wdn, bdn, layer):
    n_blocks = blk_e.shape[0]
    grid_spec = pltpu.PrefetchScalarGridSpec(
        num_scalar_prefetch=2,
        grid=(n_blocks,),
        in_specs=[pl.BlockSpec((MOE_ROWS * ROW_TILES, LANES), lambda i, be, nu: (jnp.minimum(i, nu[0] - 1), 0)),
                  pl.BlockSpec((1, 1, D_MODEL, 2 * D_FF), lambda i, be, nu: (layer, be[i], 0, 0)),
                  pl.BlockSpec((1, 1, 2 * D_FF), lambda i, be, nu: (be[i], 0, 0)),
                  pl.BlockSpec((1, 1, D_FF, D_MODEL), lambda i, be, nu: (layer, be[i], 0, 0)),
                  pl.BlockSpec((1, 1, D_MODEL), lambda i, be, nu: (be[i], 0, 0))],
        out_specs=pl.BlockSpec((MOE_ROWS * ROW_TILES, LANES), lambda i, be, nu: (i, 0)),
        scratch_shapes=[pltpu.VMEM((D_MODEL, 2 * D_FF), BF16), pltpu.VMEM((D_FF, D_MODEL), BF16)],
    )
    return pl.pallas_call(
        _moe_kernel,
        grid_spec=grid_spec,
        out_shape=jax.ShapeDtypeStruct(xs.shape, F32),
        compiler_params=_params("arbitrary"),
        name="moe_experts",
    )(blk_e, nused, xs, wgu, bgu, wdn, bdn)


def _gather_topk_rows(dest_ref, yb_hbm, dst, sem, tm):
    def body(c, carry):
        for u in range(ISSUE_UNROLL):
            t = c * ISSUE_UNROLL + u
            for kk in range(TOP_K):
                pltpu.make_async_copy(_row_slice(yb_hbm, dest_ref[0, 0, t * TOP_K + kk]),
                                      _row_slice(dst, kk * tm + t), sem).start(priority=kk % 2)
        return carry
    lax.fori_loop(0, tm // ISSUE_UNROLL, body, 0)


def _combine_kernel(dest_first_ref, dest_next_ref, yb_hbm, x1_ref, route_ref, g_ref, b_ref, x2_ref, x2b_ref,
                    buf, sem, *, tm):
    i = pl.program_id(0)
    n = pl.num_programs(0)
    slot = i % 2

    @pl.when(i == 0)
    def _():
        _gather_topk_rows(dest_first_ref, yb_hbm, buf.at[0], sem.at[0], tm)

    @pl.when(i + 1 < n)
    def _():
        _gather_topk_rows(dest_next_ref, yb_hbm, buf.at[1 - slot], sem.at[1 - slot], tm)

    pltpu.make_async_copy(yb_hbm.at[pl.ds(0, tm * TOP_K * ROW_TILES)], buf.at[slot], sem.at[slot]).wait()
    route = route_ref[...]
    ffn = jnp.zeros((tm, D_MODEL), F32)
    for kk in range(TOP_K):
        rows = _from_row_tiles(buf.at[slot], kk * tm * ROW_TILES, tm, ROW_TILES)
        ffn = ffn + rows * route[:, 2 * TOP_K + kk:2 * TOP_K + kk + 1]
    x2 = _layernorm(DN_ALPHA * x1_ref[...] + ffn, g_ref[...], b_ref[...])
    x2_ref[...] = x2
    x2b_ref[...] = x2.astype(BF16)


def _combine(dest, yb, x1, route, g, b, tm=512):
    t = x1.shape[0]
    tm = min(tm, t)
    nt = t // tm
    dest3 = dest.reshape(nt, 1, tm * TOP_K)
    return pl.pallas_call(
        functools.partial(_combine_kernel, tm=tm),
        grid=(nt,),
        in_specs=[pl.BlockSpec((1, 1, tm * TOP_K), lambda i: (0, 0, 0), memory_space=pltpu.SMEM),
                  pl.BlockSpec((1, 1, tm * TOP_K), lambda i: (jnp.minimum(i + 1, nt - 1), 0, 0),
                               memory_space=pltpu.SMEM),
                  pl.BlockSpec(memory_space=pl.ANY),
                  pl.BlockSpec((tm, D_MODEL), lambda i: (i, 0)),
                  pl.BlockSpec((tm, LANES), lambda i: (i, 0)),
                  pl.BlockSpec((1, D_MODEL), lambda i: (0, 0)),
                  pl.BlockSpec((1, D_MODEL), lambda i: (0, 0))],
        out_specs=[pl.BlockSpec((tm, D_MODEL), lambda i: (i, 0)),
                   pl.BlockSpec((tm, D_MODEL), lambda i: (i, 0))],
        out_shape=[jax.ShapeDtypeStruct((t, D_MODEL), F32), jax.ShapeDtypeStruct((t, D_MODEL), BF16)],
        scratch_shapes=[pltpu.VMEM((2, tm * TOP_K * ROW_TILES, LANES), F32),
                        pltpu.SemaphoreType.DMA((2,))],
        compiler_params=_params("arbitrary"),
        name="combine",
    )(dest3, dest3, yb, x1, route, g, b)


def _pad_cols(w, width):
    return jnp.pad(w, ((0, 0), (0, width - w.shape[1])))


def _pack_in_proj(w_in, b_in):
    offs = np.concatenate([[0], np.cumsum(IN_SIZES)])
    wb = jnp.concatenate([w_in, b_in[None, :]], 0)

    def piece(i):
        return wb[:, offs[i]:offs[i + 1]]

    kr = piece(2)
    kr_swapped = jnp.concatenate([kr[:, MLA_DR // 2:], kr[:, :MLA_DR // 2]], 1)
    dt = piece(5)
    grp_a = jnp.concatenate([piece(0), piece(1), _pad_cols(jnp.concatenate([kr, kr_swapped], 1), LANES),
                             _pad_cols(dt[:, :SSM_HEADS], LANES), _pad_cols(dt[:, SSM_HEADS:], LANES)], 1)
    groups = {"a": (grp_a, F32), "z": (piece(3), BF16), "xbc": (piece(4), F32),
              "cnv": (jnp.concatenate([piece(6), piece(7)], 1), F32), "gate": (piece(8), BF16)}
    return {k: (v[:-1].astype(BF16), v[-1:], dt_) for k, (v, dt_) in groups.items()}


def _pack_mla(w_uq, w_ukv, w_br_attn, seq):
    hq = MLA_DN + MLA_DR
    wq = w_uq.reshape(MLA_Q_LORA, MLA_HEADS, hq)
    zq = jnp.zeros((MLA_Q_LORA, MLA_HEADS, HEAD_PAD - hq), F32)
    wq_main = jnp.concatenate([wq, zq], -1).reshape(MLA_Q_LORA, MLA_W)
    wkv = w_ukv.reshape(MLA_KV_LORA, MLA_HEADS, MLA_DN + MLA_DV)
    zk = jnp.zeros((MLA_KV_LORA, MLA_HEADS, HEAD_PAD - MLA_DN), F32)
    wk = jnp.concatenate([wkv[..., :MLA_DN], zk], -1).reshape(MLA_KV_LORA, MLA_W)
    wv = jnp.concatenate([wkv[..., MLA_DN:], zk], -1).reshape(MLA_KV_LORA, MLA_W)
    e2 = np.zeros((LANES, MLA_HEADS, HEAD_PAD), np.float32)
    for j in range(MLA_DR):
        e2[j, :, MLA_DN + j] = 1.0
        e2[MLA_DR + j, :, MLA_DN + j] = 1.0
    vone = np.zeros((MLA_HEADS, HEAD_PAD), np.float32)
    vone[:, MLA_DV] = 1.0
    pos = jnp.arange(seq, dtype=F32)
    inv = ROPE_THETA ** (-jnp.arange(0, MLA_DR, 2, dtype=F32) / MLA_DR)
    ang = pos[:, None] * inv[None, :]
    cos, sin = jnp.cos(ang), jnp.sin(ang)
    scale = (MLA_DN + MLA_DR) ** -0.5 * math.log2(math.e)
    ones = jnp.ones((seq, MLA_DN), F32)
    zpad = jnp.zeros((seq, HEAD_PAD - hq), F32)
    cq = jnp.tile(jnp.concatenate([ones, cos, cos, zpad], 1) * scale, (1, MLA_HEADS))
    sq = jnp.tile(jnp.concatenate([0 * ones, -sin, sin, zpad], 1) * scale, (1, MLA_HEADS))
    tk = jnp.concatenate([cos, cos, -sin, sin, jnp.zeros((seq, LANES - 2 * MLA_DR), F32)], 1)
    return dict(wq=wq_main.astype(BF16), wk=wk.astype(BF16), wv=wv.astype(BF16),
                e2=jnp.asarray(e2.reshape(LANES, MLA_W), BF16), vone=jnp.asarray(vone.reshape(1, MLA_W)),
                wbr=w_br_attn.astype(BF16), cq=cq, sq=sq, tk=tk)


def _route_tables(route, cnt, n_tok):
    idx = route[:, :TOP_K].astype(jnp.int32)
    rank = route[:, TOP_K:2 * TOP_K].astype(jnp.int32)
    counts = cnt[0, :N_EXPERTS].astype(jnp.int32)
    n_blocks = -(-(n_tok * TOP_K + N_EXPERTS * (MOE_ROWS - 1)) // MOE_ROWS)
    padded = (counts + MOE_ROWS - 1) // MOE_ROWS * MOE_ROWS
    pad_end = jnp.cumsum(padded)
    pad_start = pad_end - padded
    onehot = idx[..., None] == jnp.arange(N_EXPERTS, dtype=jnp.int32)
    dest = (jnp.sum(jnp.where(onehot, pad_start, 0), -1) + rank).reshape(-1)
    blk_first = jnp.arange(n_blocks, dtype=jnp.int32) * MOE_ROWS
    blk_e = jnp.minimum(jnp.sum((pad_end[None, :] <= blk_first[:, None]).astype(jnp.int32), -1), N_EXPERTS - 1)
    nused = pad_end[-1:] // MOE_ROWS
    n_rows = n_blocks * MOE_ROWS
    fill_start = jnp.concatenate([pad_start + counts, pad_end[-1:]])
    fill_n = jnp.concatenate([padded - counts, (n_rows - pad_end[-1:]) // (MOE_ROWS // 2)])
    return dest, blk_e, nused, fill_start, fill_n, n_rows


def kernel(x, w_in, b_in, mla_q_norm, mla_kv_norm, mla_w_uq, mla_w_ukv, w_br_attn, ssm_conv_w, ssm_conv_b, ssm_dt_bias, ssm_a_log, ssm_d, ssm_norm, w_br_ssm, cnv_dw_w, cnv_dw_b, cnv_ln_g, cnv_ln_b, w_br_conv, b_br_conv, w_out, ln1_g, ln1_b, router_w, router_b, moe_w_gate_up, moe_b_gate_up, moe_w_down, moe_b_down, ln2_g, ln2_b):
    bsz, seq, d = x.shape
    n_tok = bsz * seq
    xf = x.reshape(n_tok, d)
    xb = xf.astype(BF16)
    for l in range(DEPTH):
        proj = _pack_in_proj(w_in[l], b_in[l])
        mla = _pack_mla(mla_w_uq[l], mla_w_ukv[l], w_br_attn[l], seq)
        a_grp = _linear(xb, *proj["a"])
        z = _linear(xb, *proj["z"])
        xbc_raw = _linear(xb, *proj["xbc"], tn=768)
        cnv_raw = _linear(xb, *proj["cnv"])
        gate_logits = _linear(xb, *proj["gate"], tn=1024)

        q, k, v = _mla_prep(a_grp, seq, mla_q_norm[l][None], mla_kv_norm[l][None], mla["wq"],
                            mla["wk"], mla["wv"], mla["e2"], mla["cq"], mla["sq"], mla["tk"], mla["vone"])
        attn = _attention(q.reshape(bsz, seq, MLA_W), k.reshape(bsz, seq, MLA_W), v.reshape(bsz, seq, MLA_W))

        xbc = _dwconv(xbc_raw.reshape(bsz, seq, SSM_CONV_DIM), ssm_conv_w[l], ssm_conv_b[l],
                      glu=False, silu_out=True, rows=256)
        a3 = a_grp.reshape(bsz, seq, GROUP_A)
        dtb = jnp.pad(ssm_dt_bias[l], ((0, 0), (0, LANES - SSM_HEADS)))
        nega = jnp.pad(-jnp.exp(ssm_a_log[l]), ((0, 0), (0, LANES - SSM_HEADS)))
        y_fwd, y_bwd = _ssd(xbc, a3, dtb, nega)

        u = _dwconv(cnv_raw.reshape(bsz, seq, 2 * CNV_CH), cnv_dw_w[l], cnv_dw_b[l], glu=True, silu_out=False,
                    rows=128)

        rw = jnp.pad(router_w[l], ((0, 0), (0, LANES - N_EXPERTS)))
        rw_hi = rw.astype(BF16)
        rw_lo = (rw - rw_hi.astype(F32)).astype(BF16)
        rw = jnp.concatenate([rw_hi, rw_hi, rw_lo], 0)
        rb = jnp.pad(router_b[l], (0, LANES - N_EXPERTS), constant_values=NEG_BIG)[None]
        x1, x1g, route, cnt = _merge(
            xf, attn.reshape(n_tok, MLA_HEADS * MLA_DV), y_fwd.reshape(n_tok, SSM_D_INNER), y_bwd.reshape(n_tok, SSM_D_INNER),
            xbc.reshape(n_tok, SSM_CONV_DIM), z, u.reshape(n_tok, CNV_CH), gate_logits,
            mla["wbr"], w_br_ssm[l].astype(BF16), w_br_conv[l].astype(BF16), w_out[l].astype(BF16),
            jnp.repeat(ssm_d[l], SSM_HEAD_DIM)[None], ssm_norm[l][None], cnv_ln_g[l][None], cnv_ln_b[l][None],
            b_br_conv[l][None], ln1_g[l][None], ln1_b[l][None], rw, rb)

        dest, blk_e, nused, fill_start, fill_n, n_rows = _route_tables(route, cnt, n_tok)
        xs = _dispatch(dest, fill_start, fill_n, x1g, n_rows)
        yb = _moe_experts(blk_e, nused, xs, moe_w_gate_up, moe_b_gate_up[l][:, None, :], moe_w_down,
                          moe_b_down[l][:, None, :], l)
        xf, xb = _combine(dest, yb, x1, route, ln2_g[l][None], ln2_b[l][None])
    return xf.reshape(bsz, seq, d)
```
